```python
import math
import jax, jax.numpy as jnp
from jax import lax
import numpy as np

D_MODEL = 1024
BATCH = 8
SEQ = 4096
DEPTH = 1

CTX_LEN = 256
GRID_W = 64
EPS = 1e-6
DA_HEADS = 4
DA_HEAD_DIM = 64
DA_V_DIM = 2 * DA_HEAD_DIM
DA_WIDTH = DA_HEADS * DA_V_DIM
Q_BLOCK = 128
ROPE_THETA = 10000.0
GM_HEADS = 4
GM_HEAD_DIM = 128
GM_WIDTH = GM_HEADS * GM_HEAD_DIM
CHUNK = 128
MIX_WIDTH = DA_WIDTH + GM_WIDTH
IN_WIDTH = 3 * DA_WIDTH + 2 * GM_WIDTH
N_EXPERTS = 32
TOP_K = 4
N_GROUPS = 4
TOPK_GROUPS = 2
D_EXPERT = 256
D_SHARED = 256
ROUTED_SCALE = 2.5
MOE_BLOCK = 128

kernel_name = "hybrid_diffattn_chunkgmlp_moe_dit"


def rms_norm(x, g):
    xf = x.astype(jnp.float32)
    y = xf * lax.rsqrt(jnp.mean(xf * xf, axis=-1, keepdims=True) + EPS)
    return (y * g.astype(jnp.float32)).astype(x.dtype)


def layer_norm(x, g, b):
    xf = x.astype(jnp.float32)
    mu = jnp.mean(xf, axis=-1, keepdims=True)
    xc = xf - mu
    var = jnp.mean(xc * xc, axis=-1, keepdims=True)
    y = xc * lax.rsqrt(var + EPS) * g.astype(jnp.float32) + b.astype(jnp.float32)
    return y.astype(x.dtype)


def adaln_params(cond, w_ada, b_ada):
    m = jax.nn.silu(cond) @ w_ada + b_ada
    return jnp.split(m, 6, axis=-1)


def modulate(h, shift, scale):
    return h * (1.0 + scale) + shift


def axial_rope_tables(n_tokens):
    rows = n_tokens // GRID_W
    row = jnp.repeat(jnp.arange(rows, dtype=jnp.float32), GRID_W)
    col = jnp.tile(jnp.arange(GRID_W, dtype=jnp.float32), rows)
    half = DA_HEAD_DIM // 2
    inv_freq = ROPE_THETA ** (-jnp.arange(0, half, 2, dtype=jnp.float32) / half)
    ang = jnp.concatenate([row[:, None] * inv_freq, col[:, None] * inv_freq], axis=-1)
    return jnp.cos(ang), jnp.sin(ang)


def apply_rope(x, cos, sin):
    xf = x.astype(jnp.float32)
    x1, x2 = xf[..., 0::2], xf[..., 1::2]
    out = jnp.stack([x1 * cos - x2 * sin, x1 * sin + x2 * cos], axis=-1).reshape(x.shape)
    return out.astype(x.dtype)


def split_proj(h, w_in):
    B, L, _ = h.shape
    p = h @ w_in
    q = p[..., :DA_WIDTH].reshape(B, L, DA_HEADS, 2, DA_HEAD_DIM)
    k = p[..., DA_WIDTH:2 * DA_WIDTH].reshape(B, L, DA_HEADS, 2, DA_HEAD_DIM)
    v = p[..., 2 * DA_WIDTH:3 * DA_WIDTH].reshape(B, L, DA_HEADS, DA_V_DIM)
    z = jax.nn.gelu(p[..., 3 * DA_WIDTH:], approximate=False).reshape(B, L, 2, GM_HEADS, GM_HEAD_DIM)
    q = jnp.transpose(q, (0, 2, 3, 1, 4))
    k = jnp.transpose(k, (0, 2, 3, 1, 4))
    v = jnp.transpose(v, (0, 2, 1, 3))
    return q, k, v, z[:, :, 0], z[:, :, 1]


def diff_attend(q, k, v, lam):
    s = jnp.einsum('bhmqd,bhmkd->bhmqk', q, k).astype(jnp.float32) * (DA_HEAD_DIM ** -0.5)
    p = jax.nn.softmax(s, axis=-1)
    a = p[:, :, 0] - lam * p[:, :, 1]
    return jnp.einsum('bhqk,bhkv->bhqv', a.astype(v.dtype), v)


def da_output(o, subln_g, lambda_init):
    B, H, L, dv = o.shape
    o = rms_norm(o, subln_g) * (1.0 - lambda_init)
    return jnp.transpose(o, (0, 2, 1, 3)).reshape(B, L, H * dv)


def chunk_gmlp(u, vg, ln_g, ln_b, w_s, b_s, out_g):
    B, L, G, dh = vg.shape
    vn = layer_norm(vg, ln_g, ln_b)
    vc = vn.reshape(B, L // CHUNK, CHUNK, G, dh)
    mixed = jnp.einsum('gpq,bnqgd->bnpgd', w_s, vc) + b_s.T[:, :, None]
    y = u * mixed.reshape(B, L, G, dh)
    return rms_norm(y, out_g).reshape(B, L, G * dh)


def moe_ffn(h, w_router, router_bias, we_gate, we_up, we_down, ws_gate, ws_up, ws_down):
    shape = h.shape
    t = h.reshape(-1, D_MODEL)
    scores = jax.nn.sigmoid((t @ w_router).astype(jnp.float32))
    biased = scores + router_bias.astype(jnp.float32)
    grp = biased.reshape(-1, N_GROUPS, N_EXPERTS // N_GROUPS)
    grp_score = jnp.sum(lax.top_k(grp, 2)[0], axis=-1)
    _, gidx = lax.top_k(grp_score, TOPK_GROUPS)
    gmask = jnp.sum(jax.nn.one_hot(gidx, N_GROUPS, dtype=jnp.float32), axis=1)
    emask = jnp.repeat(gmask, N_EXPERTS // N_GROUPS, axis=1)
    masked = jnp.where(emask > 0, biased, -jnp.inf)
    _, eidx = lax.top_k(masked, TOP_K)
    w = jnp.take_along_axis(scores, eidx, axis=-1)
    w = w / jnp.sum(w, axis=-1, keepdims=True) * ROUTED_SCALE
    gates = jnp.sum(jax.nn.one_hot(eidx, N_EXPERTS, dtype=jnp.float32) * w[..., None], axis=1)

    tb = t.reshape(-1, MOE_BLOCK, D_MODEL)
    gb = gates.reshape(-1, MOE_BLOCK, N_EXPERTS)

    def block(args):
        xb, g = args
        a = jnp.einsum('td,edf->tef', xb, we_gate)
        b = jnp.einsum('td,edf->tef', xb, we_up)
        hm = jax.nn.silu(a) * b * g[..., None].astype(xb.dtype)
        return jnp.einsum('tef,efd->td', hm, we_down)

    routed = lax.map(block, (tb, gb)).reshape(t.shape)
    shared = (jax.nn.silu(t @ ws_gate) * (t @ ws_up)) @ ws_down
    return (routed + shared).reshape(shape)


def setup_inputs(seed: int = 0) -> dict:
    key = jax.random.key(seed)
    ks = jax.random.split(key, 32)
    D = D_MODEL

    def nrm(k, shape, scale):
        return jax.random.normal(k, shape, jnp.float32) * scale

    return {
        "x": nrm(ks[0], (BATCH, SEQ, D), 1.0),
        "c": nrm(ks[1], (BATCH, D), 1.0),
        "ctx": nrm(ks[2], (BATCH, CTX_LEN, D), 1.0),
        "c_ctx": nrm(ks[3], (D,), 1.0),
        "w_ada": nrm(ks[4], (DEPTH, D, 6 * D), 0.5 * D ** -0.5),
        "b_ada": nrm(ks[5], (DEPTH, 6 * D), 0.02),
        "norm_mix_g": 1.0 + nrm(ks[6], (DEPTH, D), 0.05),
        "w_in": nrm(ks[7], (DEPTH, D, IN_WIDTH), D ** -0.5),
        "q_norm_g": 1.0 + nrm(ks[8], (DEPTH, DA_HEAD_DIM), 0.05),
        "k_norm_g": 1.0 + nrm(ks[9], (DEPTH, DA_HEAD_DIM), 0.05),
        "da_lambda": nrm(ks[10], (DEPTH, 4, DA_HEAD_DIM), 0.1),
        "subln_g": 1.0 + nrm(ks[11], (DEPTH, DA_V_DIM), 0.05),
        "gm_ln_g": 1.0 + nrm(ks[12], (DEPTH, GM_HEADS, GM_HEAD_DIM), 0.05),
        "gm_ln_b": nrm(ks[13], (DEPTH, GM_HEADS, GM_HEAD_DIM), 0.02),
        "gm_ws": nrm(ks[14], (DEPTH, GM_HEADS, CHUNK, CHUNK), CHUNK ** -0.5),
        "gm_bs": 1.0 + nrm(ks[15], (DEPTH, GM_HEADS, CHUNK), 0.1),
        "gm_out_g": 1.0 + nrm(ks[16], (DEPTH, GM_HEADS, GM_HEAD_DIM), 0.05),
        "w_out": nrm(ks[17], (DEPTH, MIX_WIDTH, D), MIX_WIDTH ** -0.5),
        "norm_ffn_g": 1.0 + nrm(ks[18], (DEPTH, D), 0.05),
        "w_router": nrm(ks[19], (DEPTH, D, N_EXPERTS), D ** -0.5),
        "router_bias": nrm(ks[20], (DEPTH, N_EXPERTS), 0.01),
        "we_gate": nrm(ks[21], (DEPTH, N_EXPERTS, D, D_EXPERT), D ** -0.5),
        "we_up": nrm(ks[22], (DEPTH, N_EXPERTS, D, D_EXPERT), D ** -0.5),
        "we_down": nrm(ks[23], (DEPTH, N_EXPERTS, D_EXPERT, D), D_EXPERT ** -0.5),
        "ws_gate": nrm(ks[24], (DEPTH, D, D_SHARED), D ** -0.5),
        "ws_up": nrm(ks[25], (DEPTH, D, D_SHARED), D ** -0.5),
        "ws_down": nrm(ks[26], (DEPTH, D_SHARED, D), D_SHARED ** -0.5),
    }


def reference(x, c, ctx, c_ctx, w_ada, b_ada, norm_mix_g, w_in, q_norm_g, k_norm_g, da_lambda,
              subln_g, gm_ln_g, gm_ln_b, gm_ws, gm_bs, gm_out_g, w_out, norm_ffn_g, w_router,
              router_bias, we_gate, we_up, we_down, ws_gate, ws_up, ws_down):
    B, L, _ = x.shape
    cos, sin = axial_rope_tables(L)
    nb = L // Q_BLOCK

    for l in range(DEPTH):
        lambda_init = 0.8 - 0.6 * math.exp(-0.3 * l)
        lp = da_lambda[l].astype(jnp.float32)
        lam = jnp.exp(jnp.sum(lp[0] * lp[1])) - jnp.exp(jnp.sum(lp[2] * lp[3])) + lambda_init

        sh_m, sc_m, g_m, sh_f, sc_f, g_f = [m[:, None, :] for m in adaln_params(c, w_ada[l], b_ada[l])]
        csh_m, csc_m, cg_m, csh_f, csc_f, cg_f = adaln_params(c_ctx, w_ada[l], b_ada[l])

        hx = modulate(rms_norm(x, norm_mix_g[l]), sh_m, sc_m)
        hc = modulate(rms_norm(ctx, norm_mix_g[l]), csh_m, csc_m)
        qx, kx, vx, ux, gx = split_proj(hx, w_in[l])
        qc, kc, vc, uc, gc = split_proj(hc, w_in[l])
        qx = apply_rope(rms_norm(qx, q_norm_g[l]), cos, sin)
        kx = apply_rope(rms_norm(kx, k_norm_g[l]), cos, sin)
        qc = rms_norm(qc, q_norm_g[l])
        kc = rms_norm(kc, k_norm_g[l])

        k_all = jnp.concatenate([kc, kx], axis=3)
        v_all = jnp.concatenate([vc, vx], axis=2)
        q_blocks = jnp.moveaxis(qx.reshape(B, DA_HEADS, 2, nb, Q_BLOCK, DA_HEAD_DIM), 3, 0)
        o = lax.map(lambda qb: diff_attend(qb, k_all, v_all, lam), q_blocks)
        o = jnp.moveaxis(o, 0, 2).reshape(B, DA_HEADS, L, DA_V_DIM)
        attn_x = da_output(o, subln_g[l], lambda_init)
        gm_x = chunk_gmlp(ux, gx, gm_ln_g[l], gm_ln_b[l], gm_ws[l], gm_bs[l], gm_out_g[l])
        mix_x = jnp.concatenate([attn_x, gm_x], axis=-1) @ w_out[l]

        if l < DEPTH - 1:
            attn_c = da_output(diff_attend(qc, kc, vc, lam), subln_g[l], lambda_init)
            gm_c = chunk_gmlp(uc, gc, gm_ln_g[l], gm_ln_b[l], gm_ws[l], gm_bs[l], gm_out_g[l])
            mix_c = jnp.concatenate([attn_c, gm_c], axis=-1) @ w_out[l]
            ctx = ctx + cg_m * mix_c

        x = x + g_m * mix_x

        fx = modulate(rms_norm(x, norm_ffn_g[l]), sh_f, sc_f)
        x = x + g_f * moe_ffn(fx, w_router[l], router_bias[l], we_gate[l], we_up[l], we_down[l],
                              ws_gate[l], ws_up[l], ws_down[l])
        if l < DEPTH - 1:
            fc = modulate(rms_norm(ctx, norm_ffn_g[l]), csh_f, csc_f)
            ctx = ctx + cg_f * moe_ffn(fc, w_router[l], router_bias[l], we_gate[l], we_up[l],
                                       we_down[l], ws_gate[l], ws_up[l], ws_down[l])
    return x
```

```python
import functools
import math

import jax
import jax.numpy as jnp
from jax import lax
from jax.experimental import pallas as pl
from jax.experimental.pallas import tpu as pltpu

BF = jnp.bfloat16
F32 = jnp.float32

EPS = 1e-6
GRID_W = 64
DA_HEADS = 4
DA_HEAD_DIM = 64
DA_V_DIM = 128
DA_WIDTH = 512
GM_HEADS = 4
GM_HEAD_DIM = 128
GM_WIDTH = 512
CHUNK = 128
ROPE_THETA = 10000.0
N_EXPERTS = 32
TOP_K = 4
N_GROUPS = 4
TOPK_GROUPS = 2
GROUP_SIZE = N_EXPERTS // N_GROUPS
ROUTED_SCALE = 2.5
LANES = 128
VMEM_LIMIT = 56 * 1024 * 1024


def _sigmoid(x):
    return 1.0 / (1.0 + jnp.exp(-x))


def _dot(a, b):
    return jnp.dot(a, b, preferred_element_type=F32)


def _dot_nt(a, b):
    return lax.dot_general(a, b, (((1,), (1,)), ((), ())), preferred_element_type=F32)


def _rms_rows(x, g):
    ms = jnp.mean(x * x, axis=-1, keepdims=True)
    return x * lax.rsqrt(ms + EPS) * g


def _group_mean_sq(y, gmat):
    y2 = y * y
    hi = y2.astype(BF)
    lo = (y2 - hi.astype(F32)).astype(BF)
    return (_dot(hi, gmat) + _dot(lo, gmat)) * (1.0 / DA_HEAD_DIM)


def _swap_pairs(y):
    lane = lax.broadcasted_iota(jnp.int32, y.shape, 1)
    nxt = pltpu.roll(y, LANES - 1, 1)
    prv = pltpu.roll(y, 1, 1)
    return jnp.where((lane & 1) == 0, nxt, prv)


def _ada_kernel(cond_ref, w_ref, b_ref, o_ref):
    c = cond_ref[...]
    s = c * _sigmoid(c)
    o_ref[...] = _dot(s.astype(BF), w_ref[...].astype(BF)) + b_ref[...]


def _ada_call(cond, w_ada, b_ada):
    rows, d = cond.shape
    n = w_ada.shape[1]
    bn = 1024
    return pl.pallas_call(
        _ada_kernel,
        out_shape=jax.ShapeDtypeStruct((rows, n), F32),
        grid=(n // bn,),
        in_specs=[pl.BlockSpec((rows, d), lambda j: (0, 0)),
                  pl.BlockSpec((d, bn), lambda j: (0, j)),
                  pl.BlockSpec((1, bn), lambda j: (0, j))],
        out_specs=pl.BlockSpec((rows, bn), lambda j: (0, j)),
        compiler_params=pltpu.CompilerParams(dimension_semantics=("arbitrary",),
                                             vmem_limit_bytes=VMEM_LIMIT),
        name="ada",
    )(cond, w_ada, b_ada)


def _inproj_kernel(x_ref, mod_ref, ng_ref, w_ref, qg_ref, kg_ref, cos_ref, sin_ref, gmat_ref,
                   lng_ref, lnb_ref, ws_ref, bs_ref, og_ref,
                   q_ref, k_ref, v_ref, gm_ref):
    tm = x_ref.shape[1]
    x = x_ref[0]
    mod = mod_ref[0]
    h = _rms_rows(x, ng_ref[...]) * (1.0 + mod[1:2]) + mod[0:1]
    p = _dot(h.astype(BF), w_ref[...])

    gmat = gmat_ref[...]
    cos = cos_ref[...]
    sin = sin_ref[...]
    for j in range(DA_HEADS):
        sl = slice(j * LANES, (j + 1) * LANES)
        qj = p[:, sl]
        qn = qj * lax.rsqrt(_group_mean_sq(qj, gmat) + EPS) * qg_ref[...]
        qr = qn * cos + _swap_pairs(qn) * sin
        q_ref[0, :, sl] = (qr * (DA_HEAD_DIM ** -0.5)).astype(BF)
        kj = p[:, DA_WIDTH + j * LANES: DA_WIDTH + (j + 1) * LANES]
        kn = kj * lax.rsqrt(_group_mean_sq(kj, gmat) + EPS) * kg_ref[...]
        kr = kn * cos + _swap_pairs(kn) * sin
        k_ref[0, :, sl] = kr.astype(BF)
    v_ref[0] = p[:, 2 * DA_WIDTH:3 * DA_WIDTH].astype(BF)

    zp = p[:, 3 * DA_WIDTH:]
    z = 0.5 * zp * (1.0 + lax.erf(zp * math.sqrt(0.5)))
    for g in range(GM_HEADS):
        sl = slice(g * LANES, (g + 1) * LANES)
        u = z[:, sl]
        vg = z[:, GM_WIDTH + g * LANES: GM_WIDTH + (g + 1) * LANES]
        mu = jnp.mean(vg, axis=-1, keepdims=True)
        xc = vg - mu
        var = jnp.mean(xc * xc, axis=-1, keepdims=True)
        vn = (xc * lax.rsqrt(var + EPS) * lng_ref[:, sl] + lnb_ref[:, sl]).astype(BF)
        for cidx in range(tm // CHUNK):
            rows = slice(cidx * CHUNK, (cidx + 1) * CHUNK)
            mixed = _dot(ws_ref[g], vn[rows]) + bs_ref[g]
            y = u[rows] * mixed
            gm_ref[0, rows, sl] = _rms_rows(y, og_ref[:, sl]).astype(BF)


def _inproj_call(x, mod, ng, w_in, qg, kg, cos, sin, gmat, lng, lnb, ws, bs, og, tm):
    b, l, d = x.shape
    nw = w_in.shape[1]
    full = lambda shape: pl.BlockSpec(shape, lambda bi, i: (0,) * len(shape))
    tok = pl.BlockSpec((1, tm, DA_WIDTH), lambda bi, i: (bi, i, 0))
    out = jax.ShapeDtypeStruct((b, l, DA_WIDTH), BF)
    return pl.pallas_call(
        _inproj_kernel,
        out_shape=(out, out, out, out),
        grid=(b, l // tm),
        in_specs=[pl.BlockSpec((1, tm, d), lambda bi, i: (bi, i, 0)),
                  pl.BlockSpec((1, 6, d), lambda bi, i: (bi, 0, 0)),
                  full((1, d)), full((d, nw)), full((1, LANES)), full((1, LANES)),
                  pl.BlockSpec((tm, LANES), lambda bi, i: (i, 0)),
                  pl.BlockSpec((tm, LANES), lambda bi, i: (i, 0)),
                  full((LANES, LANES)), full((1, GM_WIDTH)), full((1, GM_WIDTH)),
                  full((GM_HEADS, CHUNK, CHUNK)), full((GM_HEADS, CHUNK, GM_HEAD_DIM)),
                  full((1, GM_WIDTH))],
        out_specs=(tok, tok, tok, tok),
        compiler_params=pltpu.CompilerParams(dimension_semantics=("arbitrary", "arbitrary"),
                                             vmem_limit_bytes=VMEM_LIMIT),
        name="inproj",
    )(x, mod, ng, w_in, qg, kg, cos, sin, gmat, lng, lnb, ws, bs, og)


def _ctxproj_kernel(x_ref, mod_ref, ng_ref, w_ref, kg_ref, gmat_ref, k_ref, v_ref):
    x = x_ref[0]
    mod = mod_ref[0]
    h = _rms_rows(x, ng_ref[...]) * (1.0 + mod[1:2]) + mod[0:1]
    p = _dot(h.astype(BF), w_ref[...])
    gmat = gmat_ref[...]
    for j in range(DA_HEADS):
        sl = slice(j * LANES, (j + 1) * LANES)
        kj = p[:, sl]
        kn = kj * lax.rsqrt(_group_mean_sq(kj, gmat) + EPS) * kg_ref[...]
        k_ref[0, :, sl] = kn.astype(BF)
    v_ref[0] = p[:, DA_WIDTH:].astype(BF)


def _ctxproj_call(ctx, mod_ctx, ng, w_kv, kg, gmat):
    b, lc, d = ctx.shape
    full = lambda shape: pl.BlockSpec(shape, lambda bi: (0,) * len(shape))
    tok = pl.BlockSpec((1, lc, DA_WIDTH), lambda bi: (bi, 0, 0))
    out = jax.ShapeDtypeStruct((b, lc, DA_WIDTH), BF)
    return pl.pallas_call(
        _ctxproj_kernel,
        out_shape=(out, out),
        grid=(b,),
        in_specs=[pl.BlockSpec((1, lc, d), lambda bi: (bi, 0, 0)),
                  full((1, 6, d)), full((1, d)), full((d, 2 * DA_WIDTH)),
                  full((1, LANES)), full((LANES, LANES))],
        out_specs=(tok, tok),
        compiler_params=pltpu.CompilerParams(dimension_semantics=("arbitrary",),
                                             vmem_limit_bytes=VMEM_LIMIT),
        name="ctxproj",
    )(ctx, mod_ctx, ng, w_kv, kg, gmat)


def _attn_kernel(lam_ref, q_ref, kx_ref, kc_ref, vx_ref, vc_ref, sg_ref, o_ref,
                 k_scr, vt_scr, s_scr, acc_scr, *, ck, out_scale):
    tq = q_ref.shape[1]
    lc = kc_ref.shape[1]
    lx = kx_ref.shape[1]
    nck = (lc + lx) // ck

    @pl.when(pl.program_id(2) == 0)
    def _stage_keys_values():
        k_scr[0:lc, :] = kc_ref[0]
        k_scr[lc:lc + lx, :] = kx_ref[0]
        for cidx in range(nck):
            lo = cidx * ck
            if lo < lc:
                blk = vc_ref[0, lo:lo + ck, :]
            else:
                blk = vx_ref[0, lo - lc:lo - lc + ck, :]
            vt_scr[cidx] = blk.astype(F32).T.astype(BF)

    q = q_ref[0]
    lane = lax.broadcasted_iota(jnp.int32, q.shape, 1)
    zero = jnp.zeros_like(q)
    qq = jnp.concatenate([jnp.where(lane < DA_HEAD_DIM, q, zero),
                          jnp.where(lane >= DA_HEAD_DIM, q, zero)], axis=0)

    def scores(cidx, mrun):
        row0 = pl.multiple_of(cidx * ck, ck)
        st = _dot_nt(k_scr[pl.ds(row0, ck), :], qq)
        s_scr[pl.ds(row0, ck), :] = st
        return jnp.maximum(mrun, jnp.max(st.reshape(ck // 8, 8, 2 * tq), axis=0))

    mrun = lax.fori_loop(0, nck, scores, jnp.full((8, 2 * tq), -jnp.inf, F32))
    m = jnp.max(mrun, axis=0, keepdims=True)

    acc_scr[...] = jnp.zeros_like(acc_scr)

    def weighted(cidx, lrun):
        row0 = pl.multiple_of(cidx * ck, ck)
        pt = jnp.exp(s_scr[pl.ds(row0, ck), :] - m)
        acc_scr[...] += _dot(vt_scr[cidx], pt.astype(BF))
        return lrun + jnp.sum(pt.reshape(ck // 8, 8, 2 * tq), axis=0)

    lrun = lax.fori_loop(0, nck, weighted, jnp.zeros((8, 2 * tq), F32))
    r = 1.0 / jnp.sum(lrun, axis=0, keepdims=True)

    acc = acc_scr[...]
    ot = acc[:, :tq] * r[:, :tq] - lam_ref[0] * (acc[:, tq:] * r[:, tq:])
    ms = jnp.mean(ot * ot, axis=0, keepdims=True)
    on = ot * lax.rsqrt(ms + EPS) * sg_ref[...] * out_scale
    o_ref[0] = on.T.astype(BF)


def _attn_call(lam, q, kx, kc, vx, vc, subln_col, tq, ck, out_scale):
    b, l, _ = q.shape
    lc = kc.shape[1]
    lk = lc + l
    assert lk % ck == 0 and lc % ck == 0 and l % tq == 0
    kern = functools.partial(_attn_kernel, ck=ck, out_scale=out_scale)
    return pl.pallas_call(
        kern,
        out_shape=jax.ShapeDtypeStruct((b, l, DA_WIDTH), BF),
        grid=(b, DA_HEADS, l // tq),
        in_specs=[pl.BlockSpec(memory_space=pltpu.SMEM),
                  pl.BlockSpec((1, tq, LANES), lambda bi, h, i: (bi, i, h)),
                  pl.BlockSpec((1, l, LANES), lambda bi, h, i: (bi, 0, h)),
                  pl.BlockSpec((1, lc, LANES), lambda bi, h, i: (bi, 0, h)),
                  pl.BlockSpec((1, l, LANES), lambda bi, h, i: (bi, 0, h)),
                  pl.BlockSpec((1, lc, LANES), lambda bi, h, i: (bi, 0, h)),
                  pl.BlockSpec((DA_V_DIM, 1), lambda bi, h, i: (0, 0))],
        out_specs=pl.BlockSpec((1, tq, LANES), lambda bi, h, i: (bi, i, h)),
        scratch_shapes=[pltpu.VMEM((lk, LANES), BF),
                        pltpu.VMEM((lk // ck, DA_V_DIM, ck), BF),
                        pltpu.VMEM((lk, 2 * tq), F32),
                        pltpu.VMEM((DA_V_DIM, 2 * tq), F32)],
        compiler_params=pltpu.CompilerParams(
            dimension_semantics=("arbitrary", "arbitrary", "arbitrary"),
            vmem_limit_bytes=VMEM_LIMIT),
        name="attn",
    )(lam, q, kx, kc, vx, vc, subln_col)


def _route(scores, rbias):
    shape = scores.shape
    lane = lax.broadcasted_iota(jnp.int32, shape, 1)
    lanef = lane.astype(F32)
    neg = jnp.full(shape, -jnp.inf, F32)
    biased = jnp.where(lane < N_EXPERTS, scores + rbias, neg)

    gscore = []
    for g in range(N_GROUPS):
        ing = (lane >= g * GROUP_SIZE) & (lane < (g + 1) * GROUP_SIZE)
        vals = jnp.where(ing, biased, neg)
        m1 = jnp.max(vals, axis=-1, keepdims=True)
        i1 = jnp.min(jnp.where(vals == m1, lanef, 1e9), axis=-1, keepdims=True)
        m2 = jnp.max(jnp.where(lanef == i1, neg, vals), axis=-1, keepdims=True)
        gscore.append(m1 + m2)

    emask = jnp.zeros(shape, F32)
    for g in range(N_GROUPS):
        beaten = jnp.zeros_like(gscore[g])
        for g2 in range(N_GROUPS):
            if g2 == g:
                continue
            beat = (gscore[g2] >= gscore[g]) if g2 < g else (gscore[g2] > gscore[g])
            beaten = beaten + jnp.where(beat, 1.0, 0.0)
        chosen = jnp.where(beaten < TOPK_GROUPS, 1.0, 0.0)
        ing = (lane >= g * GROUP_SIZE) & (lane < (g + 1) * GROUP_SIZE)
        emask = jnp.where(ing, chosen, emask)

    cur = jnp.where(emask > 0.5, biased, neg)
    sel = jnp.zeros(shape, F32)
    for _ in range(TOP_K):
        mx = jnp.max(cur, axis=-1, keepdims=True)
        idx = jnp.min(jnp.where(cur == mx, lanef, 1e9), axis=-1, keepdims=True)
        hit = lanef == idx
        sel = jnp.where(hit, 1.0, sel)
        cur = jnp.where(hit, neg, cur)
    w = sel * scores
    return w / jnp.sum(w, axis=-1, keepdims=True) * ROUTED_SCALE


def _outproj_kernel(x_ref, at_ref, gm_ref, mod_ref, wo_ref, nfg_ref, wr_ref, rb_ref,
                    wsg_ref, wsu_ref, wsd_ref, y0_ref, fx_ref, gates_ref):
    mod = mod_ref[0]
    mix = _dot(at_ref[0], wo_ref[0:DA_WIDTH, :]) + _dot(gm_ref[0], wo_ref[DA_WIDTH:, :])
    x1 = x_ref[0] + mod[2:3] * mix
    fx = (_rms_rows(x1, nfg_ref[...]) * (1.0 + mod[4:5]) + mod[3:4]).astype(BF)
    fx_ref[0] = fx
    scores = _sigmoid(_dot(fx, wr_ref[...]))
    gates_ref[0] = _route(scores, rb_ref[...])
    sg = _dot(fx, wsg_ref[...])
    su = _dot(fx, wsu_ref[...])
    hs = (sg * _sigmoid(sg)) * su
    y0_ref[0] = x1 + mod[5:6] * _dot(hs.astype(BF), wsd_ref[...])


def _outproj_call(x, attn, gm, mod, w_out, nfg, wr, rb, wsg, wsu, wsd, tm):
    b, l, d = x.shape
    full = lambda shape: pl.BlockSpec(shape, lambda bi, i: (0,) * len(shape))
    tokd = pl.BlockSpec((1, tm, d), lambda bi, i: (bi, i, 0))
    tokh = pl.BlockSpec((1, tm, DA_WIDTH), lambda bi, i: (bi, i, 0))
    tokl = pl.BlockSpec((1, tm, LANES), lambda bi, i: (bi, i, 0))
    ds = wsg.shape[1]
    return pl.pallas_call(
        _outproj_kernel,
        out_shape=(jax.ShapeDtypeStruct((b, l, d), F32),
                   jax.ShapeDtypeStruct((b, l, d), BF),
                   jax.ShapeDtypeStruct((b, l, LANES), F32)),
        grid=(b, l // tm),
        in_specs=[tokd, tokh, tokh,
                  pl.BlockSpec((1, 6, d), lambda bi, i: (bi, 0, 0)),
                  full((d, d)), full((1, d)), full((d, LANES)), full((1, LANES)),
                  full((d, ds)), full((d, ds)), full((ds, d))],
        out_specs=(tokd, tokd, tokl),
        compiler_params=pltpu.CompilerParams(dimension_semantics=("arbitrary", "arbitrary"),
                                             vmem_limit_bytes=VMEM_LIMIT),
        name="outproj",
    )(x, attn, gm, mod, w_out, nfg, wr, rb, wsg, wsu, wsd)


def _moe_kernel(fx_ref, gates_ref, wg_ref, wu_ref, wd_ref, y0_ref, mod_ref, o_ref, acc_scr):
    e = pl.program_id(2)

    @pl.when(e == 0)
    def _zero():
        acc_scr[...] = jnp.zeros_like(acc_scr)

    x = fx_ref[0]
    a = _dot(x, wg_ref[0])
    bb = _dot(x, wu_ref[0])
    gates = gates_ref[0]
    lane = lax.broadcasted_iota(jnp.int32, gates.shape, 1)
    ge = jnp.sum(jnp.where(lane == e, gates, 0.0), axis=-1, keepdims=True)
    hm = (a * _sigmoid(a)) * bb * ge
    acc_scr[...] += _dot(hm.astype(BF), wd_ref[0])

    @pl.when(e == pl.num_programs(2) - 1)
    def _finish():
        o_ref[0] = y0_ref[0] + mod_ref[0][5:6] * acc_scr[...]


def _moe_call(fx, gates, wg, wu, wd, y0, mod, tm):
    b, l, d = fx.shape
    ne, _, de = wg.shape
    tokd = pl.BlockSpec((1, tm, d), lambda bi, i, e: (bi, i, 0))
    return pl.pallas_call(
        _moe_kernel,
        out_shape=jax.ShapeDtypeStruct((b, l, d), F32),
        grid=(b, l // tm, ne),
        in_specs=[tokd,
                  pl.BlockSpec((1, tm, LANES), lambda bi, i, e: (bi, i, 0)),
                  pl.BlockSpec((1, d, de), lambda bi, i, e: (e, 0, 0)),
                  pl.BlockSpec((1, d, de), lambda bi, i, e: (e, 0, 0)),
                  pl.BlockSpec((1, de, d), lambda bi, i, e: (e, 0, 0)),
                  tokd,
                  pl.BlockSpec((1, 6, d), lambda bi, i, e: (bi, 0, 0))],
        out_specs=tokd,
        scratch_shapes=[pltpu.VMEM((tm, d), F32)],
        compiler_params=pltpu.CompilerParams(
            dimension_semantics=("arbitrary", "arbitrary", "arbitrary"),
            vmem_limit_bytes=VMEM_LIMIT),
        name="moe",
    )(fx, gates, wg, wu, wd, y0, mod)


def _rope_tables(n_tokens):
    rows = n_tokens // GRID_W
    row = jnp.repeat(jnp.arange(rows, dtype=F32), GRID_W)
    col = jnp.tile(jnp.arange(GRID_W, dtype=F32), rows)
    half = DA_HEAD_DIM // 2
    inv_freq = ROPE_THETA ** (-jnp.arange(0, half, 2, dtype=F32) / half)
    ang = jnp.concatenate([row[:, None] * inv_freq, col[:, None] * inv_freq], axis=-1)
    cos, sin = jnp.cos(ang), jnp.sin(ang)
    cos64 = jnp.repeat(cos, 2, axis=-1)
    sin64 = jnp.stack([-sin, sin], axis=-1).reshape(n_tokens, DA_HEAD_DIM)
    return jnp.tile(cos64, (1, 2)), jnp.tile(sin64, (1, 2))


def kernel(x, c, ctx, c_ctx, w_ada, b_ada, norm_mix_g, w_in, q_norm_g, k_norm_g, da_lambda, subln_g, gm_ln_g, gm_ln_b, gm_ws, gm_bs, gm_out_g, w_out, norm_ffn_g, w_router, router_bias, we_gate, we_up, we_down, ws_gate, ws_up, ws_down):
    assert w_ada.shape[0] == 1, "single-layer kernel"
    b, l, d = x.shape
    lambda_init = 0.8 - 0.6 * math.exp(-0.3 * 0)
    lp = da_lambda[0].astype(F32)
    lam = (jnp.exp(jnp.sum(lp[0] * lp[1])) - jnp.exp(jnp.sum(lp[2] * lp[3])) + lambda_init).reshape(1)

    cond_rows = 16
    cond = jnp.zeros((cond_rows, d), F32).at[:b].set(c).at[b].set(c_ctx)
    ada = _ada_call(cond, w_ada[0], b_ada[0][None, :]).reshape(cond_rows, 6, d)
    mod = ada[:b]
    mod_ctx = ada[b:b + 1]

    cos, sin = _rope_tables(l)
    half = jnp.arange(LANES) // DA_HEAD_DIM
    gmat = (half[:, None] == half[None, :]).astype(BF)
    qg = jnp.tile(q_norm_g[0], 2)[None, :]
    kg = jnp.tile(k_norm_g[0], 2)[None, :]
    w_in_bf = w_in[0].astype(BF)
    bs_full = jnp.broadcast_to(gm_bs[0][:, :, None], (GM_HEADS, CHUNK, GM_HEAD_DIM))

    tm = min(512, l)
    q, k, v, gm = _inproj_call(
        x, mod, norm_mix_g, w_in_bf, qg, kg, cos, sin, gmat,
        gm_ln_g[0].reshape(1, GM_WIDTH), gm_ln_b[0].reshape(1, GM_WIDTH),
        gm_ws[0].astype(BF), bs_full, gm_out_g[0].reshape(1, GM_WIDTH), tm)
    kc, vc = _ctxproj_call(ctx, mod_ctx, norm_mix_g, w_in_bf[:, DA_WIDTH:3 * DA_WIDTH], kg, gmat)

    attn = _attn_call(lam, q, k, kc, v, vc, subln_g[0][:, None],
                      tq=min(256, l), ck=min(256, ctx.shape[1]), out_scale=1.0 - lambda_init)

    wr = jnp.zeros((d, LANES), BF).at[:, :N_EXPERTS].set(w_router[0].astype(BF))
    rb = jnp.zeros((1, LANES), F32).at[0, :N_EXPERTS].set(router_bias[0])
    y0, fx, gates = _outproj_call(
        x, attn, gm, mod, w_out[0].astype(BF), norm_ffn_g, wr, rb,
        ws_gate[0].astype(BF), ws_up[0].astype(BF), ws_down[0].astype(BF), tm)

    return _moe_call(fx, gates, we_gate[0].astype(BF), we_up[0].astype(BF), we_down[0].astype(BF),
                     y0, mod, min(1024, l))
```

```python
import functools
import math

import jax
import jax.numpy as jnp
from jax import lax
from jax.experimental import pallas as pl
from jax.experimental.pallas import tpu as pltpu

BF = jnp.bfloat16
F32 = jnp.float32

EPS = 1e-6
GRID_W = 64
DA_HEADS = 4
DA_HEAD_DIM = 64
DA_V_DIM = 128
DA_WIDTH = 512
GM_HEADS = 4
GM_HEAD_DIM = 128
GM_WIDTH = 512
CHUNK = 128
ROPE_THETA = 10000.0
N_EXPERTS = 32
TOP_K = 4
N_GROUPS = 4
TOPK_GROUPS = 2
GROUP_SIZE = N_EXPERTS // N_GROUPS
ROUTED_SCALE = 2.5
LANES = 128
VMEM_LIMIT = 56 * 1024 * 1024


def _sigmoid(x):
    return 1.0 / (1.0 + jnp.exp(-x))


def _dot(a, b):
    return jnp.dot(a, b, preferred_element_type=F32)


def _dot_nt(a, b):
    return lax.dot_general(a, b, (((1,), (1,)), ((), ())), preferred_element_type=F32)


def _rms_rows(x, g):
    ms = jnp.mean(x * x, axis=-1, keepdims=True)
    return x * lax.rsqrt(ms + EPS) * g


def _group_mean_sq(y, gmat):
    y2 = y * y
    hi = y2.astype(BF)
    lo = (y2 - hi.astype(F32)).astype(BF)
    return (_dot(hi, gmat) + _dot(lo, gmat)) * (1.0 / DA_HEAD_DIM)


def _swap_pairs(y):
    lane = lax.broadcasted_iota(jnp.int32, y.shape, 1)
    nxt = pltpu.roll(y, LANES - 1, 1)
    prv = pltpu.roll(y, 1, 1)
    return jnp.where((lane & 1) == 0, nxt, prv)


def _ada_kernel(cond_ref, w_ref, b_ref, o_ref):
    c = cond_ref[...]
    s = c * _sigmoid(c)
    o_ref[...] = _dot(s.astype(BF), w_ref[...].astype(BF)) + b_ref[...]


def _ada_call(cond, w_ada, b_ada):
    rows, d = cond.shape
    n = w_ada.shape[1]
    bn = 1024
    return pl.pallas_call(
        _ada_kernel,
        out_shape=jax.ShapeDtypeStruct((rows, n), F32),
        grid=(n // bn,),
        in_specs=[pl.BlockSpec((rows, d), lambda j: (0, 0)),
                  pl.BlockSpec((d, bn), lambda j: (0, j)),
                  pl.BlockSpec((1, bn), lambda j: (0, j))],
        out_specs=pl.BlockSpec((rows, bn), lambda j: (0, j)),
        compiler_params=pltpu.CompilerParams(dimension_semantics=("arbitrary",),
                                             vmem_limit_bytes=VMEM_LIMIT),
        name="ada",
    )(cond, w_ada, b_ada)


def _inproj_kernel(x_ref, mod_ref, ng_ref, w_ref, qg_ref, kg_ref, cos_ref, sin_ref, gmat_ref,
                   lng_ref, lnb_ref, ws_ref, bs_ref, og_ref,
                   q_ref, k_ref, v_ref, gm_ref):
    tm = x_ref.shape[1]
    x = x_ref[0]
    mod = mod_ref[0]
    h = _rms_rows(x, ng_ref[...]) * (1.0 + mod[1:2]) + mod[0:1]
    p = _dot(h.astype(BF), w_ref[...])

    gmat = gmat_ref[...]
    cos = cos_ref[...]
    sin = sin_ref[...]
    for j in range(DA_HEADS):
        sl = slice(j * LANES, (j + 1) * LANES)
        qj = p[:, sl]
        qn = qj * lax.rsqrt(_group_mean_sq(qj, gmat) + EPS) * qg_ref[...]
        qr = qn * cos + _swap_pairs(qn) * sin
        q_ref[0, :, sl] = (qr * (DA_HEAD_DIM ** -0.5)).astype(BF)
        kj = p[:, DA_WIDTH + j * LANES: DA_WIDTH + (j + 1) * LANES]
        kn = kj * lax.rsqrt(_group_mean_sq(kj, gmat) + EPS) * kg_ref[...]
        kr = kn * cos + _swap_pairs(kn) * sin
        k_ref[0, :, sl] = kr.astype(BF)
    v_ref[0] = p[:, 2 * DA_WIDTH:3 * DA_WIDTH].astype(BF)

    zp = p[:, 3 * DA_WIDTH:]
    z = 0.5 * zp * (1.0 + lax.erf(zp * math.sqrt(0.5)))
    for g in range(GM_HEADS):
        sl = slice(g * LANES, (g + 1) * LANES)
        u = z[:, sl]
        vg = z[:, GM_WIDTH + g * LANES: GM_WIDTH + (g + 1) * LANES]
        mu = jnp.mean(vg, axis=-1, keepdims=True)
        xc = vg - mu
        var = jnp.mean(xc * xc, axis=-1, keepdims=True)
        vn = (xc * lax.rsqrt(var + EPS) * lng_ref[:, sl] + lnb_ref[:, sl]).astype(BF)
        for cidx in range(tm // CHUNK):
            rows = slice(cidx * CHUNK, (cidx + 1) * CHUNK)
            mixed = _dot(ws_ref[g], vn[rows]) + bs_ref[g]
            y = u[rows] * mixed
            gm_ref[0, rows, sl] = _rms_rows(y, og_ref[:, sl]).astype(BF)


def _inproj_call(x, mod, ng, w_in, qg, kg, cos, sin, gmat, lng, lnb, ws, bs, og, tm):
    b, l, d = x.shape
    nw = w_in.shape[1]
    full = lambda shape: pl.BlockSpec(shape, lambda bi, i: (0,) * len(shape))
    tok = pl.BlockSpec((1, tm, DA_WIDTH), lambda bi, i: (bi, i, 0))
    out = jax.ShapeDtypeStruct((b, l, DA_WIDTH), BF)
    return pl.pallas_call(
        _inproj_kernel,
        out_shape=(out, out, out, out),
        grid=(b, l // tm),
        in_specs=[pl.BlockSpec((1, tm, d), lambda bi, i: (bi, i, 0)),
                  pl.BlockSpec((1, 6, d), lambda bi, i: (bi, 0, 0)),
                  full((1, d)), full((d, nw)), full((1, LANES)), full((1, LANES)),
                  pl.BlockSpec((tm, LANES), lambda bi, i: (i, 0)),
                  pl.BlockSpec((tm, LANES), lambda bi, i: (i, 0)),
                  full((LANES, LANES)), full((1, GM_WIDTH)), full((1, GM_WIDTH)),
                  full((GM_HEADS, CHUNK, CHUNK)), full((GM_HEADS, CHUNK, GM_HEAD_DIM)),
                  full((1, GM_WIDTH))],
        out_specs=(tok, tok, tok, tok),
        compiler_params=pltpu.CompilerParams(dimension_semantics=("arbitrary", "arbitrary"),
                                             vmem_limit_bytes=VMEM_LIMIT),
        name="inproj",
    )(x, mod, ng, w_in, qg, kg, cos, sin, gmat, lng, lnb, ws, bs, og)


def _ctxproj_kernel(x_ref, mod_ref, ng_ref, w_ref, kg_ref, gmat_ref, k_ref, v_ref):
    x = x_ref[0]
    mod = mod_ref[0]
    h = _rms_rows(x, ng_ref[...]) * (1.0 + mod[1:2]) + mod[0:1]
    p = _dot(h.astype(BF), w_ref[...])
    gmat = gmat_ref[...]
    for j in range(DA_HEADS):
        sl = slice(j * LANES, (j + 1) * LANES)
        kj = p[:, sl]
        kn = kj * lax.rsqrt(_group_mean_sq(kj, gmat) + EPS) * kg_ref[...]
        k_ref[0, :, sl] = kn.astype(BF)
    v_ref[0] = p[:, DA_WIDTH:].astype(BF)


def _ctxproj_call(ctx, mod_ctx, ng, w_kv, kg, gmat):
    b, lc, d = ctx.shape
    full = lambda shape: pl.BlockSpec(shape, lambda bi: (0,) * len(shape))
    tok = pl.BlockSpec((1, lc, DA_WIDTH), lambda bi: (bi, 0, 0))
    out = jax.ShapeDtypeStruct((b, lc, DA_WIDTH), BF)
    return pl.pallas_call(
        _ctxproj_kernel,
        out_shape=(out, out),
        grid=(b,),
        in_specs=[pl.BlockSpec((1, lc, d), lambda bi: (bi, 0, 0)),
                  full((1, 6, d)), full((1, d)), full((d, 2 * DA_WIDTH)),
                  full((1, LANES)), full((LANES, LANES))],
        out_specs=(tok, tok),
        compiler_params=pltpu.CompilerParams(dimension_semantics=("arbitrary",),
                                             vmem_limit_bytes=VMEM_LIMIT),
        name="ctxproj",
    )(ctx, mod_ctx, ng, w_kv, kg, gmat)


def _attn_kernel(lam_ref, q_ref, kx_ref, kc_ref, vx_ref, vc_ref, sg_ref, o_ref,
                 k_scr, vt_scr, s_scr, m_scr, *, ck, out_scale):
    tq = q_ref.shape[1]
    lc = kc_ref.shape[1]
    lx = kx_ref.shape[1]
    nck = (lc + lx) // ck

    step = pl.program_id(2)
    last = pl.num_programs(2) - 1

    def stacked_queries():
        q = q_ref[0]
        lane = lax.broadcasted_iota(jnp.int32, q.shape, 1)
        zero = jnp.zeros_like(q)
        return jnp.concatenate([jnp.where(lane < DA_HEAD_DIM, q, zero),
                                jnp.where(lane >= DA_HEAD_DIM, q, zero)], axis=0)

    def score_chunk(qq, cidx, mrun):
        rows = slice(cidx * ck, (cidx + 1) * ck)
        st = _dot_nt(k_scr[rows, :], qq)
        s_scr[rows, :] = st
        return jnp.maximum(mrun, jnp.max(st.reshape(ck // 8, 8, 2 * tq), axis=0))

    def weight_chunk(m, cidx, acc, lrun):
        rows = slice(cidx * ck, (cidx + 1) * ck)
        pt = jnp.exp(s_scr[rows, :] - m)
        acc = acc + _dot(vt_scr[cidx], pt.astype(BF))
        return acc, lrun + jnp.sum(pt.reshape(ck // 8, 8, 2 * tq), axis=0)

    def finish(acc, lrun):
        r = 1.0 / jnp.sum(lrun, axis=0, keepdims=True)
        ot = acc[:, :tq] * r[:, :tq] - lam_ref[0] * (acc[:, tq:] * r[:, tq:])
        ms = jnp.mean(ot * ot, axis=0, keepdims=True)
        on = ot * lax.rsqrt(ms + EPS) * sg_ref[...] * out_scale
        o_ref[0] = on.T.astype(BF)

    mrun0 = jnp.full((8, 2 * tq), -jnp.inf, F32)
    lrun0 = jnp.zeros((8, 2 * tq), F32)
    acc0 = jnp.zeros((DA_V_DIM, 2 * tq), F32)

    @pl.when(step == 0)
    def _first():
        k_scr[0:lc, :] = kc_ref[0]
        k_scr[lc:lc + lx, :] = kx_ref[0]
        for cidx in range(nck):
            lo = cidx * ck
            if lo < lc:
                blk = vc_ref[0, lo:lo + ck, :]
            else:
                blk = vx_ref[0, lo - lc:lo - lc + ck, :]
            vt_scr[cidx] = blk.astype(F32).T.astype(BF)
        qq = stacked_queries()
        mrun = mrun0
        for cidx in range(nck):
            mrun = score_chunk(qq, cidx, mrun)
        m_scr[...] = mrun

    @pl.when((step > 0) & (step < last))
    def _steady():
        qq = stacked_queries()
        m = jnp.max(m_scr[...], axis=0, keepdims=True)
        mrun, lrun, acc = mrun0, lrun0, acc0
        for cidx in range(nck):
            acc, lrun = weight_chunk(m, cidx, acc, lrun)
            mrun = score_chunk(qq, cidx, mrun)
        finish(acc, lrun)
        m_scr[...] = mrun

    @pl.when(step == last)
    def _last():
        m = jnp.max(m_scr[...], axis=0, keepdims=True)
        lrun, acc = lrun0, acc0
        for cidx in range(nck):
            acc, lrun = weight_chunk(m, cidx, acc, lrun)
        finish(acc, lrun)


def _attn_call(lam, q, kx, kc, vx, vc, subln_col, tq, ck, out_scale):
    b, l, _ = q.shape
    lc = kc.shape[1]
    lk = lc + l
    assert lk % ck == 0 and lc % ck == 0 and l % tq == 0
    nq = l // tq
    kern = functools.partial(_attn_kernel, ck=ck, out_scale=out_scale)
    return pl.pallas_call(
        kern,
        out_shape=jax.ShapeDtypeStruct((b, l, DA_WIDTH), BF),
        grid=(b, DA_HEADS, nq + 1),
        in_specs=[pl.BlockSpec(memory_space=pltpu.SMEM),
                  pl.BlockSpec((1, tq, LANES), lambda bi, h, i: (bi, jnp.minimum(i, nq - 1), h)),
                  pl.BlockSpec((1, l, LANES), lambda bi, h, i: (bi, 0, h)),
                  pl.BlockSpec((1, lc, LANES), lambda bi, h, i: (bi, 0, h)),
                  pl.BlockSpec((1, l, LANES), lambda bi, h, i: (bi, 0, h)),
                  pl.BlockSpec((1, lc, LANES), lambda bi, h, i: (bi, 0, h)),
                  pl.BlockSpec((DA_V_DIM, 1), lambda bi, h, i: (0, 0))],
        out_specs=pl.BlockSpec((1, tq, LANES), lambda bi, h, i: (bi, jnp.maximum(i - 1, 0), h)),
        scratch_shapes=[pltpu.VMEM((lk, LANES), BF),
                        pltpu.VMEM((lk // ck, DA_V_DIM, ck), BF),
                        pltpu.VMEM((lk, 2 * tq), F32),
                        pltpu.VMEM((8, 2 * tq), F32)],
        compiler_params=pltpu.CompilerParams(
            dimension_semantics=("arbitrary", "arbitrary", "arbitrary"),
            vmem_limit_bytes=VMEM_LIMIT),
        name="attn",
    )(lam, q, kx, kc, vx, vc, subln_col)


def _route(scores, rbias):
    shape = scores.shape
    lane = lax.broadcasted_iota(jnp.int32, shape, 1)
    lanef = lane.astype(F32)
    neg = jnp.full(shape, -jnp.inf, F32)
    biased = jnp.where(lane < N_EXPERTS, scores + rbias, neg)

    gscore = []
    for g in range(N_GROUPS):
        ing = (lane >= g * GROUP_SIZE) & (lane < (g + 1) * GROUP_SIZE)
        vals = jnp.where(ing, biased, neg)
        m1 = jnp.max(vals, axis=-1, keepdims=True)
        i1 = jnp.min(jnp.where(vals == m1, lanef, 1e9), axis=-1, keepdims=True)
        m2 = jnp.max(jnp.where(lanef == i1, neg, vals), axis=-1, keepdims=True)
        gscore.append(m1 + m2)

    emask = jnp.zeros(shape, F32)
    for g in range(N_GROUPS):
        beaten = jnp.zeros_like(gscore[g])
        for g2 in range(N_GROUPS):
            if g2 == g:
                continue
            beat = (gscore[g2] >= gscore[g]) if g2 < g else (gscore[g2] > gscore[g])
            beaten = beaten + jnp.where(beat, 1.0, 0.0)
        chosen = jnp.where(beaten < TOPK_GROUPS, 1.0, 0.0)
        ing = (lane >= g * GROUP_SIZE) & (lane < (g + 1) * GROUP_SIZE)
        emask = jnp.where(ing, chosen, emask)

    cur = jnp.where(emask > 0.5, biased, neg)
    sel = jnp.zeros(shape, F32)
    for _ in range(TOP_K):
        mx = jnp.max(cur, axis=-1, keepdims=True)
        idx = jnp.min(jnp.where(cur == mx, lanef, 1e9), axis=-1, keepdims=True)
        hit = lanef == idx
        sel = jnp.where(hit, 1.0, sel)
        cur = jnp.where(hit, neg, cur)
    w = sel * scores
    return w / jnp.sum(w, axis=-1, keepdims=True) * ROUTED_SCALE


def _outproj_kernel(x_ref, at_ref, gm_ref, mod_ref, wo_ref, nfg_ref, wr_ref, rb_ref,
                    wsg_ref, wsu_ref, wsd_ref, y0_ref, fx_ref, gates_ref):
    mod = mod_ref[0]
    mix = _dot(at_ref[0], wo_ref[0:DA_WIDTH, :]) + _dot(gm_ref[0], wo_ref[DA_WIDTH:, :])
    x1 = x_ref[0] + mod[2:3] * mix
    fx = (_rms_rows(x1, nfg_ref[...]) * (1.0 + mod[4:5]) + mod[3:4]).astype(BF)
    fx_ref[0] = fx
    scores = _sigmoid(_dot(fx, wr_ref[...]))
    gates_ref[0] = _route(scores, rb_ref[...])
    sg = _dot(fx, wsg_ref[...])
    su = _dot(fx, wsu_ref[...])
    hs = (sg * _sigmoid(sg)) * su
    y0_ref[0] = x1 + mod[5:6] * _dot(hs.astype(BF), wsd_ref[...])


def _outproj_call(x, attn, gm, mod, w_out, nfg, wr, rb, wsg, wsu, wsd, tm):
    b, l, d = x.shape
    full = lambda shape: pl.BlockSpec(shape, lambda bi, i: (0,) * len(shape))
    tokd = pl.BlockSpec((1, tm, d), lambda bi, i: (bi, i, 0))
    tokh = pl.BlockSpec((1, tm, DA_WIDTH), lambda bi, i: (bi, i, 0))
    tokl = pl.BlockSpec((1, tm, LANES), lambda bi, i: (bi, i, 0))
    ds = wsg.shape[1]
    return pl.pallas_call(
        _outproj_kernel,
        out_shape=(jax.ShapeDtypeStruct((b, l, d), F32),
                   jax.ShapeDtypeStruct((b, l, d), BF),
                   jax.ShapeDtypeStruct((b, l, LANES), F32)),
        grid=(b, l // tm),
        in_specs=[tokd, tokh, tokh,
                  pl.BlockSpec((1, 6, d), lambda bi, i: (bi, 0, 0)),
                  full((d, d)), full((1, d)), full((d, LANES)), full((1, LANES)),
                  full((d, ds)), full((d, ds)), full((ds, d))],
        out_specs=(tokd, tokd, tokl),
        compiler_params=pltpu.CompilerParams(dimension_semantics=("arbitrary", "arbitrary"),
                                             vmem_limit_bytes=VMEM_LIMIT),
        name="outproj",
    )(x, attn, gm, mod, w_out, nfg, wr, rb, wsg, wsu, wsd)


def _moe_kernel(fx_ref, gates_ref, wg_ref, wu_ref, wd_ref, y0_ref, mod_ref, o_ref, acc_scr):
    e = pl.program_id(2)

    @pl.when(e == 0)
    def _zero():
        acc_scr[...] = jnp.zeros_like(acc_scr)

    x = fx_ref[0]
    a = _dot(x, wg_ref[0])
    bb = _dot(x, wu_ref[0])
    gates = gates_ref[0]
    lane = lax.broadcasted_iota(jnp.int32, gates.shape, 1)
    ge = jnp.sum(jnp.where(lane == e, gates, 0.0), axis=-1, keepdims=True)
    hm = (a * _sigmoid(a)) * bb * ge
    acc_scr[...] += _dot(hm.astype(BF), wd_ref[0])

    @pl.when(e == pl.num_programs(2) - 1)
    def _finish():
        o_ref[0] = y0_ref[0] + mod_ref[0][5:6] * acc_scr[...]


def _moe_call(fx, gates, wg, wu, wd, y0, mod, tm):
    b, l, d = fx.shape
    ne, _, de = wg.shape
    tokd = pl.BlockSpec((1, tm, d), lambda bi, i, e: (bi, i, 0))
    return pl.pallas_call(
        _moe_kernel,
        out_shape=jax.ShapeDtypeStruct((b, l, d), F32),
        grid=(b, l // tm, ne),
        in_specs=[tokd,
                  pl.BlockSpec((1, tm, LANES), lambda bi, i, e: (bi, i, 0)),
                  pl.BlockSpec((1, d, de), lambda bi, i, e: (e, 0, 0)),
                  pl.BlockSpec((1, d, de), lambda bi, i, e: (e, 0, 0)),
                  pl.BlockSpec((1, de, d), lambda bi, i, e: (e, 0, 0)),
                  tokd,
                  pl.BlockSpec((1, 6, d), lambda bi, i, e: (bi, 0, 0))],
        out_specs=tokd,
        scratch_shapes=[pltpu.VMEM((tm, d), F32)],
        compiler_params=pltpu.CompilerParams(
            dimension_semantics=("arbitrary", "arbitrary", "arbitrary"),
            vmem_limit_bytes=VMEM_LIMIT),
        name="moe",
    )(fx, gates, wg, wu, wd, y0, mod)


def _rope_tables(n_tokens):
    rows = n_tokens // GRID_W
    row = jnp.repeat(jnp.arange(rows, dtype=F32), GRID_W)
    col = jnp.tile(jnp.arange(GRID_W, dtype=F32), rows)
    half = DA_HEAD_DIM // 2
    inv_freq = ROPE_THETA ** (-jnp.arange(0, half, 2, dtype=F32) / half)
    ang = jnp.concatenate([row[:, None] * inv_freq, col[:, None] * inv_freq], axis=-1)
    cos, sin = jnp.cos(ang), jnp.sin(ang)
    cos64 = jnp.repeat(cos, 2, axis=-1)
    sin64 = jnp.stack([-sin, sin], axis=-1).reshape(n_tokens, DA_HEAD_DIM)
    return jnp.tile(cos64, (1, 2)), jnp.tile(sin64, (1, 2))


def kernel(x, c, ctx, c_ctx, w_ada, b_ada, norm_mix_g, w_in, q_norm_g, k_norm_g, da_lambda, subln_g, gm_ln_g, gm_ln_b, gm_ws, gm_bs, gm_out_g, w_out, norm_ffn_g, w_router, router_bias, we_gate, we_up, we_down, ws_gate, ws_up, ws_down):
    assert w_ada.shape[0] == 1, "single-layer kernel"
    b, l, d = x.shape
    lambda_init = 0.8 - 0.6 * math.exp(-0.3 * 0)
    lp = da_lambda[0].astype(F32)
    lam = (jnp.exp(jnp.sum(lp[0] * lp[1])) - jnp.exp(jnp.sum(lp[2] * lp[3])) + lambda_init).reshape(1)

    cond_rows = 16
    cond = jnp.zeros((cond_rows, d), F32).at[:b].set(c).at[b].set(c_ctx)
    ada = _ada_call(cond, w_ada[0], b_ada[0][None, :]).reshape(cond_rows, 6, d)
    mod = ada[:b]
    mod_ctx = ada[b:b + 1]

    cos, sin = _rope_tables(l)
    half = jnp.arange(LANES) // DA_HEAD_DIM
    gmat = (half[:, None] == half[None, :]).astype(BF)
    qg = jnp.tile(q_norm_g[0], 2)[None, :]
    kg = jnp.tile(k_norm_g[0], 2)[None, :]
    w_in_bf = w_in[0].astype(BF)
    bs_full = jnp.broadcast_to(gm_bs[0][:, :, None], (GM_HEADS, CHUNK, GM_HEAD_DIM))

    tm = min(512, l)
    q, k, v, gm = _inproj_call(
        x, mod, norm_mix_g, w_in_bf, qg, kg, cos, sin, gmat,
        gm_ln_g[0].reshape(1, GM_WIDTH), gm_ln_b[0].reshape(1, GM_WIDTH),
        gm_ws[0].astype(BF), bs_full, gm_out_g[0].reshape(1, GM_WIDTH), tm)
    kc, vc = _ctxproj_call(ctx, mod_ctx, norm_mix_g, w_in_bf[:, DA_WIDTH:3 * DA_WIDTH], kg, gmat)

    attn = _attn_call(lam, q, k, kc, v, vc, subln_g[0][:, None],
                      tq=min(256, l), ck=min(256, ctx.shape[1]), out_scale=1.0 - lambda_init)

    wr = jnp.zeros((d, LANES), BF).at[:, :N_EXPERTS].set(w_router[0].astype(BF))
    rb = jnp.zeros((1, LANES), F32).at[0, :N_EXPERTS].set(router_bias[0])
    y0, fx, gates = _outproj_call(
        x, attn, gm, mod, w_out[0].astype(BF), norm_ffn_g, wr, rb,
        ws_gate[0].astype(BF), ws_up[0].astype(BF), ws_down[0].astype(BF), tm)

    return _moe_call(fx, gates, we_gate[0].astype(BF), we_up[0].astype(BF), we_down[0].astype(BF),
                     y0, mod, min(1024, l))
```

```python
import functools
import math

import jax
import jax.numpy as jnp
from jax import lax
from jax.experimental import pallas as pl
from jax.experimental.pallas import tpu as pltpu

BF = jnp.bfloat16
F32 = jnp.float32

EPS = 1e-6
GRID_W = 64
DA_HEADS = 4
DA_HEAD_DIM = 64
DA_V_DIM = 128
DA_WIDTH = 512
GM_HEADS = 4
GM_HEAD_DIM = 128
GM_WIDTH = 512
CHUNK = 128
ROPE_THETA = 10000.0
N_EXPERTS = 32
TOP_K = 4
N_GROUPS = 4
TOPK_GROUPS = 2
GROUP_SIZE = N_EXPERTS // N_GROUPS
ROUTED_SCALE = 2.5
MOE_TILE = 1024
MOE_CAP = 80
MOE_EXPERTS_PER_STEP = 2
LANES = 128
VMEM_LIMIT = 56 * 1024 * 1024


def _sigmoid(x):
    return 1.0 / (1.0 + jnp.exp(-x))


def _dot(a, b):
    return jnp.dot(a, b, preferred_element_type=F32)


def _dot_nt(a, b):
    return lax.dot_general(a, b, (((1,), (1,)), ((), ())), preferred_element_type=F32)


def _rms_rows(x, g):
    ms = jnp.mean(x * x, axis=-1, keepdims=True)
    return x * lax.rsqrt(ms + EPS) * g


def _group_mean_sq(y, gmat):
    y2 = y * y
    hi = y2.astype(BF)
    lo = (y2 - hi.astype(F32)).astype(BF)
    return (_dot(hi, gmat) + _dot(lo, gmat)) * (1.0 / DA_HEAD_DIM)


def _swap_pairs(y):
    lane = lax.broadcasted_iota(jnp.int32, y.shape, 1)
    nxt = pltpu.roll(y, LANES - 1, 1)
    prv = pltpu.roll(y, 1, 1)
    return jnp.where((lane & 1) == 0, nxt, prv)


def _ada_kernel(cond_ref, w_ref, b_ref, o_ref):
    c = cond_ref[...]
    s = c * _sigmoid(c)
    o_ref[...] = _dot(s.astype(BF), w_ref[...].astype(BF)) + b_ref[...]


def _ada_call(cond, w_ada, b_ada):
    rows, d = cond.shape
    n = w_ada.shape[1]
    bn = 1024
    return pl.pallas_call(
        _ada_kernel,
        out_shape=jax.ShapeDtypeStruct((rows, n), F32),
        grid=(n // bn,),
        in_specs=[pl.BlockSpec((rows, d), lambda j: (0, 0)),
                  pl.BlockSpec((d, bn), lambda j: (0, j)),
                  pl.BlockSpec((1, bn), lambda j: (0, j))],
        out_specs=pl.BlockSpec((rows, bn), lambda j: (0, j)),
        compiler_params=pltpu.CompilerParams(dimension_semantics=("arbitrary",),
                                             vmem_limit_bytes=VMEM_LIMIT),
        name="ada",
    )(cond, w_ada, b_ada)


def _inproj_kernel(x_ref, mod_ref, ng_ref, w_ref, qg_ref, kg_ref, cos_ref, sin_ref, gmat_ref,
                   lng_ref, lnb_ref, ws_ref, bs_ref, og_ref,
                   q_ref, k_ref, v_ref, gm_ref):
    tm = x_ref.shape[1]
    x = x_ref[0]
    mod = mod_ref[0]
    h = _rms_rows(x, ng_ref[...]) * (1.0 + mod[1:2]) + mod[0:1]
    p = _dot(h.astype(BF), w_ref[...])

    gmat = gmat_ref[...]
    cos = cos_ref[...]
    sin = sin_ref[...]
    for j in range(DA_HEADS):
        sl = slice(j * LANES, (j + 1) * LANES)
        qj = p[:, sl]
        qn = qj * lax.rsqrt(_group_mean_sq(qj, gmat) + EPS) * qg_ref[...]
        qr = qn * cos + _swap_pairs(qn) * sin
        q_ref[0, :, sl] = (qr * (DA_HEAD_DIM ** -0.5)).astype(BF)
        kj = p[:, DA_WIDTH + j * LANES: DA_WIDTH + (j + 1) * LANES]
        kn = kj * lax.rsqrt(_group_mean_sq(kj, gmat) + EPS) * kg_ref[...]
        kr = kn * cos + _swap_pairs(kn) * sin
        k_ref[0, :, sl] = kr.astype(BF)
    v_ref[0] = p[:, 2 * DA_WIDTH:3 * DA_WIDTH].astype(BF)

    zp = p[:, 3 * DA_WIDTH:]
    z = 0.5 * zp * (1.0 + lax.erf(zp * math.sqrt(0.5)))
    for g in range(GM_HEADS):
        sl = slice(g * LANES, (g + 1) * LANES)
        u = z[:, sl]
        vg = z[:, GM_WIDTH + g * LANES: GM_WIDTH + (g + 1) * LANES]
        mu = jnp.mean(vg, axis=-1, keepdims=True)
        xc = vg - mu
        var = jnp.mean(xc * xc, axis=-1, keepdims=True)
        vn = (xc * lax.rsqrt(var + EPS) * lng_ref[:, sl] + lnb_ref[:, sl]).astype(BF)
        for cidx in range(tm // CHUNK):
            rows = slice(cidx * CHUNK, (cidx + 1) * CHUNK)
            mixed = _dot(ws_ref[g], vn[rows]) + bs_ref[g]
            y = u[rows] * mixed
            gm_ref[0, rows, sl] = _rms_rows(y, og_ref[:, sl]).astype(BF)


def _inproj_call(x, mod, ng, w_in, qg, kg, cos, sin, gmat, lng, lnb, ws, bs, og, tm):
    b, l, d = x.shape
    nw = w_in.shape[1]
    full = lambda shape: pl.BlockSpec(shape, lambda bi, i: (0,) * len(shape))
    tok = pl.BlockSpec((1, tm, DA_WIDTH), lambda bi, i: (bi, i, 0))
    out = jax.ShapeDtypeStruct((b, l, DA_WIDTH), BF)
    return pl.pallas_call(
        _inproj_kernel,
        out_shape=(out, out, out, out),
        grid=(b, l // tm),
        in_specs=[pl.BlockSpec((1, tm, d), lambda bi, i: (bi, i, 0)),
                  pl.BlockSpec((1, 6, d), lambda bi, i: (bi, 0, 0)),
                  full((1, d)), full((d, nw)), full((1, LANES)), full((1, LANES)),
                  pl.BlockSpec((tm, LANES), lambda bi, i: (i, 0)),
                  pl.BlockSpec((tm, LANES), lambda bi, i: (i, 0)),
                  full((LANES, LANES)), full((1, GM_WIDTH)), full((1, GM_WIDTH)),
                  full((GM_HEADS, CHUNK, CHUNK)), full((GM_HEADS, CHUNK, GM_HEAD_DIM)),
                  full((1, GM_WIDTH))],
        out_specs=(tok, tok, tok, tok),
        compiler_params=pltpu.CompilerParams(dimension_semantics=("arbitrary", "arbitrary"),
                                             vmem_limit_bytes=VMEM_LIMIT),
        name="inproj",
    )(x, mod, ng, w_in, qg, kg, cos, sin, gmat, lng, lnb, ws, bs, og)


def _ctxproj_kernel(x_ref, mod_ref, ng_ref, w_ref, kg_ref, gmat_ref, k_ref, v_ref):
    x = x_ref[0]
    mod = mod_ref[0]
    h = _rms_rows(x, ng_ref[...]) * (1.0 + mod[1:2]) + mod[0:1]
    p = _dot(h.astype(BF), w_ref[...])
    gmat = gmat_ref[...]
    for j in range(DA_HEADS):
        sl = slice(j * LANES, (j + 1) * LANES)
        kj = p[:, sl]
        kn = kj * lax.rsqrt(_group_mean_sq(kj, gmat) + EPS) * kg_ref[...]
        k_ref[0, :, sl] = kn.astype(BF)
    v_ref[0] = p[:, DA_WIDTH:].astype(BF)


def _ctxproj_call(ctx, mod_ctx, ng, w_kv, kg, gmat):
    b, lc, d = ctx.shape
    full = lambda shape: pl.BlockSpec(shape, lambda bi: (0,) * len(shape))
    tok = pl.BlockSpec((1, lc, DA_WIDTH), lambda bi: (bi, 0, 0))
    out = jax.ShapeDtypeStruct((b, lc, DA_WIDTH), BF)
    return pl.pallas_call(
        _ctxproj_kernel,
        out_shape=(out, out),
        grid=(b,),
        in_specs=[pl.BlockSpec((1, lc, d), lambda bi: (bi, 0, 0)),
                  full((1, 6, d)), full((1, d)), full((d, 2 * DA_WIDTH)),
                  full((1, LANES)), full((LANES, LANES))],
        out_specs=(tok, tok),
        compiler_params=pltpu.CompilerParams(dimension_semantics=("arbitrary",),
                                             vmem_limit_bytes=VMEM_LIMIT),
        name="ctxproj",
    )(ctx, mod_ctx, ng, w_kv, kg, gmat)


def _attn_kernel(lam_ref, q_ref, kx_ref, kc_ref, vx_ref, vc_ref, sg_ref, o_ref,
                 k_scr, vt_scr, s_scr, m_scr, *, ck, out_scale):
    tq = q_ref.shape[1]
    lc = kc_ref.shape[1]
    lx = kx_ref.shape[1]
    nck = (lc + lx) // ck

    step = pl.program_id(2)
    last = pl.num_programs(2) - 1

    def stacked_queries():
        q = q_ref[0]
        lane = lax.broadcasted_iota(jnp.int32, q.shape, 1)
        zero = jnp.zeros_like(q)
        return jnp.concatenate([jnp.where(lane < DA_HEAD_DIM, q, zero),
                                jnp.where(lane >= DA_HEAD_DIM, q, zero)], axis=0)

    def score_chunk(qq, cidx, mrun):
        rows = slice(cidx * ck, (cidx + 1) * ck)
        st = _dot_nt(k_scr[rows, :], qq)
        s_scr[rows, :] = st
        return jnp.maximum(mrun, jnp.max(st.reshape(ck // 8, 8, 2 * tq), axis=0))

    def weight_chunk(m, cidx, acc, lrun):
        rows = slice(cidx * ck, (cidx + 1) * ck)
        pt = jnp.exp(s_scr[rows, :] - m)
        acc = acc + _dot(vt_scr[cidx], pt.astype(BF))
        return acc, lrun + jnp.sum(pt.reshape(ck // 8, 8, 2 * tq), axis=0)

    def finish(acc, lrun):
        r = 1.0 / jnp.sum(lrun, axis=0, keepdims=True)
        ot = acc[:, :tq] * r[:, :tq] - lam_ref[0] * (acc[:, tq:] * r[:, tq:])
        ms = jnp.mean(ot * ot, axis=0, keepdims=True)
        on = ot * lax.rsqrt(ms + EPS) * sg_ref[...] * out_scale
        o_ref[0] = on.T.astype(BF)

    mrun0 = jnp.full((8, 2 * tq), -jnp.inf, F32)
    lrun0 = jnp.zeros((8, 2 * tq), F32)
    acc0 = jnp.zeros((DA_V_DIM, 2 * tq), F32)

    @pl.when(step == 0)
    def _first():
        k_scr[0:lc, :] = kc_ref[0]
        k_scr[lc:lc + lx, :] = kx_ref[0]
        for cidx in range(nck):
            lo = cidx * ck
            if lo < lc:
                blk = vc_ref[0, lo:lo + ck, :]
            else:
                blk = vx_ref[0, lo - lc:lo - lc + ck, :]
            vt_scr[cidx] = blk.astype(F32).T.astype(BF)
        qq = stacked_queries()
        mrun = mrun0
        for cidx in range(nck):
            mrun = score_chunk(qq, cidx, mrun)
        m_scr[...] = mrun

    @pl.when((step > 0) & (step < last))
    def _steady():
        qq = stacked_queries()
        m = jnp.max(m_scr[...], axis=0, keepdims=True)
        mrun, lrun, acc = mrun0, lrun0, acc0
        for cidx in range(nck):
            acc, lrun = weight_chunk(m, cidx, acc, lrun)
            mrun = score_chunk(qq, cidx, mrun)
        finish(acc, lrun)
        m_scr[...] = mrun

    @pl.when(step == last)
    def _last():
        m = jnp.max(m_scr[...], axis=0, keepdims=True)
        lrun, acc = lrun0, acc0
        for cidx in range(nck):
            acc, lrun = weight_chunk(m, cidx, acc, lrun)
        finish(acc, lrun)


def _attn_call(lam, q, kx, kc, vx, vc, subln_col, tq, ck, out_scale):
    b, l, _ = q.shape
    lc = kc.shape[1]
    lk = lc + l
    assert lk % ck == 0 and lc % ck == 0 and l % tq == 0
    nq = l // tq
    kern = functools.partial(_attn_kernel, ck=ck, out_scale=out_scale)
    return pl.pallas_call(
        kern,
        out_shape=jax.ShapeDtypeStruct((b, l, DA_WIDTH), BF),
        grid=(b, DA_HEADS, nq + 1),
        in_specs=[pl.BlockSpec(memory_space=pltpu.SMEM),
                  pl.BlockSpec((1, tq, LANES), lambda bi, h, i: (bi, jnp.minimum(i, nq - 1), h)),
                  pl.BlockSpec((1, l, LANES), lambda bi, h, i: (bi, 0, h)),
                  pl.BlockSpec((1, lc, LANES), lambda bi, h, i: (bi, 0, h)),
                  pl.BlockSpec((1, l, LANES), lambda bi, h, i: (bi, 0, h)),
                  pl.BlockSpec((1, lc, LANES), lambda bi, h, i: (bi, 0, h)),
                  pl.BlockSpec((DA_V_DIM, 1), lambda bi, h, i: (0, 0))],
        out_specs=pl.BlockSpec((1, tq, LANES), lambda bi, h, i: (bi, jnp.maximum(i - 1, 0), h)),
        scratch_shapes=[pltpu.VMEM((lk, LANES), BF),
                        pltpu.VMEM((lk // ck, DA_V_DIM, ck), BF),
                        pltpu.VMEM((lk, 2 * tq), F32),
                        pltpu.VMEM((8, 2 * tq), F32)],
        compiler_params=pltpu.CompilerParams(
            dimension_semantics=("arbitrary", "arbitrary", "arbitrary"),
            vmem_limit_bytes=VMEM_LIMIT),
        name="attn",
    )(lam, q, kx, kc, vx, vc, subln_col)


def _route(scores, rbias):
    shape = scores.shape
    lane = lax.broadcasted_iota(jnp.int32, shape, 1)
    lanef = lane.astype(F32)
    neg = jnp.full(shape, -jnp.inf, F32)
    biased = jnp.where(lane < N_EXPERTS, scores + rbias, neg)

    gscore = []
    for g in range(N_GROUPS):
        ing = (lane >= g * GROUP_SIZE) & (lane < (g + 1) * GROUP_SIZE)
        vals = jnp.where(ing, biased, neg)
        m1 = jnp.max(vals, axis=-1, keepdims=True)
        i1 = jnp.min(jnp.where(vals == m1, lanef, 1e9), axis=-1, keepdims=True)
        m2 = jnp.max(jnp.where(lanef == i1, neg, vals), axis=-1, keepdims=True)
        gscore.append(m1 + m2)

    emask = jnp.zeros(shape, F32)
    for g in range(N_GROUPS):
        beaten = jnp.zeros_like(gscore[g])
        for g2 in range(N_GROUPS):
            if g2 == g:
                continue
            beat = (gscore[g2] >= gscore[g]) if g2 < g else (gscore[g2] > gscore[g])
            beaten = beaten + jnp.where(beat, 1.0, 0.0)
        chosen = jnp.where(beaten < TOPK_GROUPS, 1.0, 0.0)
        ing = (lane >= g * GROUP_SIZE) & (lane < (g + 1) * GROUP_SIZE)
        emask = jnp.where(ing, chosen, emask)

    cur = jnp.where(emask > 0.5, biased, neg)
    sel = jnp.zeros(shape, F32)
    for _ in range(TOP_K):
        mx = jnp.max(cur, axis=-1, keepdims=True)
        idx = jnp.min(jnp.where(cur == mx, lanef, 1e9), axis=-1, keepdims=True)
        hit = lanef == idx
        sel = jnp.where(hit, 1.0, sel)
        cur = jnp.where(hit, neg, cur)
    w = sel * scores
    return w / jnp.sum(w, axis=-1, keepdims=True) * ROUTED_SCALE, sel


def _outproj_kernel(x_ref, at_ref, gm_ref, mod_ref, wo_ref, nfg_ref, wr_ref, rb_ref,
                    wsg_ref, wsu_ref, wsd_ref, tri_ref,
                    y0_ref, fx_ref, gt_ref, pt_ref, cnt_ref):
    mod = mod_ref[0]
    mix = _dot(at_ref[0], wo_ref[0:DA_WIDTH, :]) + _dot(gm_ref[0], wo_ref[DA_WIDTH:, :])
    x1 = x_ref[0] + mod[2:3] * mix
    fx = (_rms_rows(x1, nfg_ref[...]) * (1.0 + mod[4:5]) + mod[3:4]).astype(BF)
    fx_ref[0] = fx
    scores = _sigmoid(_dot(fx, wr_ref[...]))
    gates, sel = _route(scores, rb_ref[...])
    rank = _dot(tri_ref[...], sel.astype(BF))
    rank = jnp.where(sel > 0.5, rank, -1.0)
    gt_ref[0] = gates.T[:N_EXPERTS]
    pt_ref[0] = rank.T[:N_EXPERTS]
    cnt_ref[0] = jnp.sum(sel, axis=0, keepdims=True).astype(jnp.int32)
    sg = _dot(fx, wsg_ref[...])
    su = _dot(fx, wsu_ref[...])
    hs = (sg * _sigmoid(sg)) * su
    y0_ref[0] = x1 + mod[5:6] * _dot(hs.astype(BF), wsd_ref[...])


def _outproj_call(x, attn, gm, mod, w_out, nfg, wr, rb, wsg, wsu, wsd, tri, tm):
    b, l, d = x.shape
    nt = l // tm
    full = lambda shape: pl.BlockSpec(shape, lambda bi, i: (0,) * len(shape))
    tokd = pl.BlockSpec((1, tm, d), lambda bi, i: (bi, i, 0))
    tokh = pl.BlockSpec((1, tm, DA_WIDTH), lambda bi, i: (bi, i, 0))
    expt = pl.BlockSpec((1, N_EXPERTS, tm), lambda bi, i: (bi, 0, i))
    ds = wsg.shape[1]
    return pl.pallas_call(
        _outproj_kernel,
        out_shape=(jax.ShapeDtypeStruct((b, l, d), F32),
                   jax.ShapeDtypeStruct((b, l, d), BF),
                   jax.ShapeDtypeStruct((b, N_EXPERTS, l), F32),
                   jax.ShapeDtypeStruct((b, N_EXPERTS, l), F32),
                   jax.ShapeDtypeStruct((b * nt, 1, LANES), jnp.int32)),
        grid=(b, nt),
        in_specs=[tokd, tokh, tokh,
                  pl.BlockSpec((1, 6, d), lambda bi, i: (bi, 0, 0)),
                  full((d, d)), full((1, d)), full((d, LANES)), full((1, LANES)),
                  full((d, ds)), full((d, ds)), full((ds, d)), full((tm, tm))],
        out_specs=(tokd, tokd, expt, expt,
                   pl.BlockSpec((1, 1, LANES), lambda bi, i: (bi * nt + i, 0, 0))),
        compiler_params=pltpu.CompilerParams(dimension_semantics=("arbitrary", "arbitrary"),
                                             vmem_limit_bytes=VMEM_LIMIT),
        name="outproj",
    )(x, attn, gm, mod, w_out, nfg, wr, rb, wsg, wsu, wsd, tri)


def _moe_kernel(cnt_ref, fx_ref, gt_ref, pt_ref, wg_ref, wu_ref, wd_ref, y0_ref, mod_ref, o_ref,
                *, sub, cap):
    epg = wg_ref.shape[0]
    tt = fx_ref.shape[1]
    nsub = tt // sub
    pair = pl.program_id(2)
    e0 = pair * epg
    sub0 = (pl.program_id(0) * pl.num_programs(1) + pl.program_id(1)) * nsub

    @pl.when(pair == 0)
    def _zero():
        o_ref[...] = jnp.zeros_like(o_ref)

    cmax = jnp.int32(0)
    for s in range(nsub):
        for k in range(epg):
            cmax = jnp.maximum(cmax, cnt_ref[(sub0 + s) * LANES + e0 + k])
    rounds = (cmax + (cap - 1)) // cap

    def do_round(r):
        slot = lax.broadcasted_iota(jnp.int32, (cap, sub), 0).astype(F32) + (r * cap).astype(F32)
        picks, xrows, grows = [], [], []
        for s in range(nsub):
            cols = slice(s * sub, (s + 1) * sub)
            hit = [pt_ref[0, pl.ds(e0 + k, 1), cols] == slot for k in range(epg)]
            grows.append([jnp.sum(jnp.where(hit[k], gt_ref[0, pl.ds(e0 + k, 1), cols], 0.0),
                                  axis=-1, keepdims=True) for k in range(epg)])
            pick = jnp.concatenate([jnp.where(h, 1.0, 0.0) for h in hit], axis=0).astype(BF)
            picks.append(pick)
            xrows.append(_dot(pick, fx_ref[0, cols, :]).astype(BF))
        outs = []
        for k in range(epg):
            xk = jnp.concatenate([xrows[s][k * cap:(k + 1) * cap] for s in range(nsub)], axis=0)
            gk = jnp.concatenate([grows[s][k] for s in range(nsub)], axis=0)
            a = _dot(xk, wg_ref[k])
            bb = _dot(xk, wu_ref[k])
            hm = (a * _sigmoid(a)) * bb * gk
            outs.append(_dot(hm.astype(BF), wd_ref[k]).astype(BF))
        for s in range(nsub):
            cols = slice(s * sub, (s + 1) * sub)
            stacked = jnp.concatenate([outs[k][s * cap:(s + 1) * cap] for k in range(epg)], axis=0)
            o_ref[0, cols, :] += lax.dot_general(picks[s], stacked, (((0,), (0,)), ((), ())),
                                                 preferred_element_type=F32)

    do_round(jnp.int32(0))

    @pl.when(rounds > 1)
    def _overflow():
        def body(r, carry):
            do_round(r)
            return carry
        lax.fori_loop(1, rounds, body, 0)

    @pl.when(pair == pl.num_programs(2) - 1)
    def _finish():
        o_ref[0] = y0_ref[0] + mod_ref[0][5:6] * o_ref[0]


def _moe_call(counts, fx, gt, pt, wg, wu, wd, y0, mod, tt, sub, cap, epg):
    b, l, d = fx.shape
    ne, _, de = wg.shape
    assert l % tt == 0 and tt % sub == 0 and ne % epg == 0 and cap % 16 == 0
    tokd = pl.BlockSpec((1, tt, d), lambda bi, i, p, cnt: (bi, i, 0))
    expt = pl.BlockSpec((1, ne, tt), lambda bi, i, p, cnt: (bi, 0, i))
    kern = functools.partial(_moe_kernel, sub=sub, cap=cap)
    grid_spec = pltpu.PrefetchScalarGridSpec(
        num_scalar_prefetch=1,
        grid=(b, l // tt, ne // epg),
        in_specs=[tokd, expt, expt,
                  pl.BlockSpec((epg, d, de), lambda bi, i, p, cnt: (p, 0, 0)),
                  pl.BlockSpec((epg, d, de), lambda bi, i, p, cnt: (p, 0, 0)),
                  pl.BlockSpec((epg, de, d), lambda bi, i, p, cnt: (p, 0, 0)),
                  tokd,
                  pl.BlockSpec((1, 6, d), lambda bi, i, p, cnt: (bi, 0, 0))],
        out_specs=tokd)
    return pl.pallas_call(
        kern,
        out_shape=jax.ShapeDtypeStruct((b, l, d), F32),
        grid_spec=grid_spec,
        compiler_params=pltpu.CompilerParams(
            dimension_semantics=("arbitrary", "arbitrary", "arbitrary"),
            vmem_limit_bytes=VMEM_LIMIT),
        name="moe",
    )(counts, fx, gt, pt, wg, wu, wd, y0, mod)


def _rope_tables(n_tokens):
    rows = n_tokens // GRID_W
    row = jnp.repeat(jnp.arange(rows, dtype=F32), GRID_W)
    col = jnp.tile(jnp.arange(GRID_W, dtype=F32), rows)
    half = DA_HEAD_DIM // 2
    inv_freq = ROPE_THETA ** (-jnp.arange(0, half, 2, dtype=F32) / half)
    ang = jnp.concatenate([row[:, None] * inv_freq, col[:, None] * inv_freq], axis=-1)
    cos, sin = jnp.cos(ang), jnp.sin(ang)
    cos64 = jnp.repeat(cos, 2, axis=-1)
    sin64 = jnp.stack([-sin, sin], axis=-1).reshape(n_tokens, DA_HEAD_DIM)
    return jnp.tile(cos64, (1, 2)), jnp.tile(sin64, (1, 2))


def kernel(x, c, ctx, c_ctx, w_ada, b_ada, norm_mix_g, w_in, q_norm_g, k_norm_g, da_lambda, subln_g, gm_ln_g, gm_ln_b, gm_ws, gm_bs, gm_out_g, w_out, norm_ffn_g, w_router, router_bias, we_gate, we_up, we_down, ws_gate, ws_up, ws_down):
    assert w_ada.shape[0] == 1, "single-layer kernel"
    b, l, d = x.shape
    lambda_init = 0.8 - 0.6 * math.exp(-0.3 * 0)
    lp = da_lambda[0].astype(F32)
    lam = (jnp.exp(jnp.sum(lp[0] * lp[1])) - jnp.exp(jnp.sum(lp[2] * lp[3])) + lambda_init).reshape(1)

    cond_rows = 16
    cond = jnp.zeros((cond_rows, d), F32).at[:b].set(c).at[b].set(c_ctx)
    ada = _ada_call(cond, w_ada[0], b_ada[0][None, :]).reshape(cond_rows, 6, d)
    mod = ada[:b]
    mod_ctx = ada[b:b + 1]

    cos, sin = _rope_tables(l)
    half = jnp.arange(LANES) // DA_HEAD_DIM
    gmat = (half[:, None] == half[None, :]).astype(BF)
    qg = jnp.tile(q_norm_g[0], 2)[None, :]
    kg = jnp.tile(k_norm_g[0], 2)[None, :]
    w_in_bf = w_in[0].astype(BF)
    bs_full = jnp.broadcast_to(gm_bs[0][:, :, None], (GM_HEADS, CHUNK, GM_HEAD_DIM))

    tm = min(512, l)
    q, k, v, gm = _inproj_call(
        x, mod, norm_mix_g, w_in_bf, qg, kg, cos, sin, gmat,
        gm_ln_g[0].reshape(1, GM_WIDTH), gm_ln_b[0].reshape(1, GM_WIDTH),
        gm_ws[0].astype(BF), bs_full, gm_out_g[0].reshape(1, GM_WIDTH), tm)
    kc, vc = _ctxproj_call(ctx, mod_ctx, norm_mix_g, w_in_bf[:, DA_WIDTH:3 * DA_WIDTH], kg, gmat)

    attn = _attn_call(lam, q, k, kc, v, vc, subln_g[0][:, None],
                      tq=min(256, l), ck=min(256, ctx.shape[1]), out_scale=1.0 - lambda_init)

    wr = jnp.zeros((d, LANES), BF).at[:, :N_EXPERTS].set(w_router[0].astype(BF))
    rb = jnp.zeros((1, LANES), F32).at[0, :N_EXPERTS].set(router_bias[0])
    tok = jnp.arange(tm)
    tri = (tok[None, :] < tok[:, None]).astype(BF)
    y0, fx, gt, pt, counts = _outproj_call(
        x, attn, gm, mod, w_out[0].astype(BF), norm_ffn_g, wr, rb,
        ws_gate[0].astype(BF), ws_up[0].astype(BF), ws_down[0].astype(BF), tri, tm)

    return _moe_call(counts.reshape(-1), fx, gt, pt,
                     we_gate[0].astype(BF), we_up[0].astype(BF), we_down[0].astype(BF),
                     y0, mod, tt=min(MOE_TILE, l), sub=tm, cap=MOE_CAP, epg=MOE_EXPERTS_PER_STEP)
```

```python
import functools
import math

import jax
import jax.numpy as jnp
from jax import lax
from jax.experimental import pallas as pl
from jax.experimental.pallas import tpu as pltpu

BF = jnp.bfloat16
F32 = jnp.float32

EPS = 1e-6
GRID_W = 64
DA_HEADS = 4
DA_HEAD_DIM = 64
DA_V_DIM = 128
DA_WIDTH = 512
GM_HEADS = 4
GM_HEAD_DIM = 128
GM_WIDTH = 512
CHUNK = 128
ROPE_THETA = 10000.0
N_EXPERTS = 32
TOP_K = 4
N_GROUPS = 4
TOPK_GROUPS = 2
GROUP_SIZE = N_EXPERTS // N_GROUPS
ROUTED_SCALE = 2.5
MOE_TILE = 1024
MOE_CAPS = (32, 64, 96, 128, 192, 256)
MOE_EXPERTS_PER_STEP = 2
LANES = 128
VMEM_LIMIT = 56 * 1024 * 1024


def _sigmoid(x):
    return 1.0 / (1.0 + jnp.exp(-x))


def _dot(a, b):
    return jnp.dot(a, b, preferred_element_type=F32)


def _dot_nt(a, b):
    return lax.dot_general(a, b, (((1,), (1,)), ((), ())), preferred_element_type=F32)


def _rms_rows(x, g):
    ms = jnp.mean(x * x, axis=-1, keepdims=True)
    return x * lax.rsqrt(ms + EPS) * g


def _group_mean_sq(y, gmat):
    y2 = y * y
    hi = y2.astype(BF)
    lo = (y2 - hi.astype(F32)).astype(BF)
    return (_dot(hi, gmat) + _dot(lo, gmat)) * (1.0 / DA_HEAD_DIM)


def _swap_pairs(y):
    lane = lax.broadcasted_iota(jnp.int32, y.shape, 1)
    nxt = pltpu.roll(y, LANES - 1, 1)
    prv = pltpu.roll(y, 1, 1)
    return jnp.where((lane & 1) == 0, nxt, prv)


def _ada_kernel(cond_ref, w_ref, b_ref, o_ref):
    c = cond_ref[...]
    s = c * _sigmoid(c)
    o_ref[...] = _dot(s.astype(BF), w_ref[...].astype(BF)) + b_ref[...]


def _ada_call(cond, w_ada, b_ada):
    rows, d = cond.shape
    n = w_ada.shape[1]
    bn = 1024
    return pl.pallas_call(
        _ada_kernel,
        out_shape=jax.ShapeDtypeStruct((rows, n), F32),
        grid=(n // bn,),
        in_specs=[pl.BlockSpec((rows, d), lambda j: (0, 0)),
                  pl.BlockSpec((d, bn), lambda j: (0, j)),
                  pl.BlockSpec((1, bn), lambda j: (0, j))],
        out_specs=pl.BlockSpec((rows, bn), lambda j: (0, j)),
        compiler_params=pltpu.CompilerParams(dimension_semantics=("arbitrary",),
                                             vmem_limit_bytes=VMEM_LIMIT),
        name="ada",
    )(cond, w_ada, b_ada)


def _inproj_kernel(x_ref, mod_ref, ng_ref, w_ref, qg_ref, kg_ref, cos_ref, sin_ref, gmat_ref,
                   lng_ref, lnb_ref, ws_ref, bs_ref, og_ref,
                   q_ref, k_ref, v_ref, gm_ref):
    tm = x_ref.shape[1]
    x = x_ref[0]
    mod = mod_ref[0]
    h = _rms_rows(x, ng_ref[...]) * (1.0 + mod[1:2]) + mod[0:1]
    p = _dot(h.astype(BF), w_ref[...])

    gmat = gmat_ref[...]
    cos = cos_ref[...]
    sin = sin_ref[...]
    for j in range(DA_HEADS):
        sl = slice(j * LANES, (j + 1) * LANES)
        qj = p[:, sl]
        qn = qj * lax.rsqrt(_group_mean_sq(qj, gmat) + EPS) * qg_ref[...]
        qr = qn * cos + _swap_pairs(qn) * sin
        q_ref[0, :, sl] = (qr * (DA_HEAD_DIM ** -0.5)).astype(BF)
        kj = p[:, DA_WIDTH + j * LANES: DA_WIDTH + (j + 1) * LANES]
        kn = kj * lax.rsqrt(_group_mean_sq(kj, gmat) + EPS) * kg_ref[...]
        kr = kn * cos + _swap_pairs(kn) * sin
        k_ref[0, :, sl] = kr.astype(BF)
    v_ref[0] = p[:, 2 * DA_WIDTH:3 * DA_WIDTH].astype(BF)

    zp = p[:, 3 * DA_WIDTH:]
    z = 0.5 * zp * (1.0 + lax.erf(zp * math.sqrt(0.5)))
    for g in range(GM_HEADS):
        sl = slice(g * LANES, (g + 1) * LANES)
        u = z[:, sl]
        vg = z[:, GM_WIDTH + g * LANES: GM_WIDTH + (g + 1) * LANES]
        mu = jnp.mean(vg, axis=-1, keepdims=True)
        xc = vg - mu
        var = jnp.mean(xc * xc, axis=-1, keepdims=True)
        vn = (xc * lax.rsqrt(var + EPS) * lng_ref[:, sl] + lnb_ref[:, sl]).astype(BF)
        for cidx in range(tm // CHUNK):
            rows = slice(cidx * CHUNK, (cidx + 1) * CHUNK)
            mixed = _dot(ws_ref[g], vn[rows]) + bs_ref[g]
            y = u[rows] * mixed
            gm_ref[0, rows, sl] = _rms_rows(y, og_ref[:, sl]).astype(BF)


def _inproj_call(x, mod, ng, w_in, qg, kg, cos, sin, gmat, lng, lnb, ws, bs, og, tm):
    b, l, d = x.shape
    nw = w_in.shape[1]
    full = lambda shape: pl.BlockSpec(shape, lambda bi, i: (0,) * len(shape))
    tok = pl.BlockSpec((1, tm, DA_WIDTH), lambda bi, i: (bi, i, 0))
    out = jax.ShapeDtypeStruct((b, l, DA_WIDTH), BF)
    return pl.pallas_call(
        _inproj_kernel,
        out_shape=(out, out, out, out),
        grid=(b, l // tm),
        in_specs=[pl.BlockSpec((1, tm, d), lambda bi, i: (bi, i, 0)),
                  pl.BlockSpec((1, 6, d), lambda bi, i: (bi, 0, 0)),
                  full((1, d)), full((d, nw)), full((1, LANES)), full((1, LANES)),
                  pl.BlockSpec((tm, LANES), lambda bi, i: (i, 0)),
                  pl.BlockSpec((tm, LANES), lambda bi, i: (i, 0)),
                  full((LANES, LANES)), full((1, GM_WIDTH)), full((1, GM_WIDTH)),
                  full((GM_HEADS, CHUNK, CHUNK)), full((GM_HEADS, CHUNK, GM_HEAD_DIM)),
                  full((1, GM_WIDTH))],
        out_specs=(tok, tok, tok, tok),
        compiler_params=pltpu.CompilerParams(dimension_semantics=("arbitrary", "arbitrary"),
                                             vmem_limit_bytes=VMEM_LIMIT),
        name="inproj",
    )(x, mod, ng, w_in, qg, kg, cos, sin, gmat, lng, lnb, ws, bs, og)


def _ctxproj_kernel(x_ref, mod_ref, ng_ref, w_ref, kg_ref, gmat_ref, k_ref, v_ref):
    x = x_ref[0]
    mod = mod_ref[0]
    h = _rms_rows(x, ng_ref[...]) * (1.0 + mod[1:2]) + mod[0:1]
    p = _dot(h.astype(BF), w_ref[...])
    gmat = gmat_ref[...]
    for j in range(DA_HEADS):
        sl = slice(j * LANES, (j + 1) * LANES)
        kj = p[:, sl]
        kn = kj * lax.rsqrt(_group_mean_sq(kj, gmat) + EPS) * kg_ref[...]
        k_ref[0, :, sl] = kn.astype(BF)
    v_ref[0] = p[:, DA_WIDTH:].astype(BF)


def _ctxproj_call(ctx, mod_ctx, ng, w_kv, kg, gmat):
    b, lc, d = ctx.shape
    full = lambda shape: pl.BlockSpec(shape, lambda bi: (0,) * len(shape))
    tok = pl.BlockSpec((1, lc, DA_WIDTH), lambda bi: (bi, 0, 0))
    out = jax.ShapeDtypeStruct((b, lc, DA_WIDTH), BF)
    return pl.pallas_call(
        _ctxproj_kernel,
        out_shape=(out, out),
        grid=(b,),
        in_specs=[pl.BlockSpec((1, lc, d), lambda bi: (bi, 0, 0)),
                  full((1, 6, d)), full((1, d)), full((d, 2 * DA_WIDTH)),
                  full((1, LANES)), full((LANES, LANES))],
        out_specs=(tok, tok),
        compiler_params=pltpu.CompilerParams(dimension_semantics=("arbitrary",),
                                             vmem_limit_bytes=VMEM_LIMIT),
        name="ctxproj",
    )(ctx, mod_ctx, ng, w_kv, kg, gmat)


def _attn_kernel(lam_ref, q_ref, kx_ref, kc_ref, vx_ref, vc_ref, sg_ref, o_ref,
                 k_scr, vt_scr, s_scr, m_scr, *, ck, out_scale):
    tq = q_ref.shape[1]
    lc = kc_ref.shape[1]
    lx = kx_ref.shape[1]
    nck = (lc + lx) // ck

    step = pl.program_id(2)
    last = pl.num_programs(2) - 1

    def stacked_queries():
        q = q_ref[0]
        lane = lax.broadcasted_iota(jnp.int32, q.shape, 1)
        zero = jnp.zeros_like(q)
        return jnp.concatenate([jnp.where(lane < DA_HEAD_DIM, q, zero),
                                jnp.where(lane >= DA_HEAD_DIM, q, zero)], axis=0)

    def score_chunk(qq, cidx, mrun):
        rows = slice(cidx * ck, (cidx + 1) * ck)
        st = _dot_nt(k_scr[rows, :], qq)
        s_scr[rows, :] = st
        return jnp.maximum(mrun, jnp.max(st.reshape(ck // 8, 8, 2 * tq), axis=0))

    def weight_chunk(m, cidx, acc, lrun):
        rows = slice(cidx * ck, (cidx + 1) * ck)
        pt = jnp.exp(s_scr[rows, :] - m)
        acc = acc + _dot(vt_scr[cidx], pt.astype(BF))
        return acc, lrun + jnp.sum(pt.reshape(ck // 8, 8, 2 * tq), axis=0)

    def finish(acc, lrun):
        r = 1.0 / jnp.sum(lrun, axis=0, keepdims=True)
        ot = acc[:, :tq] * r[:, :tq] - lam_ref[0] * (acc[:, tq:] * r[:, tq:])
        ms = jnp.mean(ot * ot, axis=0, keepdims=True)
        on = ot * lax.rsqrt(ms + EPS) * sg_ref[...] * out_scale
        o_ref[0] = on.T.astype(BF)

    mrun0 = jnp.full((8, 2 * tq), -jnp.inf, F32)
    lrun0 = jnp.zeros((8, 2 * tq), F32)
    acc0 = jnp.zeros((DA_V_DIM, 2 * tq), F32)

    @pl.when(step == 0)
    def _first():
        k_scr[0:lc, :] = kc_ref[0]
        k_scr[lc:lc + lx, :] = kx_ref[0]
        for cidx in range(nck):
            lo = cidx * ck
            if lo < lc:
                blk = vc_ref[0, lo:lo + ck, :]
            else:
                blk = vx_ref[0, lo - lc:lo - lc + ck, :]
            vt_scr[cidx] = blk.astype(F32).T.astype(BF)
        qq = stacked_queries()
        mrun = mrun0
        for cidx in range(nck):
            mrun = score_chunk(qq, cidx, mrun)
        m_scr[...] = mrun

    @pl.when((step > 0) & (step < last))
    def _steady():
        qq = stacked_queries()
        m = jnp.max(m_scr[...], axis=0, keepdims=True)
        mrun, lrun, acc = mrun0, lrun0, acc0
        for cidx in range(nck):
            acc, lrun = weight_chunk(m, cidx, acc, lrun)
            mrun = score_chunk(qq, cidx, mrun)
        finish(acc, lrun)
        m_scr[...] = mrun

    @pl.when(step == last)
    def _last():
        m = jnp.max(m_scr[...], axis=0, keepdims=True)
        lrun, acc = lrun0, acc0
        for cidx in range(nck):
            acc, lrun = weight_chunk(m, cidx, acc, lrun)
        finish(acc, lrun)


def _attn_call(lam, q, kx, kc, vx, vc, subln_col, tq, ck, out_scale):
    b, l, _ = q.shape
    lc = kc.shape[1]
    lk = lc + l
    assert lk % ck == 0 and lc % ck == 0 and l % tq == 0
    nq = l // tq
    kern = functools.partial(_attn_kernel, ck=ck, out_scale=out_scale)
    return pl.pallas_call(
        kern,
        out_shape=jax.ShapeDtypeStruct((b, l, DA_WIDTH), BF),
        grid=(b, DA_HEADS, nq + 1),
        in_specs=[pl.BlockSpec(memory_space=pltpu.SMEM),
                  pl.BlockSpec((1, tq, LANES), lambda bi, h, i: (bi, jnp.minimum(i, nq - 1), h)),
                  pl.BlockSpec((1, l, LANES), lambda bi, h, i: (bi, 0, h)),
                  pl.BlockSpec((1, lc, LANES), lambda bi, h, i: (bi, 0, h)),
                  pl.BlockSpec((1, l, LANES), lambda bi, h, i: (bi, 0, h)),
                  pl.BlockSpec((1, lc, LANES), lambda bi, h, i: (bi, 0, h)),
                  pl.BlockSpec((DA_V_DIM, 1), lambda bi, h, i: (0, 0))],
        out_specs=pl.BlockSpec((1, tq, LANES), lambda bi, h, i: (bi, jnp.maximum(i - 1, 0), h)),
        scratch_shapes=[pltpu.VMEM((lk, LANES), BF),
                        pltpu.VMEM((lk // ck, DA_V_DIM, ck), BF),
                        pltpu.VMEM((lk, 2 * tq), F32),
                        pltpu.VMEM((8, 2 * tq), F32)],
        compiler_params=pltpu.CompilerParams(
            dimension_semantics=("arbitrary", "arbitrary", "arbitrary"),
            vmem_limit_bytes=VMEM_LIMIT),
        name="attn",
    )(lam, q, kx, kc, vx, vc, subln_col)


def _route(scores, rbias):
    shape = scores.shape
    lane = lax.broadcasted_iota(jnp.int32, shape, 1)
    lanef = lane.astype(F32)
    neg = jnp.full(shape, -jnp.inf, F32)
    biased = jnp.where(lane < N_EXPERTS, scores + rbias, neg)

    gscore = []
    for g in range(N_GROUPS):
        ing = (lane >= g * GROUP_SIZE) & (lane < (g + 1) * GROUP_SIZE)
        vals = jnp.where(ing, biased, neg)
        m1 = jnp.max(vals, axis=-1, keepdims=True)
        i1 = jnp.min(jnp.where(vals == m1, lanef, 1e9), axis=-1, keepdims=True)
        m2 = jnp.max(jnp.where(lanef == i1, neg, vals), axis=-1, keepdims=True)
        gscore.append(m1 + m2)

    emask = jnp.zeros(shape, F32)
    for g in range(N_GROUPS):
        beaten = jnp.zeros_like(gscore[g])
        for g2 in range(N_GROUPS):
            if g2 == g:
                continue
            beat = (gscore[g2] >= gscore[g]) if g2 < g else (gscore[g2] > gscore[g])
            beaten = beaten + jnp.where(beat, 1.0, 0.0)
        chosen = jnp.where(beaten < TOPK_GROUPS, 1.0, 0.0)
        ing = (lane >= g * GROUP_SIZE) & (lane < (g + 1) * GROUP_SIZE)
        emask = jnp.where(ing, chosen, emask)

    cur = jnp.where(emask > 0.5, biased, neg)
    sel = jnp.zeros(shape, F32)
    for _ in range(TOP_K):
        mx = jnp.max(cur, axis=-1, keepdims=True)
        idx = jnp.min(jnp.where(cur == mx, lanef, 1e9), axis=-1, keepdims=True)
        hit = lanef == idx
        sel = jnp.where(hit, 1.0, sel)
        cur = jnp.where(hit, neg, cur)
    w = sel * scores
    return w / jnp.sum(w, axis=-1, keepdims=True) * ROUTED_SCALE, sel


def _outproj_kernel(x_ref, at_ref, gm_ref, mod_ref, wo_ref, nfg_ref, wr_ref, rb_ref,
                    wsg_ref, wsu_ref, wsd_ref, tri_ref,
                    y0_ref, fx_ref, gt_ref, pt_ref, cnt_ref):
    mod = mod_ref[0]
    mix = _dot(at_ref[0], wo_ref[0:DA_WIDTH, :]) + _dot(gm_ref[0], wo_ref[DA_WIDTH:, :])
    x1 = x_ref[0] + mod[2:3] * mix
    fx = (_rms_rows(x1, nfg_ref[...]) * (1.0 + mod[4:5]) + mod[3:4]).astype(BF)
    fx_ref[0] = fx
    scores = _sigmoid(_dot(fx, wr_ref[...]))
    gates, sel = _route(scores, rb_ref[...])
    rank = _dot(tri_ref[...], sel.astype(BF))
    rank = jnp.where(sel > 0.5, rank, -1.0)
    gt_ref[0] = gates.T[:N_EXPERTS]
    pt_ref[0] = rank.T[:N_EXPERTS]
    cnt_ref[0] = jnp.sum(sel, axis=0, keepdims=True).astype(jnp.int32)
    sg = _dot(fx, wsg_ref[...])
    su = _dot(fx, wsu_ref[...])
    hs = (sg * _sigmoid(sg)) * su
    y0_ref[0] = x1 + mod[5:6] * _dot(hs.astype(BF), wsd_ref[...])


def _outproj_call(x, attn, gm, mod, w_out, nfg, wr, rb, wsg, wsu, wsd, tri, tm):
    b, l, d = x.shape
    nt = l // tm
    full = lambda shape: pl.BlockSpec(shape, lambda bi, i: (0,) * len(shape))
    tokd = pl.BlockSpec((1, tm, d), lambda bi, i: (bi, i, 0))
    tokh = pl.BlockSpec((1, tm, DA_WIDTH), lambda bi, i: (bi, i, 0))
    expt = pl.BlockSpec((1, N_EXPERTS, tm), lambda bi, i: (bi, 0, i))
    ds = wsg.shape[1]
    return pl.pallas_call(
        _outproj_kernel,
        out_shape=(jax.ShapeDtypeStruct((b, l, d), F32),
                   jax.ShapeDtypeStruct((b, l, d), BF),
                   jax.ShapeDtypeStruct((b, N_EXPERTS, l), F32),
                   jax.ShapeDtypeStruct((b, N_EXPERTS, l), F32),
                   jax.ShapeDtypeStruct((b * nt, 1, LANES), jnp.int32)),
        grid=(b, nt),
        in_specs=[tokd, tokh, tokh,
                  pl.BlockSpec((1, 6, d), lambda bi, i: (bi, 0, 0)),
                  full((d, d)), full((1, d)), full((d, LANES)), full((1, LANES)),
                  full((d, ds)), full((d, ds)), full((ds, d)), full((tm, tm))],
        out_specs=(tokd, tokd, expt, expt,
                   pl.BlockSpec((1, 1, LANES), lambda bi, i: (bi * nt + i, 0, 0))),
        compiler_params=pltpu.CompilerParams(dimension_semantics=("arbitrary", "arbitrary"),
                                             vmem_limit_bytes=VMEM_LIMIT),
        name="outproj",
    )(x, attn, gm, mod, w_out, nfg, wr, rb, wsg, wsu, wsd, tri)


def _moe_kernel(cnt_ref, fx_ref, gt_ref, pt_ref, wg_ref, wu_ref, wd_ref, y0_ref, mod_ref, o_ref,
                *, sub, caps):
    epg = wg_ref.shape[0]
    tt = fx_ref.shape[1]
    nsub = tt // sub
    pair = pl.program_id(2)
    e0 = pair * epg
    sub0 = (pl.program_id(0) * pl.num_programs(1) + pl.program_id(1)) * nsub

    @pl.when(pair == 0)
    def _zero():
        o_ref[...] = jnp.zeros_like(o_ref)

    cmax = jnp.int32(0)
    for s in range(nsub):
        for k in range(epg):
            cmax = jnp.maximum(cmax, cnt_ref[(sub0 + s) * LANES + e0 + k])
    def do_round(r, cap):
        slot = lax.broadcasted_iota(jnp.int32, (cap, sub), 0).astype(F32) + (r * cap).astype(F32)
        picks, xrows, grows = [], [], []
        for s in range(nsub):
            cols = slice(s * sub, (s + 1) * sub)
            hit = [pt_ref[0, pl.ds(e0 + k, 1), cols] == slot for k in range(epg)]
            grows.append([jnp.sum(jnp.where(hit[k], gt_ref[0, pl.ds(e0 + k, 1), cols], 0.0),
                                  axis=-1, keepdims=True) for k in range(epg)])
            pick = jnp.concatenate([jnp.where(h, 1.0, 0.0) for h in hit], axis=0).astype(BF)
            picks.append(pick)
            xrows.append(_dot(pick, fx_ref[0, cols, :]).astype(BF))
        outs = []
        for k in range(epg):
            xk = jnp.concatenate([xrows[s][k * cap:(k + 1) * cap] for s in range(nsub)], axis=0)
            gk = jnp.concatenate([grows[s][k] for s in range(nsub)], axis=0)
            a = _dot(xk, wg_ref[k])
            bb = _dot(xk, wu_ref[k])
            hm = (a * _sigmoid(a)) * bb * gk
            outs.append(_dot(hm.astype(BF), wd_ref[k]).astype(BF))
        for s in range(nsub):
            cols = slice(s * sub, (s + 1) * sub)
            stacked = jnp.concatenate([outs[k][s * cap:(s + 1) * cap] for k in range(epg)], axis=0)
            o_ref[0, cols, :] += lax.dot_general(picks[s], stacked, (((0,), (0,)), ((), ())),
                                                 preferred_element_type=F32)

    below = 0
    for cap in caps:
        fits = (cmax <= cap) if below == 0 else ((cmax > below) & (cmax <= cap))
        pl.when(fits)(functools.partial(do_round, jnp.int32(0), cap))
        below = cap

    @pl.when(cmax > caps[-1])
    def _many_rounds():
        def body(r, carry):
            do_round(r, caps[-1])
            return carry
        lax.fori_loop(0, (cmax + (caps[-1] - 1)) // caps[-1], body, 0)

    @pl.when(pair == pl.num_programs(2) - 1)
    def _finish():
        o_ref[0] = y0_ref[0] + mod_ref[0][5:6] * o_ref[0]


def _moe_call(counts, fx, gt, pt, wg, wu, wd, y0, mod, tt, sub, caps, epg):
    b, l, d = fx.shape
    ne, _, de = wg.shape
    assert l % tt == 0 and tt % sub == 0 and ne % epg == 0
    assert all(c % 16 == 0 for c in caps) and list(caps) == sorted(caps)
    tokd = pl.BlockSpec((1, tt, d), lambda bi, i, p, cnt: (bi, i, 0))
    expt = pl.BlockSpec((1, ne, tt), lambda bi, i, p, cnt: (bi, 0, i))
    kern = functools.partial(_moe_kernel, sub=sub, caps=tuple(caps))
    grid_spec = pltpu.PrefetchScalarGridSpec(
        num_scalar_prefetch=1,
        grid=(b, l // tt, ne // epg),
        in_specs=[tokd, expt, expt,
                  pl.BlockSpec((epg, d, de), lambda bi, i, p, cnt: (p, 0, 0)),
                  pl.BlockSpec((epg, d, de), lambda bi, i, p, cnt: (p, 0, 0)),
                  pl.BlockSpec((epg, de, d), lambda bi, i, p, cnt: (p, 0, 0)),
                  tokd,
                  pl.BlockSpec((1, 6, d), lambda bi, i, p, cnt: (bi, 0, 0))],
        out_specs=tokd)
    return pl.pallas_call(
        kern,
        out_shape=jax.ShapeDtypeStruct((b, l, d), F32),
        grid_spec=grid_spec,
        compiler_params=pltpu.CompilerParams(
            dimension_semantics=("arbitrary", "arbitrary", "arbitrary"),
            vmem_limit_bytes=VMEM_LIMIT),
        name="moe",
    )(counts, fx, gt, pt, wg, wu, wd, y0, mod)


def _rope_tables(n_tokens):
    rows = n_tokens // GRID_W
    row = jnp.repeat(jnp.arange(rows, dtype=F32), GRID_W)
    col = jnp.tile(jnp.arange(GRID_W, dtype=F32), rows)
    half = DA_HEAD_DIM // 2
    inv_freq = ROPE_THETA ** (-jnp.arange(0, half, 2, dtype=F32) / half)
    ang = jnp.concatenate([row[:, None] * inv_freq, col[:, None] * inv_freq], axis=-1)
    cos, sin = jnp.cos(ang), jnp.sin(ang)
    cos64 = jnp.repeat(cos, 2, axis=-1)
    sin64 = jnp.stack([-sin, sin], axis=-1).reshape(n_tokens, DA_HEAD_DIM)
    return jnp.tile(cos64, (1, 2)), jnp.tile(sin64, (1, 2))


def kernel(x, c, ctx, c_ctx, w_ada, b_ada, norm_mix_g, w_in, q_norm_g, k_norm_g, da_lambda, subln_g, gm_ln_g, gm_ln_b, gm_ws, gm_bs, gm_out_g, w_out, norm_ffn_g, w_router, router_bias, we_gate, we_up, we_down, ws_gate, ws_up, ws_down):
    assert w_ada.shape[0] == 1, "single-layer kernel"
    b, l, d = x.shape
    lambda_init = 0.8 - 0.6 * math.exp(-0.3 * 0)
    lp = da_lambda[0].astype(F32)
    lam = (jnp.exp(jnp.sum(lp[0] * lp[1])) - jnp.exp(jnp.sum(lp[2] * lp[3])) + lambda_init).reshape(1)

    cond_rows = 16
    cond = jnp.zeros((cond_rows, d), F32).at[:b].set(c).at[b].set(c_ctx)
    ada = _ada_call(cond, w_ada[0], b_ada[0][None, :]).reshape(cond_rows, 6, d)
    mod = ada[:b]
    mod_ctx = ada[b:b + 1]

    cos, sin = _rope_tables(l)
    half = jnp.arange(LANES) // DA_HEAD_DIM
    gmat = (half[:, None] == half[None, :]).astype(BF)
    qg = jnp.tile(q_norm_g[0], 2)[None, :]
    kg = jnp.tile(k_norm_g[0], 2)[None, :]
    w_in_bf = w_in[0].astype(BF)
    bs_full = jnp.broadcast_to(gm_bs[0][:, :, None], (GM_HEADS, CHUNK, GM_HEAD_DIM))

    tm = min(512, l)
    q, k, v, gm = _inproj_call(
        x, mod, norm_mix_g, w_in_bf, qg, kg, cos, sin, gmat,
        gm_ln_g[0].reshape(1, GM_WIDTH), gm_ln_b[0].reshape(1, GM_WIDTH),
        gm_ws[0].astype(BF), bs_full, gm_out_g[0].reshape(1, GM_WIDTH), tm)
    kc, vc = _ctxproj_call(ctx, mod_ctx, norm_mix_g, w_in_bf[:, DA_WIDTH:3 * DA_WIDTH], kg, gmat)

    attn = _attn_call(lam, q, k, kc, v, vc, subln_g[0][:, None],
                      tq=min(256, l), ck=min(256, ctx.shape[1]), out_scale=1.0 - lambda_init)

    wr = jnp.zeros((d, LANES), BF).at[:, :N_EXPERTS].set(w_router[0].astype(BF))
    rb = jnp.zeros((1, LANES), F32).at[0, :N_EXPERTS].set(router_bias[0])
    tok = jnp.arange(tm)
    tri = (tok[None, :] < tok[:, None]).astype(BF)
    y0, fx, gt, pt, counts = _outproj_call(
        x, attn, gm, mod, w_out[0].astype(BF), norm_ffn_g, wr, rb,
        ws_gate[0].astype(BF), ws_up[0].astype(BF), ws_down[0].astype(BF), tri, tm)

    return _moe_call(counts.reshape(-1), fx, gt, pt,
                     we_gate[0].astype(BF), we_up[0].astype(BF), we_down[0].astype(BF),
                     y0, mod, tt=min(MOE_TILE, l), sub=tm, caps=MOE_CAPS, epg=MOE_EXPERTS_PER_STEP)
```

```python
import functools
import math

import jax
import jax.numpy as jnp
from jax import lax
from jax.experimental import pallas as pl
from jax.experimental.pallas import tpu as pltpu

BF = jnp.bfloat16
F32 = jnp.float32

EPS = 1e-6
GRID_W = 64
DA_HEADS = 4
DA_HEAD_DIM = 64
DA_V_DIM = 128
DA_WIDTH = 512
GM_HEADS = 4
GM_HEAD_DIM = 128
GM_WIDTH = 512
CHUNK = 128
ROPE_THETA = 10000.0
Q_SCALE = DA_HEAD_DIM ** -0.5 * math.log2(math.e)
N_EXPERTS = 32
TOP_K = 4
N_GROUPS = 4
TOPK_GROUPS = 2
GROUP_SIZE = N_EXPERTS // N_GROUPS
ROUTED_SCALE = 2.5
INPROJ_TILE = 512
ROUTE_TILE = 512
MOE_TILE = 1024
MOE_CAPS = (32, 64, 96, 128, 192, 256)
MOE_EXPERTS_PER_STEP = 2
LANES = 128
VMEM_LIMIT = 56 * 1024 * 1024


def _sigmoid(x):
    return 1.0 / (1.0 + jnp.exp(-x))


def _dot(a, b):
    return jnp.dot(a, b, preferred_element_type=F32)


def _dot_nt(a, b):
    return lax.dot_general(a, b, (((1,), (1,)), ((), ())), preferred_element_type=F32)


def _rms_rows(x, g):
    ms = jnp.mean(x * x, axis=-1, keepdims=True)
    return x * lax.rsqrt(ms + EPS) * g


def _group_mean_sq(y, gmat):
    y2 = y * y
    hi = y2.astype(BF)
    lo = (y2 - hi.astype(F32)).astype(BF)
    return (_dot(hi, gmat) + _dot(lo, gmat)) * (1.0 / DA_HEAD_DIM)


def _swap_pairs(y):
    lane = lax.broadcasted_iota(jnp.int32, y.shape, 1)
    nxt = pltpu.roll(y, LANES - 1, 1)
    prv = pltpu.roll(y, 1, 1)
    return jnp.where((lane & 1) == 0, nxt, prv)


def _ada_kernel(cond_ref, w_ref, b_ref, o_ref):
    c = cond_ref[...]
    s = c * _sigmoid(c)
    o_ref[...] = _dot(s.astype(BF), w_ref[...].astype(BF)) + b_ref[...]


def _ada_call(cond, w_ada, b_ada):
    rows, d = cond.shape
    n = w_ada.shape[1]
    bn = 1024
    return pl.pallas_call(
        _ada_kernel,
        out_shape=jax.ShapeDtypeStruct((rows, n), F32),
        grid=(n // bn,),
        in_specs=[pl.BlockSpec((rows, d), lambda j: (0, 0)),
                  pl.BlockSpec((d, bn), lambda j: (0, j)),
                  pl.BlockSpec((1, bn), lambda j: (0, j))],
        out_specs=pl.BlockSpec((rows, bn), lambda j: (0, j)),
        compiler_params=pltpu.CompilerParams(dimension_semantics=("arbitrary",),
                                             vmem_limit_bytes=VMEM_LIMIT),
        name="ada",
    )(cond, w_ada, b_ada)


def _inproj_kernel(x_ref, mod_ref, ng_ref, w_ref, qg_ref, kg_ref, cos_ref, sin_ref, gmat_ref,
                   lng_ref, lnb_ref, ws_ref, bs_ref, og_ref,
                   q_ref, k_ref, v_ref, gm_ref):
    tm = x_ref.shape[1]
    x = x_ref[0]
    mod = mod_ref[0]
    h = (_rms_rows(x, ng_ref[...]) * (1.0 + mod[1:2]) + mod[0:1]).astype(BF)

    def proj(col0, width=2 * LANES):
        return _dot(h, w_ref[:, col0:col0 + width])

    def gelu(t):
        return 0.5 * t * (1.0 + lax.erf(t * math.sqrt(0.5)))

    gmat = gmat_ref[...]
    cos = cos_ref[...]
    sin = sin_ref[...]
    for jb in range(DA_HEADS // 2):
        pq = proj(jb * 2 * LANES)
        pk = proj(DA_WIDTH + jb * 2 * LANES)
        for jj in range(2):
            sl = slice((2 * jb + jj) * LANES, (2 * jb + jj + 1) * LANES)
            qj = pq[:, jj * LANES:(jj + 1) * LANES]
            qn = qj * lax.rsqrt(_group_mean_sq(qj, gmat) + EPS) * qg_ref[...]
            qr = qn * cos + _swap_pairs(qn) * sin
            q_ref[0, :, sl] = (qr * Q_SCALE).astype(BF)
            kj = pk[:, jj * LANES:(jj + 1) * LANES]
            kn = kj * lax.rsqrt(_group_mean_sq(kj, gmat) + EPS) * kg_ref[...]
            kr = kn * cos + _swap_pairs(kn) * sin
            k_ref[0, :, sl] = kr.astype(BF)
    v_ref[0] = proj(2 * DA_WIDTH, DA_WIDTH).astype(BF)

    for g in range(GM_HEADS):
        sl = slice(g * LANES, (g + 1) * LANES)
        if g % 2 == 0:
            zu = gelu(proj(3 * DA_WIDTH + g * LANES))
            zv = gelu(proj(3 * DA_WIDTH + GM_WIDTH + g * LANES))
        u = zu[:, (g % 2) * LANES:(g % 2 + 1) * LANES]
        vg = zv[:, (g % 2) * LANES:(g % 2 + 1) * LANES]
        mu = jnp.mean(vg, axis=-1, keepdims=True)
        xc = vg - mu
        var = jnp.mean(xc * xc, axis=-1, keepdims=True)
        vn = (xc * lax.rsqrt(var + EPS) * lng_ref[:, sl] + lnb_ref[:, sl]).astype(BF)
        for cidx in range(tm // CHUNK):
            rows = slice(cidx * CHUNK, (cidx + 1) * CHUNK)
            mixed = _dot(ws_ref[g], vn[rows]) + bs_ref[g]
            y = u[rows] * mixed
            gm_ref[0, rows, sl] = _rms_rows(y, og_ref[:, sl]).astype(BF)


def _inproj_call(x, mod, ng, w_in, qg, kg, cos, sin, gmat, lng, lnb, ws, bs, og, tm):
    b, l, d = x.shape
    nw = w_in.shape[1]
    full = lambda shape: pl.BlockSpec(shape, lambda bi, i: (0,) * len(shape))
    tok = pl.BlockSpec((1, tm, DA_WIDTH), lambda bi, i: (bi, i, 0))
    out = jax.ShapeDtypeStruct((b, l, DA_WIDTH), BF)
    return pl.pallas_call(
        _inproj_kernel,
        out_shape=(out, out, out, out),
        grid=(b, l // tm),
        in_specs=[pl.BlockSpec((1, tm, d), lambda bi, i: (bi, i, 0)),
                  pl.BlockSpec((1, 6, d), lambda bi, i: (bi, 0, 0)),
                  full((1, d)), full((d, nw)), full((1, LANES)), full((1, LANES)),
                  pl.BlockSpec((tm, LANES), lambda bi, i: (i, 0)),
                  pl.BlockSpec((tm, LANES), lambda bi, i: (i, 0)),
                  full((LANES, LANES)), full((1, GM_WIDTH)), full((1, GM_WIDTH)),
                  full((GM_HEADS, CHUNK, CHUNK)), full((GM_HEADS, CHUNK, GM_HEAD_DIM)),
                  full((1, GM_WIDTH))],
        out_specs=(tok, tok, tok, tok),
        compiler_params=pltpu.CompilerParams(dimension_semantics=("arbitrary", "arbitrary"),
                                             vmem_limit_bytes=VMEM_LIMIT),
        name="inproj",
    )(x, mod, ng, w_in, qg, kg, cos, sin, gmat, lng, lnb, ws, bs, og)


def _ctxproj_kernel(x_ref, mod_ref, ng_ref, w_ref, kg_ref, gmat_ref, k_ref, v_ref):
    x = x_ref[0]
    mod = mod_ref[0]
    h = _rms_rows(x, ng_ref[...]) * (1.0 + mod[1:2]) + mod[0:1]
    p = _dot(h.astype(BF), w_ref[...])
    gmat = gmat_ref[...]
    for j in range(DA_HEADS):
        sl = slice(j * LANES, (j + 1) * LANES)
        kj = p[:, sl]
        kn = kj * lax.rsqrt(_group_mean_sq(kj, gmat) + EPS) * kg_ref[...]
        k_ref[0, :, sl] = kn.astype(BF)
    v_ref[0] = p[:, DA_WIDTH:].astype(BF)


def _ctxproj_call(ctx, mod_ctx, ng, w_kv, kg, gmat):
    b, lc, d = ctx.shape
    full = lambda shape: pl.BlockSpec(shape, lambda bi: (0,) * len(shape))
    tok = pl.BlockSpec((1, lc, DA_WIDTH), lambda bi: (bi, 0, 0))
    out = jax.ShapeDtypeStruct((b, lc, DA_WIDTH), BF)
    return pl.pallas_call(
        _ctxproj_kernel,
        out_shape=(out, out),
        grid=(b,),
        in_specs=[pl.BlockSpec((1, lc, d), lambda bi: (bi, 0, 0)),
                  full((1, 6, d)), full((1, d)), full((d, 2 * DA_WIDTH)),
                  full((1, LANES)), full((LANES, LANES))],
        out_specs=(tok, tok),
        compiler_params=pltpu.CompilerParams(dimension_semantics=("arbitrary",),
                                             vmem_limit_bytes=VMEM_LIMIT),
        name="ctxproj",
    )(ctx, mod_ctx, ng, w_kv, kg, gmat)


def _attn_kernel(lam_ref, q_ref, kx_ref, kc_ref, vx_ref, vc_ref, sg_ref, o_ref,
                 k_scr, vt_scr, s_scr, m_scr, *, ck, out_scale):
    tq = q_ref.shape[1]
    lc = kc_ref.shape[1]
    lx = kx_ref.shape[1]
    nck = (lc + lx) // ck

    step = pl.program_id(2)
    last = pl.num_programs(2) - 1

    def stacked_queries():
        q = q_ref[0]
        lane = lax.broadcasted_iota(jnp.int32, q.shape, 1)
        zero = jnp.zeros_like(q)
        return jnp.concatenate([jnp.where(lane < DA_HEAD_DIM, q, zero),
                                jnp.where(lane >= DA_HEAD_DIM, q, zero)], axis=0)

    def score_chunk(qq, cidx, mrun):
        rows = slice(cidx * ck, (cidx + 1) * ck)
        st = _dot_nt(k_scr[rows, :], qq)
        s_scr[rows, :] = st
        return jnp.maximum(mrun, jnp.max(st.reshape(ck // 8, 8, 2 * tq), axis=0))

    def weight_chunk(m, cidx, acc, lrun):
        rows = slice(cidx * ck, (cidx + 1) * ck)
        pt = jnp.exp2(s_scr[rows, :] - m)
        acc = acc + _dot(vt_scr[cidx], pt.astype(BF))
        return acc, lrun + jnp.sum(pt.reshape(ck // 8, 8, 2 * tq), axis=0)

    def finish(acc, lrun):
        r = 1.0 / jnp.sum(lrun, axis=0, keepdims=True)
        ot = acc[:, :tq] * r[:, :tq] - lam_ref[0] * (acc[:, tq:] * r[:, tq:])
        ms = jnp.mean(ot * ot, axis=0, keepdims=True)
        on = ot * lax.rsqrt(ms + EPS) * sg_ref[...] * out_scale
        o_ref[0] = on.T.astype(BF)

    mrun0 = jnp.full((8, 2 * tq), -jnp.inf, F32)
    lrun0 = jnp.zeros((8, 2 * tq), F32)
    acc0 = jnp.zeros((DA_V_DIM, 2 * tq), F32)

    @pl.when(step == 0)
    def _first():
        k_scr[0:lc, :] = kc_ref[0]
        k_scr[lc:lc + lx, :] = kx_ref[0]
        for cidx in range(nck):
            lo = cidx * ck
            if lo < lc:
                blk = vc_ref[0, lo:lo + ck, :]
            else:
                blk = vx_ref[0, lo - lc:lo - lc + ck, :]
            vt_scr[cidx] = blk.astype(F32).T.astype(BF)
        qq = stacked_queries()
        mrun = mrun0
        for cidx in range(nck):
            mrun = score_chunk(qq, cidx, mrun)
        m_scr[...] = mrun

    @pl.when((step > 0) & (step < last))
    def _steady():
        qq = stacked_queries()
        m = jnp.max(m_scr[...], axis=0, keepdims=True)
        mrun, lrun, acc = mrun0, lrun0, acc0
        for cidx in range(nck):
            acc, lrun = weight_chunk(m, cidx, acc, lrun)
            mrun = score_chunk(qq, cidx, mrun)
        finish(acc, lrun)
        m_scr[...] = mrun

    @pl.when(step == last)
    def _last():
        m = jnp.max(m_scr[...], axis=0, keepdims=True)
        lrun, acc = lrun0, acc0
        for cidx in range(nck):
            acc, lrun = weight_chunk(m, cidx, acc, lrun)
        finish(acc, lrun)


def _attn_call(lam, q, kx, kc, vx, vc, subln_col, tq, ck, out_scale):
    b, l, _ = q.shape
    lc = kc.shape[1]
    lk = lc + l
    assert lk % ck == 0 and lc % ck == 0 and l % tq == 0
    nq = l // tq
    kern = functools.partial(_attn_kernel, ck=ck, out_scale=out_scale)
    return pl.pallas_call(
        kern,
        out_shape=jax.ShapeDtypeStruct((b, l, DA_WIDTH), BF),
        grid=(b, DA_HEADS, nq + 1),
        in_specs=[pl.BlockSpec(memory_space=pltpu.SMEM),
                  pl.BlockSpec((1, tq, LANES), lambda bi, h, i: (bi, jnp.minimum(i, nq - 1), h)),
                  pl.BlockSpec((1, l, LANES), lambda bi, h, i: (bi, 0, h)),
                  pl.BlockSpec((1, lc, LANES), lambda bi, h, i: (bi, 0, h)),
                  pl.BlockSpec((1, l, LANES), lambda bi, h, i: (bi, 0, h)),
                  pl.BlockSpec((1, lc, LANES), lambda bi, h, i: (bi, 0, h)),
                  pl.BlockSpec((DA_V_DIM, 1), lambda bi, h, i: (0, 0))],
        out_specs=pl.BlockSpec((1, tq, LANES), lambda bi, h, i: (bi, jnp.maximum(i - 1, 0), h)),
        scratch_shapes=[pltpu.VMEM((lk, LANES), BF),
                        pltpu.VMEM((lk // ck, DA_V_DIM, ck), BF),
                        pltpu.VMEM((lk, 2 * tq), F32),
                        pltpu.VMEM((8, 2 * tq), F32)],
        compiler_params=pltpu.CompilerParams(
            dimension_semantics=("arbitrary", "arbitrary", "arbitrary"),
            vmem_limit_bytes=VMEM_LIMIT),
        name="attn",
    )(lam, q, kx, kc, vx, vc, subln_col)


def _route(scores, rbias):
    tm = scores.shape[1]
    assert GROUP_SIZE == 8 and scores.shape[0] == N_EXPERTS
    row = lax.broadcasted_iota(jnp.int32, (GROUP_SIZE, tm), 0).astype(F32)
    neg = jnp.full((GROUP_SIZE, tm), -jnp.inf, F32)
    biased = scores + rbias
    groups = [slice(g * GROUP_SIZE, (g + 1) * GROUP_SIZE) for g in range(N_GROUPS)]
    vals = [biased[sl] for sl in groups]

    gscore = []
    for v in vals:
        m1 = jnp.max(v, axis=0, keepdims=True)
        i1 = jnp.min(jnp.where(v == m1, row, 1e9), axis=0, keepdims=True)
        m2 = jnp.max(jnp.where(row == i1, neg, v), axis=0, keepdims=True)
        gscore.append(m1 + m2)

    cur = []
    for g in range(N_GROUPS):
        beaten = jnp.zeros_like(gscore[g])
        for g2 in range(N_GROUPS):
            if g2 == g:
                continue
            beat = (gscore[g2] >= gscore[g]) if g2 < g else (gscore[g2] > gscore[g])
            beaten = beaten + jnp.where(beat, 1.0, 0.0)
        cur.append(jnp.where(beaten < TOPK_GROUPS, vals[g], neg))

    ids = [row + float(g * GROUP_SIZE) for g in range(N_GROUPS)]
    sel = [jnp.zeros((GROUP_SIZE, tm), F32) for _ in range(N_GROUPS)]
    for _ in range(TOP_K):
        best = functools.reduce(jnp.maximum, cur)
        mx = jnp.max(best, axis=0, keepdims=True)
        cand = functools.reduce(jnp.minimum, [jnp.where(c == mx, i, 1e9) for c, i in zip(cur, ids)])
        idx = jnp.min(cand, axis=0, keepdims=True)
        hits = [i == idx for i in ids]
        sel = [jnp.where(h, 1.0, s) for h, s in zip(hits, sel)]
        cur = [jnp.where(h, neg, c) for h, c in zip(hits, cur)]
    w = [s * scores[sl] for s, sl in zip(sel, groups)]
    total = jnp.sum(functools.reduce(lambda a, b: a + b, w), axis=0, keepdims=True)
    gates = [wg / total * ROUTED_SCALE for wg in w]
    return jnp.concatenate(gates, axis=0), jnp.concatenate(sel, axis=0)


def _outproj_kernel(x_ref, at_ref, gm_ref, mod_ref, wo_ref, nfg_ref, wr_ref, rb_ref,
                    wsg_ref, wsu_ref, wsd_ref, tri_ref,
                    y0_ref, fx_ref, gt_ref, pt_ref, cnt_ref):
    mod = mod_ref[0]
    mix = _dot(at_ref[0], wo_ref[0:DA_WIDTH, :]) + _dot(gm_ref[0], wo_ref[DA_WIDTH:, :])
    x1 = x_ref[0] + mod[2:3] * mix
    fx = (_rms_rows(x1, nfg_ref[...]) * (1.0 + mod[4:5]) + mod[3:4]).astype(BF)
    fx_ref[0] = fx
    scores = _sigmoid(_dot_nt(wr_ref[...], fx))
    gates, sel = _route(scores, rb_ref[...])
    rank = _dot(sel.astype(BF), tri_ref[...])
    gt_ref[0] = gates
    pt_ref[0] = jnp.where(sel > 0.5, rank, -1.0)
    cnt_ref[0] = jnp.sum(sel, axis=1, keepdims=True).astype(jnp.int32)
    sg = _dot(fx, wsg_ref[...])
    su = _dot(fx, wsu_ref[...])
    hs = (sg * _sigmoid(sg)) * su
    y0_ref[0] = x1 + mod[5:6] * _dot(hs.astype(BF), wsd_ref[...])


def _outproj_call(x, attn, gm, mod, w_out, nfg, wr, rb, wsg, wsu, wsd, tri, tm):
    b, l, d = x.shape
    nt = l // tm
    full = lambda shape: pl.BlockSpec(shape, lambda bi, i: (0,) * len(shape))
    tokd = pl.BlockSpec((1, tm, d), lambda bi, i: (bi, i, 0))
    tokh = pl.BlockSpec((1, tm, DA_WIDTH), lambda bi, i: (bi, i, 0))
    expt = pl.BlockSpec((1, N_EXPERTS, tm), lambda bi, i: (bi, 0, i))
    ds = wsg.shape[1]
    return pl.pallas_call(
        _outproj_kernel,
        out_shape=(jax.ShapeDtypeStruct((b, l, d), F32),
                   jax.ShapeDtypeStruct((b, l, d), BF),
                   jax.ShapeDtypeStruct((b, N_EXPERTS, l), F32),
                   jax.ShapeDtypeStruct((b, N_EXPERTS, l), F32),
                   jax.ShapeDtypeStruct((b * nt, N_EXPERTS, 1), jnp.int32)),
        grid=(b, nt),
        in_specs=[tokd, tokh, tokh,
                  pl.BlockSpec((1, 6, d), lambda bi, i: (bi, 0, 0)),
                  full((d, d)), full((1, d)), full((N_EXPERTS, d)), full((N_EXPERTS, 1)),
                  full((d, ds)), full((d, ds)), full((ds, d)), full((tm, tm))],
        out_specs=(tokd, tokd, expt, expt,
                   pl.BlockSpec((1, N_EXPERTS, 1), lambda bi, i: (bi * nt + i, 0, 0))),
        compiler_params=pltpu.CompilerParams(dimension_semantics=("arbitrary", "arbitrary"),
                                             vmem_limit_bytes=VMEM_LIMIT),
        name="outproj",
    )(x, attn, gm, mod, w_out, nfg, wr, rb, wsg, wsu, wsd, tri)


def _moe_kernel(cnt_ref, fx_ref, gt_ref, pt_ref, wg_ref, wu_ref, wd_ref, y0_ref, mod_ref, o_ref,
                *, sub, caps):
    epg = wg_ref.shape[0]
    tt = fx_ref.shape[1]
    nsub = tt // sub
    pair = pl.program_id(2)
    e0 = pair * epg
    sub0 = (pl.program_id(0) * pl.num_programs(1) + pl.program_id(1)) * nsub

    @pl.when(pair == 0)
    def _zero():
        o_ref[...] = jnp.zeros_like(o_ref)

    cmax = jnp.int32(0)
    for s in range(nsub):
        for k in range(epg):
            cmax = jnp.maximum(cmax, cnt_ref[(sub0 + s) * N_EXPERTS + e0 + k])
    def do_round(r, cap):
        slot = lax.broadcasted_iota(jnp.int32, (cap, sub), 0).astype(F32) + (r * cap).astype(F32)
        picks, xrows, grows = [], [], []
        for s in range(nsub):
            cols = slice(s * sub, (s + 1) * sub)
            hit = [pt_ref[0, pl.ds(e0 + k, 1), cols] == slot for k in range(epg)]
            grows.append([jnp.sum(jnp.where(hit[k], gt_ref[0, pl.ds(e0 + k, 1), cols], 0.0),
                                  axis=-1, keepdims=True) for k in range(epg)])
            pick = jnp.concatenate([jnp.where(h, 1.0, 0.0) for h in hit], axis=0).astype(BF)
            picks.append(pick)
            xrows.append(_dot(pick, fx_ref[0, cols, :]).astype(BF))
        outs = []
        for k in range(epg):
            xk = jnp.concatenate([xrows[s][k * cap:(k + 1) * cap] for s in range(nsub)], axis=0)
            gk = jnp.concatenate([grows[s][k] for s in range(nsub)], axis=0)
            a = _dot(xk, wg_ref[k])
            bb = _dot(xk, wu_ref[k])
            hm = (a * _sigmoid(a)) * bb * gk
            outs.append(_dot(hm.astype(BF), wd_ref[k]).astype(BF))
        for s in range(nsub):
            cols = slice(s * sub, (s + 1) * sub)
            stacked = jnp.concatenate([outs[k][s * cap:(s + 1) * cap] for k in range(epg)], axis=0)
            o_ref[0, cols, :] += lax.dot_general(picks[s], stacked, (((0,), (0,)), ((), ())),
                                                 preferred_element_type=F32)

    below = 0
    for cap in caps:
        fits = (cmax <= cap) if below == 0 else ((cmax > below) & (cmax <= cap))
        pl.when(fits)(functools.partial(do_round, jnp.int32(0), cap))
        below = cap

    @pl.when(cmax > caps[-1])
    def _many_rounds():
        def body(r, carry):
            do_round(r, caps[-1])
            return carry
        lax.fori_loop(0, (cmax + (caps[-1] - 1)) // caps[-1], body, 0)

    @pl.when(pair == pl.num_programs(2) - 1)
    def _finish():
        o_ref[0] = y0_ref[0] + mod_ref[0][5:6] * o_ref[0]


def _moe_call(counts, fx, gt, pt, wg, wu, wd, y0, mod, tt, sub, caps, epg):
    b, l, d = fx.shape
    ne, _, de = wg.shape
    assert l % tt == 0 and tt % sub == 0 and ne % epg == 0
    assert all(c % 16 == 0 for c in caps) and list(caps) == sorted(caps)
    tokd = pl.BlockSpec((1, tt, d), lambda bi, i, p, cnt: (bi, i, 0))
    expt = pl.BlockSpec((1, ne, tt), lambda bi, i, p, cnt: (bi, 0, i))
    kern = functools.partial(_moe_kernel, sub=sub, caps=tuple(caps))
    grid_spec = pltpu.PrefetchScalarGridSpec(
        num_scalar_prefetch=1,
        grid=(b, l // tt, ne // epg),
        in_specs=[tokd, expt, expt,
                  pl.BlockSpec((epg, d, de), lambda bi, i, p, cnt: (p, 0, 0)),
                  pl.BlockSpec((epg, d, de), lambda bi, i, p, cnt: (p, 0, 0)),
                  pl.BlockSpec((epg, de, d), lambda bi, i, p, cnt: (p, 0, 0)),
                  tokd,
                  pl.BlockSpec((1, 6, d), lambda bi, i, p, cnt: (bi, 0, 0))],
        out_specs=tokd)
    return pl.pallas_call(
        kern,
        out_shape=jax.ShapeDtypeStruct((b, l, d), F32),
        grid_spec=grid_spec,
        compiler_params=pltpu.CompilerParams(
            dimension_semantics=("arbitrary", "arbitrary", "arbitrary"),
            vmem_limit_bytes=VMEM_LIMIT),
        name="moe",
    )(counts, fx, gt, pt, wg, wu, wd, y0, mod)


def _rope_tables(n_tokens):
    rows = n_tokens // GRID_W
    row = jnp.repeat(jnp.arange(rows, dtype=F32), GRID_W)
    col = jnp.tile(jnp.arange(GRID_W, dtype=F32), rows)
    half = DA_HEAD_DIM // 2
    inv_freq = ROPE_THETA ** (-jnp.arange(0, half, 2, dtype=F32) / half)
    ang = jnp.concatenate([row[:, None] * inv_freq, col[:, None] * inv_freq], axis=-1)
    cos, sin = jnp.cos(ang), jnp.sin(ang)
    cos64 = jnp.repeat(cos, 2, axis=-1)
    sin64 = jnp.stack([-sin, sin], axis=-1).reshape(n_tokens, DA_HEAD_DIM)
    return jnp.tile(cos64, (1, 2)), jnp.tile(sin64, (1, 2))


def kernel(x, c, ctx, c_ctx, w_ada, b_ada, norm_mix_g, w_in, q_norm_g, k_norm_g, da_lambda, subln_g, gm_ln_g, gm_ln_b, gm_ws, gm_bs, gm_out_g, w_out, norm_ffn_g, w_router, router_bias, we_gate, we_up, we_down, ws_gate, ws_up, ws_down):
    assert w_ada.shape[0] == 1, "single-layer kernel"
    b, l, d = x.shape
    lambda_init = 0.8 - 0.6 * math.exp(-0.3 * 0)
    lp = da_lambda[0].astype(F32)
    lam = (jnp.exp(jnp.sum(lp[0] * lp[1])) - jnp.exp(jnp.sum(lp[2] * lp[3])) + lambda_init).reshape(1)

    cond_rows = 16
    cond = jnp.zeros((cond_rows, d), F32).at[:b].set(c).at[b].set(c_ctx)
    ada = _ada_call(cond, w_ada[0], b_ada[0][None, :]).reshape(cond_rows, 6, d)
    mod = ada[:b]
    mod_ctx = ada[b:b + 1]

    cos, sin = _rope_tables(l)
    half = jnp.arange(LANES) // DA_HEAD_DIM
    gmat = (half[:, None] == half[None, :]).astype(BF)
    qg = jnp.tile(q_norm_g[0], 2)[None, :]
    kg = jnp.tile(k_norm_g[0], 2)[None, :]
    w_in_bf = w_in[0].astype(BF)
    bs_full = jnp.broadcast_to(gm_bs[0][:, :, None], (GM_HEADS, CHUNK, GM_HEAD_DIM))

    tm = min(ROUTE_TILE, l)
    q, k, v, gm = _inproj_call(
        x, mod, norm_mix_g, w_in_bf, qg, kg, cos, sin, gmat,
        gm_ln_g[0].reshape(1, GM_WIDTH), gm_ln_b[0].reshape(1, GM_WIDTH),
        gm_ws[0].astype(BF), bs_full, gm_out_g[0].reshape(1, GM_WIDTH), min(INPROJ_TILE, l))
    kc, vc = _ctxproj_call(ctx, mod_ctx, norm_mix_g, w_in_bf[:, DA_WIDTH:3 * DA_WIDTH], kg, gmat)

    attn = _attn_call(lam, q, k, kc, v, vc, subln_g[0][:, None],
                      tq=min(256, l), ck=min(256, ctx.shape[1]), out_scale=1.0 - lambda_init)

    wr = w_router[0].T.astype(BF)
    rb = router_bias[0][:, None]
    tok = jnp.arange(tm)
    tri = (tok[:, None] < tok[None, :]).astype(BF)
    y0, fx, gt, pt, counts = _outproj_call(
        x, attn, gm, mod, w_out[0].astype(BF), norm_ffn_g, wr, rb,
        ws_gate[0].astype(BF), ws_up[0].astype(BF), ws_down[0].astype(BF), tri, tm)

    return _moe_call(counts.reshape(-1), fx, gt, pt,
                     we_gate[0].astype(BF), we_up[0].astype(BF), we_down[0].astype(BF),
                     y0, mod, tt=min(MOE_TILE, l), sub=tm, caps=MOE_CAPS, epg=MOE_EXPERTS_PER_STEP)
```

```python
import functools
import math

import jax
import jax.numpy as jnp
from jax import lax
from jax.experimental import pallas as pl
from jax.experimental.pallas import tpu as pltpu

BF = jnp.bfloat16
F32 = jnp.float32

EPS = 1e-6
GRID_W = 64
DA_HEADS = 4
DA_HEAD_DIM = 64
DA_V_DIM = 128
DA_WIDTH = 512
GM_HEADS = 4
GM_HEAD_DIM = 128
GM_WIDTH = 512
CHUNK = 128
ROPE_THETA = 10000.0
Q_SCALE = DA_HEAD_DIM ** -0.5 * math.log2(math.e)
N_EXPERTS = 32
TOP_K = 4
N_GROUPS = 4
TOPK_GROUPS = 2
GROUP_SIZE = N_EXPERTS // N_GROUPS
ROUTED_SCALE = 2.5
ATTN_Q_TILE = 256
ATTN_KEY_CHUNK = 256
INPROJ_TILE = 512
ROUTE_TILE = 512
MOE_TILE = 2048
MOE_CAPS = (32, 64, 96, 128, 192, 256)
MOE_EXPERTS_PER_STEP = 2
LANES = 128
VMEM_LIMIT = 56 * 1024 * 1024


def _sigmoid(x):
    return 1.0 / (1.0 + jnp.exp(-x))


def _dot(a, b):
    return jnp.dot(a, b, preferred_element_type=F32)


def _dot_nt(a, b):
    return lax.dot_general(a, b, (((1,), (1,)), ((), ())), preferred_element_type=F32)


def _rms_rows(x, g):
    ms = jnp.mean(x * x, axis=-1, keepdims=True)
    return x * lax.rsqrt(ms + EPS) * g


def _group_mean_sq(y, gmat):
    y2 = y * y
    hi = y2.astype(BF)
    lo = (y2 - hi.astype(F32)).astype(BF)
    return (_dot(hi, gmat) + _dot(lo, gmat)) * (1.0 / DA_HEAD_DIM)


def _swap_pairs(y):
    lane = lax.broadcasted_iota(jnp.int32, y.shape, 1)
    nxt = pltpu.roll(y, LANES - 1, 1)
    prv = pltpu.roll(y, 1, 1)
    return jnp.where((lane & 1) == 0, nxt, prv)


def _ada_kernel(cond_ref, w_ref, b_ref, o_ref):
    c = cond_ref[...]
    s = c * _sigmoid(c)
    o_ref[...] = _dot(s.astype(BF), w_ref[...].astype(BF)) + b_ref[...]


def _ada_call(cond, w_ada, b_ada):
    rows, d = cond.shape
    n = w_ada.shape[1]
    bn = 1024
    return pl.pallas_call(
        _ada_kernel,
        out_shape=jax.ShapeDtypeStruct((rows, n), F32),
        grid=(n // bn,),
        in_specs=[pl.BlockSpec((rows, d), lambda j: (0, 0)),
                  pl.BlockSpec((d, bn), lambda j: (0, j)),
                  pl.BlockSpec((1, bn), lambda j: (0, j))],
        out_specs=pl.BlockSpec((rows, bn), lambda j: (0, j)),
        compiler_params=pltpu.CompilerParams(dimension_semantics=("arbitrary",),
                                             vmem_limit_bytes=VMEM_LIMIT),
        name="ada",
    )(cond, w_ada, b_ada)


def _inproj_kernel(x_ref, mod_ref, ng_ref, w_ref, qg_ref, kg_ref, cos_ref, sin_ref, gmat_ref,
                   lng_ref, lnb_ref, ws_ref, bs_ref, og_ref,
                   q_ref, k_ref, v_ref, gm_ref):
    tm = x_ref.shape[1]
    x = x_ref[0]
    mod = mod_ref[0]
    h = (_rms_rows(x, ng_ref[...]) * (1.0 + mod[1:2]) + mod[0:1]).astype(BF)

    def proj(col0, width=2 * LANES):
        return _dot(h, w_ref[:, col0:col0 + width])

    def gelu(t):
        return 0.5 * t * (1.0 + lax.erf(t * math.sqrt(0.5)))

    gmat = gmat_ref[...]
    cos = cos_ref[...]
    sin = sin_ref[...]
    for jb in range(DA_HEADS // 2):
        pq = proj(jb * 2 * LANES)
        pk = proj(DA_WIDTH + jb * 2 * LANES)
        for jj in range(2):
            sl = slice((2 * jb + jj) * LANES, (2 * jb + jj + 1) * LANES)
            qj = pq[:, jj * LANES:(jj + 1) * LANES]
            qn = qj * lax.rsqrt(_group_mean_sq(qj, gmat) + EPS) * qg_ref[...]
            qr = qn * cos + _swap_pairs(qn) * sin
            q_ref[0, :, sl] = (qr * Q_SCALE).astype(BF)
            kj = pk[:, jj * LANES:(jj + 1) * LANES]
            kn = kj * lax.rsqrt(_group_mean_sq(kj, gmat) + EPS) * kg_ref[...]
            kr = kn * cos + _swap_pairs(kn) * sin
            k_ref[0, :, sl] = kr.astype(BF)
    v_ref[0] = proj(2 * DA_WIDTH, DA_WIDTH).astype(BF)

    for g in range(GM_HEADS):
        sl = slice(g * LANES, (g + 1) * LANES)
        if g % 2 == 0:
            zu = gelu(proj(3 * DA_WIDTH + g * LANES))
            zv = gelu(proj(3 * DA_WIDTH + GM_WIDTH + g * LANES))
        u = zu[:, (g % 2) * LANES:(g % 2 + 1) * LANES]
        vg = zv[:, (g % 2) * LANES:(g % 2 + 1) * LANES]
        mu = jnp.mean(vg, axis=-1, keepdims=True)
        xc = vg - mu
        var = jnp.mean(xc * xc, axis=-1, keepdims=True)
        vn = (xc * lax.rsqrt(var + EPS) * lng_ref[:, sl] + lnb_ref[:, sl]).astype(BF)
        for cidx in range(tm // CHUNK):
            rows = slice(cidx * CHUNK, (cidx + 1) * CHUNK)
            mixed = _dot(ws_ref[g], vn[rows]) + bs_ref[g]
            y = u[rows] * mixed
            gm_ref[0, rows, sl] = _rms_rows(y, og_ref[:, sl]).astype(BF)


def _inproj_call(x, mod, ng, w_in, qg, kg, cos, sin, gmat, lng, lnb, ws, bs, og, tm):
    b, l, d = x.shape
    nw = w_in.shape[1]
    full = lambda shape: pl.BlockSpec(shape, lambda bi, i: (0,) * len(shape))
    tok = pl.BlockSpec((1, tm, DA_WIDTH), lambda bi, i: (bi, i, 0))
    out = jax.ShapeDtypeStruct((b, l, DA_WIDTH), BF)
    return pl.pallas_call(
        _inproj_kernel,
        out_shape=(out, out, out, out),
        grid=(b, l // tm),
        in_specs=[pl.BlockSpec((1, tm, d), lambda bi, i: (bi, i, 0)),
                  pl.BlockSpec((1, 6, d), lambda bi, i: (bi, 0, 0)),
                  full((1, d)), full((d, nw)), full((1, LANES)), full((1, LANES)),
                  pl.BlockSpec((tm, LANES), lambda bi, i: (i, 0)),
                  pl.BlockSpec((tm, LANES), lambda bi, i: (i, 0)),
                  full((LANES, LANES)), full((1, GM_WIDTH)), full((1, GM_WIDTH)),
                  full((GM_HEADS, CHUNK, CHUNK)), full((GM_HEADS, CHUNK, GM_HEAD_DIM)),
                  full((1, GM_WIDTH))],
        out_specs=(tok, tok, tok, tok),
        compiler_params=pltpu.CompilerParams(dimension_semantics=("arbitrary", "arbitrary"),
                                             vmem_limit_bytes=VMEM_LIMIT),
        name="inproj",
    )(x, mod, ng, w_in, qg, kg, cos, sin, gmat, lng, lnb, ws, bs, og)


def _ctxproj_kernel(x_ref, mod_ref, ng_ref, w_ref, kg_ref, gmat_ref, k_ref, v_ref):
    x = x_ref[0]
    mod = mod_ref[0]
    h = _rms_rows(x, ng_ref[...]) * (1.0 + mod[1:2]) + mod[0:1]
    p = _dot(h.astype(BF), w_ref[...])
    gmat = gmat_ref[...]
    for j in range(DA_HEADS):
        sl = slice(j * LANES, (j + 1) * LANES)
        kj = p[:, sl]
        kn = kj * lax.rsqrt(_group_mean_sq(kj, gmat) + EPS) * kg_ref[...]
        k_ref[0, :, sl] = kn.astype(BF)
    v_ref[0] = p[:, DA_WIDTH:].astype(BF)


def _ctxproj_call(ctx, mod_ctx, ng, w_kv, kg, gmat):
    b, lc, d = ctx.shape
    full = lambda shape: pl.BlockSpec(shape, lambda bi: (0,) * len(shape))
    tok = pl.BlockSpec((1, lc, DA_WIDTH), lambda bi: (bi, 0, 0))
    out = jax.ShapeDtypeStruct((b, lc, DA_WIDTH), BF)
    return pl.pallas_call(
        _ctxproj_kernel,
        out_shape=(out, out),
        grid=(b,),
        in_specs=[pl.BlockSpec((1, lc, d), lambda bi: (bi, 0, 0)),
                  full((1, 6, d)), full((1, d)), full((d, 2 * DA_WIDTH)),
                  full((1, LANES)), full((LANES, LANES))],
        out_specs=(tok, tok),
        compiler_params=pltpu.CompilerParams(dimension_semantics=("arbitrary",),
                                             vmem_limit_bytes=VMEM_LIMIT),
        name="ctxproj",
    )(ctx, mod_ctx, ng, w_kv, kg, gmat)


def _attn_kernel(lam_ref, q_ref, kx_ref, kc_ref, vx_ref, vc_ref, sg_ref, o_ref,
                 k_scr, vt_scr, s_scr, m_scr, *, ck, out_scale):
    tq = q_ref.shape[1]
    lc = kc_ref.shape[1]
    lx = kx_ref.shape[1]
    nck = (lc + lx) // ck

    step = pl.program_id(2)
    last = pl.num_programs(2) - 1

    def stacked_queries():
        q = q_ref[0]
        lane = lax.broadcasted_iota(jnp.int32, q.shape, 1)
        zero = jnp.zeros_like(q)
        return jnp.concatenate([jnp.where(lane < DA_HEAD_DIM, q, zero),
                                jnp.where(lane >= DA_HEAD_DIM, q, zero)], axis=0)

    def score_chunk(qq, cidx, mrun):
        rows = slice(cidx * ck, (cidx + 1) * ck)
        st = _dot_nt(k_scr[rows, :], qq)
        s_scr[rows, :] = st
        return jnp.maximum(mrun, jnp.max(st.reshape(ck // 8, 8, 2 * tq), axis=0))

    def weight_chunk(m, cidx, acc, lrun):
        rows = slice(cidx * ck, (cidx + 1) * ck)
        pt = jnp.exp2(s_scr[rows, :] - m)
        acc = acc + _dot(vt_scr[cidx], pt.astype(BF))
        return acc, lrun + jnp.sum(pt.reshape(ck // 8, 8, 2 * tq), axis=0)

    def finish(acc, lrun):
        r = 1.0 / jnp.sum(lrun, axis=0, keepdims=True)
        ot = acc[:, :tq] * r[:, :tq] - lam_ref[0] * (acc[:, tq:] * r[:, tq:])
        ms = jnp.mean(ot * ot, axis=0, keepdims=True)
        on = ot * lax.rsqrt(ms + EPS) * sg_ref[...] * out_scale
        o_ref[0] = on.T.astype(BF)

    mrun0 = jnp.full((8, 2 * tq), -jnp.inf, F32)
    lrun0 = jnp.zeros((8, 2 * tq), F32)
    acc0 = jnp.zeros((DA_V_DIM, 2 * tq), F32)

    @pl.when(step == 0)
    def _first():
        k_scr[0:lc, :] = kc_ref[0]
        k_scr[lc:lc + lx, :] = kx_ref[0]
        for cidx in range(nck):
            lo = cidx * ck
            if lo < lc:
                blk = vc_ref[0, lo:lo + ck, :]
            else:
                blk = vx_ref[0, lo - lc:lo - lc + ck, :]
            vt_scr[cidx] = blk.astype(F32).T.astype(BF)
        qq = stacked_queries()
        mrun = mrun0
        for cidx in range(nck):
            mrun = score_chunk(qq, cidx, mrun)
        m_scr[...] = mrun

    @pl.when((step > 0) & (step < last))
    def _steady():
        qq = stacked_queries()
        m = jnp.max(m_scr[...], axis=0, keepdims=True)
        mrun, lrun, acc = mrun0, lrun0, acc0
        for cidx in range(nck):
            acc, lrun = weight_chunk(m, cidx, acc, lrun)
            mrun = score_chunk(qq, cidx, mrun)
        finish(acc, lrun)
        m_scr[...] = mrun

    @pl.when(step == last)
    def _last():
        m = jnp.max(m_scr[...], axis=0, keepdims=True)
        lrun, acc = lrun0, acc0
        for cidx in range(nck):
            acc, lrun = weight_chunk(m, cidx, acc, lrun)
        finish(acc, lrun)


def _attn_call(lam, q, kx, kc, vx, vc, subln_col, tq, ck, out_scale):
    b, l, _ = q.shape
    lc = kc.shape[1]
    lk = lc + l
    assert lk % ck == 0 and lc % ck == 0 and l % tq == 0
    nq = l // tq
    kern = functools.partial(_attn_kernel, ck=ck, out_scale=out_scale)
    return pl.pallas_call(
        kern,
        out_shape=jax.ShapeDtypeStruct((b, l, DA_WIDTH), BF),
        grid=(b, DA_HEADS, nq + 1),
        in_specs=[pl.BlockSpec(memory_space=pltpu.SMEM),
                  pl.BlockSpec((1, tq, LANES), lambda bi, h, i: (bi, jnp.minimum(i, nq - 1), h)),
                  pl.BlockSpec((1, l, LANES), lambda bi, h, i: (bi, 0, h)),
                  pl.BlockSpec((1, lc, LANES), lambda bi, h, i: (bi, 0, h)),
                  pl.BlockSpec((1, l, LANES), lambda bi, h, i: (bi, 0, h)),
                  pl.BlockSpec((1, lc, LANES), lambda bi, h, i: (bi, 0, h)),
                  pl.BlockSpec((DA_V_DIM, 1), lambda bi, h, i: (0, 0))],
        out_specs=pl.BlockSpec((1, tq, LANES), lambda bi, h, i: (bi, jnp.maximum(i - 1, 0), h)),
        scratch_shapes=[pltpu.VMEM((lk, LANES), BF),
                        pltpu.VMEM((lk // ck, DA_V_DIM, ck), BF),
                        pltpu.VMEM((lk, 2 * tq), F32),
                        pltpu.VMEM((8, 2 * tq), F32)],
        compiler_params=pltpu.CompilerParams(
            dimension_semantics=("arbitrary", "arbitrary", "arbitrary"),
            vmem_limit_bytes=VMEM_LIMIT),
        name="attn",
    )(lam, q, kx, kc, vx, vc, subln_col)


def _route(scores, rbias):
    tm = scores.shape[1]
    assert GROUP_SIZE == 8 and scores.shape[0] == N_EXPERTS
    row = lax.broadcasted_iota(jnp.int32, (GROUP_SIZE, tm), 0).astype(F32)
    neg = jnp.full((GROUP_SIZE, tm), -jnp.inf, F32)
    biased = scores + rbias
    groups = [slice(g * GROUP_SIZE, (g + 1) * GROUP_SIZE) for g in range(N_GROUPS)]
    vals = [biased[sl] for sl in groups]

    gscore = []
    for v in vals:
        m1 = jnp.max(v, axis=0, keepdims=True)
        i1 = jnp.min(jnp.where(v == m1, row, 1e9), axis=0, keepdims=True)
        m2 = jnp.max(jnp.where(row == i1, neg, v), axis=0, keepdims=True)
        gscore.append(m1 + m2)

    cur = []
    for g in range(N_GROUPS):
        beaten = jnp.zeros_like(gscore[g])
        for g2 in range(N_GROUPS):
            if g2 == g:
                continue
            beat = (gscore[g2] >= gscore[g]) if g2 < g else (gscore[g2] > gscore[g])
            beaten = beaten + jnp.where(beat, 1.0, 0.0)
        cur.append(jnp.where(beaten < TOPK_GROUPS, vals[g], neg))

    ids = [row + float(g * GROUP_SIZE) for g in range(N_GROUPS)]
    sel = [jnp.zeros((GROUP_SIZE, tm), F32) for _ in range(N_GROUPS)]
    for _ in range(TOP_K):
        best = functools.reduce(jnp.maximum, cur)
        mx = jnp.max(best, axis=0, keepdims=True)
        cand = functools.reduce(jnp.minimum, [jnp.where(c == mx, i, 1e9) for c, i in zip(cur, ids)])
        idx = jnp.min(cand, axis=0, keepdims=True)
        hits = [i == idx for i in ids]
        sel = [jnp.where(h, 1.0, s) for h, s in zip(hits, sel)]
        cur = [jnp.where(h, neg, c) for h, c in zip(hits, cur)]
    w = [s * scores[sl] for s, sl in zip(sel, groups)]
    total = jnp.sum(functools.reduce(lambda a, b: a + b, w), axis=0, keepdims=True)
    gates = [wg / total * ROUTED_SCALE for wg in w]
    return jnp.concatenate(gates, axis=0), jnp.concatenate(sel, axis=0)


def _outproj_kernel(x_ref, at_ref, gm_ref, mod_ref, wo_ref, nfg_ref, wr_ref, rb_ref,
                    wsg_ref, wsu_ref, wsd_ref, tri_ref,
                    y0_ref, fx_ref, gt_ref, pt_ref, cnt_ref):
    mod = mod_ref[0]
    mix = _dot(at_ref[0], wo_ref[0:DA_WIDTH, :]) + _dot(gm_ref[0], wo_ref[DA_WIDTH:, :])
    x1 = x_ref[0] + mod[2:3] * mix
    fx = (_rms_rows(x1, nfg_ref[...]) * (1.0 + mod[4:5]) + mod[3:4]).astype(BF)
    fx_ref[0] = fx
    scores = _sigmoid(_dot_nt(wr_ref[...], fx))
    gates, sel = _route(scores, rb_ref[...])
    rank = _dot(sel.astype(BF), tri_ref[...])
    gt_ref[0] = gates
    pt_ref[0] = jnp.where(sel > 0.5, rank, -1.0)
    cnt_ref[0] = jnp.sum(sel, axis=1, keepdims=True).astype(jnp.int32)
    sg = _dot(fx, wsg_ref[...])
    su = _dot(fx, wsu_ref[...])
    hs = (sg * _sigmoid(sg)) * su
    y0_ref[0] = x1 + mod[5:6] * _dot(hs.astype(BF), wsd_ref[...])


def _outproj_call(x, attn, gm, mod, w_out, nfg, wr, rb, wsg, wsu, wsd, tri, tm):
    b, l, d = x.shape
    nt = l // tm
    full = lambda shape: pl.BlockSpec(shape, lambda bi, i: (0,) * len(shape))
    tokd = pl.BlockSpec((1, tm, d), lambda bi, i: (bi, i, 0))
    tokh = pl.BlockSpec((1, tm, DA_WIDTH), lambda bi, i: (bi, i, 0))
    expt = pl.BlockSpec((1, N_EXPERTS, tm), lambda bi, i: (bi, 0, i))
    ds = wsg.shape[1]
    return pl.pallas_call(
        _outproj_kernel,
        out_shape=(jax.ShapeDtypeStruct((b, l, d), F32),
                   jax.ShapeDtypeStruct((b, l, d), BF),
                   jax.ShapeDtypeStruct((b, N_EXPERTS, l), F32),
                   jax.ShapeDtypeStruct((b, N_EXPERTS, l), F32),
                   jax.ShapeDtypeStruct((b * nt, N_EXPERTS, 1), jnp.int32)),
        grid=(b, nt),
        in_specs=[tokd, tokh, tokh,
                  pl.BlockSpec((1, 6, d), lambda bi, i: (bi, 0, 0)),
                  full((d, d)), full((1, d)), full((N_EXPERTS, d)), full((N_EXPERTS, 1)),
                  full((d, ds)), full((d, ds)), full((ds, d)), full((tm, tm))],
        out_specs=(tokd, tokd, expt, expt,
                   pl.BlockSpec((1, N_EXPERTS, 1), lambda bi, i: (bi * nt + i, 0, 0))),
        compiler_params=pltpu.CompilerParams(dimension_semantics=("arbitrary", "arbitrary"),
                                             vmem_limit_bytes=VMEM_LIMIT),
        name="outproj",
    )(x, attn, gm, mod, w_out, nfg, wr, rb, wsg, wsu, wsd, tri)


def _moe_kernel(cnt_ref, fx_ref, gt_ref, pt_ref, wg_ref, wu_ref, wd_ref, y0_ref, mod_ref, o_ref,
                *, sub, caps):
    epg = wg_ref.shape[0]
    tt = fx_ref.shape[1]
    nsub = tt // sub
    pair = pl.program_id(2)
    e0 = pair * epg
    sub0 = (pl.program_id(0) * pl.num_programs(1) + pl.program_id(1)) * nsub

    @pl.when(pair == 0)
    def _zero():
        o_ref[...] = jnp.zeros_like(o_ref)

    cmax = jnp.int32(0)
    for s in range(nsub):
        for k in range(epg):
            cmax = jnp.maximum(cmax, cnt_ref[(sub0 + s) * N_EXPERTS + e0 + k])
    def do_round(r, cap):
        slot = lax.broadcasted_iota(jnp.int32, (cap, sub), 0).astype(F32) + (r * cap).astype(F32)
        picks, xrows, grows = [], [], []
        for s in range(nsub):
            cols = slice(s * sub, (s + 1) * sub)
            hit = [pt_ref[0, pl.ds(e0 + k, 1), cols] == slot for k in range(epg)]
            grows.append([jnp.sum(jnp.where(hit[k], gt_ref[0, pl.ds(e0 + k, 1), cols], 0.0),
                                  axis=-1, keepdims=True) for k in range(epg)])
            pick = jnp.concatenate([jnp.where(h, 1.0, 0.0) for h in hit], axis=0).astype(BF)
            picks.append(pick)
            xrows.append(_dot(pick, fx_ref[0, cols, :]).astype(BF))
        outs = []
        for k in range(epg):
            xk = jnp.concatenate([xrows[s][k * cap:(k + 1) * cap] for s in range(nsub)], axis=0)
            gk = jnp.concatenate([grows[s][k] for s in range(nsub)], axis=0)
            a = _dot(xk, wg_ref[k])
            bb = _dot(xk, wu_ref[k])
            hm = (a * _sigmoid(a)) * bb * gk
            outs.append(_dot(hm.astype(BF), wd_ref[k]).astype(BF))
        for s in range(nsub):
            cols = slice(s * sub, (s + 1) * sub)
            stacked = jnp.concatenate([outs[k][s * cap:(s + 1) * cap] for k in range(epg)], axis=0)
            o_ref[0, cols, :] += lax.dot_general(picks[s], stacked, (((0,), (0,)), ((), ())),
                                                 preferred_element_type=F32)

    below = 0
    for cap in caps:
        fits = (cmax <= cap) if below == 0 else ((cmax > below) & (cmax <= cap))
        pl.when(fits)(functools.partial(do_round, jnp.int32(0), cap))
        below = cap

    @pl.when(cmax > caps[-1])
    def _many_rounds():
        def body(r, carry):
            do_round(r, caps[-1])
            return carry
        lax.fori_loop(0, (cmax + (caps[-1] - 1)) // caps[-1], body, 0)

    @pl.when(pair == pl.num_programs(2) - 1)
    def _finish():
        o_ref[0] = y0_ref[0] + mod_ref[0][5:6] * o_ref[0]


def _moe_call(counts, fx, gt, pt, wg, wu, wd, y0, mod, tt, sub, caps, epg):
    b, l, d = fx.shape
    ne, _, de = wg.shape
    assert l % tt == 0 and tt % sub == 0 and ne % epg == 0
    assert all(c % 16 == 0 for c in caps) and list(caps) == sorted(caps)
    tokd = pl.BlockSpec((1, tt, d), lambda bi, i, p, cnt: (bi, i, 0))
    expt = pl.BlockSpec((1, ne, tt), lambda bi, i, p, cnt: (bi, 0, i))
    kern = functools.partial(_moe_kernel, sub=sub, caps=tuple(caps))
    grid_spec = pltpu.PrefetchScalarGridSpec(
        num_scalar_prefetch=1,
        grid=(b, l // tt, ne // epg),
        in_specs=[tokd, expt, expt,
                  pl.BlockSpec((epg, d, de), lambda bi, i, p, cnt: (p, 0, 0)),
                  pl.BlockSpec((epg, d, de), lambda bi, i, p, cnt: (p, 0, 0)),
                  pl.BlockSpec((epg, de, d), lambda bi, i, p, cnt: (p, 0, 0)),
                  tokd,
                  pl.BlockSpec((1, 6, d), lambda bi, i, p, cnt: (bi, 0, 0))],
        out_specs=tokd)
    return pl.pallas_call(
        kern,
        out_shape=jax.ShapeDtypeStruct((b, l, d), F32),
        grid_spec=grid_spec,
        compiler_params=pltpu.CompilerParams(
            dimension_semantics=("arbitrary", "arbitrary", "arbitrary"),
            vmem_limit_bytes=VMEM_LIMIT),
        name="moe",
    )(counts, fx, gt, pt, wg, wu, wd, y0, mod)


def _rope_tables(n_tokens):
    rows = n_tokens // GRID_W
    row = jnp.repeat(jnp.arange(rows, dtype=F32), GRID_W)
    col = jnp.tile(jnp.arange(GRID_W, dtype=F32), rows)
    half = DA_HEAD_DIM // 2
    inv_freq = ROPE_THETA ** (-jnp.arange(0, half, 2, dtype=F32) / half)
    ang = jnp.concatenate([row[:, None] * inv_freq, col[:, None] * inv_freq], axis=-1)
    cos, sin = jnp.cos(ang), jnp.sin(ang)
    cos64 = jnp.repeat(cos, 2, axis=-1)
    sin64 = jnp.stack([-sin, sin], axis=-1).reshape(n_tokens, DA_HEAD_DIM)
    return jnp.tile(cos64, (1, 2)), jnp.tile(sin64, (1, 2))


def kernel(x, c, ctx, c_ctx, w_ada, b_ada, norm_mix_g, w_in, q_norm_g, k_norm_g, da_lambda, subln_g, gm_ln_g, gm_ln_b, gm_ws, gm_bs, gm_out_g, w_out, norm_ffn_g, w_router, router_bias, we_gate, we_up, we_down, ws_gate, ws_up, ws_down):
    assert w_ada.shape[0] == 1, "single-layer kernel"
    b, l, d = x.shape
    lambda_init = 0.8 - 0.6 * math.exp(-0.3 * 0)
    lp = da_lambda[0].astype(F32)
    lam = (jnp.exp(jnp.sum(lp[0] * lp[1])) - jnp.exp(jnp.sum(lp[2] * lp[3])) + lambda_init).reshape(1)

    cond_rows = 16
    cond = jnp.zeros((cond_rows, d), F32).at[:b].set(c).at[b].set(c_ctx)
    ada = _ada_call(cond, w_ada[0], b_ada[0][None, :]).reshape(cond_rows, 6, d)
    mod = ada[:b]
    mod_ctx = ada[b:b + 1]

    cos, sin = _rope_tables(l)
    half = jnp.arange(LANES) // DA_HEAD_DIM
    gmat = (half[:, None] == half[None, :]).astype(BF)
    qg = jnp.tile(q_norm_g[0], 2)[None, :]
    kg = jnp.tile(k_norm_g[0], 2)[None, :]
    w_in_bf = w_in[0].astype(BF)
    bs_full = jnp.broadcast_to(gm_bs[0][:, :, None], (GM_HEADS, CHUNK, GM_HEAD_DIM))

    tm = min(ROUTE_TILE, l)
    q, k, v, gm = _inproj_call(
        x, mod, norm_mix_g, w_in_bf, qg, kg, cos, sin, gmat,
        gm_ln_g[0].reshape(1, GM_WIDTH), gm_ln_b[0].reshape(1, GM_WIDTH),
        gm_ws[0].astype(BF), bs_full, gm_out_g[0].reshape(1, GM_WIDTH), min(INPROJ_TILE, l))
    kc, vc = _ctxproj_call(ctx, mod_ctx, norm_mix_g, w_in_bf[:, DA_WIDTH:3 * DA_WIDTH], kg, gmat)

    attn = _attn_call(lam, q, k, kc, v, vc, subln_g[0][:, None],
                      tq=min(ATTN_Q_TILE, l), ck=min(ATTN_KEY_CHUNK, ctx.shape[1]),
                      out_scale=1.0 - lambda_init)

    wr = w_router[0].T.astype(BF)
    rb = router_bias[0][:, None]
    tok = jnp.arange(tm)
    tri = (tok[:, None] < tok[None, :]).astype(BF)
    y0, fx, gt, pt, counts = _outproj_call(
        x, attn, gm, mod, w_out[0].astype(BF), norm_ffn_g, wr, rb,
        ws_gate[0].astype(BF), ws_up[0].astype(BF), ws_down[0].astype(BF), tri, tm)

    return _moe_call(counts.reshape(-1), fx, gt, pt,
                     we_gate[0].astype(BF), we_up[0].astype(BF), we_down[0].astype(BF),
                     y0, mod, tt=min(MOE_TILE, l), sub=tm, caps=MOE_CAPS, epg=MOE_EXPERTS_PER_STEP)
```

```python
import functools
import math

import jax
import jax.numpy as jnp
from jax import lax
from jax.experimental import pallas as pl
from jax.experimental.pallas import tpu as pltpu

BF = jnp.bfloat16
F32 = jnp.float32

EPS = 1e-6
GRID_W = 64
DA_HEADS = 4
DA_HEAD_DIM = 64
DA_V_DIM = 128
DA_WIDTH = 512
GM_HEADS = 4
GM_HEAD_DIM = 128
GM_WIDTH = 512
CHUNK = 128
ROPE_THETA = 10000.0
Q_SCALE = DA_HEAD_DIM ** -0.5 * math.log2(math.e)
N_EXPERTS = 32
TOP_K = 4
N_GROUPS = 4
TOPK_GROUPS = 2
GROUP_SIZE = N_EXPERTS // N_GROUPS
ROUTED_SCALE = 2.5
ATTN_Q_TILE = 256
ATTN_KEY_CHUNK = 256
INPROJ_TILE = 512
ROUTE_TILE = 512
MOE_TILE = 2048
MOE_CAPS = (32, 48, 64, 80, 96, 112, 128, 160, 192, 224, 256)
MOE_EXPERTS_PER_STEP = 2
LANES = 128
VMEM_LIMIT = 56 * 1024 * 1024


def _sigmoid(x):
    return 1.0 / (1.0 + jnp.exp(-x))


def _dot(a, b):
    return jnp.dot(a, b, preferred_element_type=F32)


def _dot_nt(a, b):
    return lax.dot_general(a, b, (((1,), (1,)), ((), ())), preferred_element_type=F32)


def _rms_rows(x, g):
    ms = jnp.mean(x * x, axis=-1, keepdims=True)
    return x * lax.rsqrt(ms + EPS) * g


def _group_mean_sq(y, gmat):
    y2 = y * y
    hi = y2.astype(BF)
    lo = (y2 - hi.astype(F32)).astype(BF)
    return (_dot(hi, gmat) + _dot(lo, gmat)) * (1.0 / DA_HEAD_DIM)


def _swap_pairs(y):
    lane = lax.broadcasted_iota(jnp.int32, y.shape, 1)
    nxt = pltpu.roll(y, LANES - 1, 1)
    prv = pltpu.roll(y, 1, 1)
    return jnp.where((lane & 1) == 0, nxt, prv)


def _ada_kernel(cond_ref, w_ref, b_ref, o_ref):
    c = cond_ref[...]
    s = c * _sigmoid(c)
    o_ref[...] = _dot(s.astype(BF), w_ref[...].astype(BF)) + b_ref[...]


def _ada_call(cond, w_ada, b_ada):
    rows, d = cond.shape
    n = w_ada.shape[1]
    bn = 1024
    return pl.pallas_call(
        _ada_kernel,
        out_shape=jax.ShapeDtypeStruct((rows, n), F32),
        grid=(n // bn,),
        in_specs=[pl.BlockSpec((rows, d), lambda j: (0, 0)),
                  pl.BlockSpec((d, bn), lambda j: (0, j)),
                  pl.BlockSpec((1, bn), lambda j: (0, j))],
        out_specs=pl.BlockSpec((rows, bn), lambda j: (0, j)),
        compiler_params=pltpu.CompilerParams(dimension_semantics=("arbitrary",),
                                             vmem_limit_bytes=VMEM_LIMIT),
        name="ada",
    )(cond, w_ada, b_ada)


def _inproj_kernel(x_ref, mod_ref, ng_ref, w_ref, qg_ref, kg_ref, cos_ref, sin_ref, gmat_ref,
                   lng_ref, lnb_ref, ws_ref, bs_ref, og_ref,
                   q_ref, k_ref, v_ref, gm_ref):
    tm = x_ref.shape[1]
    x = x_ref[0]
    mod = mod_ref[0]
    h = (_rms_rows(x, ng_ref[...]) * (1.0 + mod[1:2]) + mod[0:1]).astype(BF)

    def proj(col0, width=2 * LANES):
        return _dot(h, w_ref[:, col0:col0 + width])

    def gelu(t):
        return 0.5 * t * (1.0 + lax.erf(t * math.sqrt(0.5)))

    gmat = gmat_ref[...]
    cos = cos_ref[...]
    sin = sin_ref[...]
    for jb in range(DA_HEADS // 2):
        pq = proj(jb * 2 * LANES)
        pk = proj(DA_WIDTH + jb * 2 * LANES)
        for jj in range(2):
            sl = slice((2 * jb + jj) * LANES, (2 * jb + jj + 1) * LANES)
            qj = pq[:, jj * LANES:(jj + 1) * LANES]
            qn = qj * lax.rsqrt(_group_mean_sq(qj, gmat) + EPS) * qg_ref[...]
            qr = qn * cos + _swap_pairs(qn) * sin
            q_ref[0, :, sl] = (qr * Q_SCALE).astype(BF)
            kj = pk[:, jj * LANES:(jj + 1) * LANES]
            kn = kj * lax.rsqrt(_group_mean_sq(kj, gmat) + EPS) * kg_ref[...]
            kr = kn * cos + _swap_pairs(kn) * sin
            k_ref[0, :, sl] = kr.astype(BF)
    v_ref[0] = proj(2 * DA_WIDTH, DA_WIDTH).astype(BF)

    for g in range(GM_HEADS):
        sl = slice(g * LANES, (g + 1) * LANES)
        if g % 2 == 0:
            zu = gelu(proj(3 * DA_WIDTH + g * LANES))
            zv = gelu(proj(3 * DA_WIDTH + GM_WIDTH + g * LANES))
        u = zu[:, (g % 2) * LANES:(g % 2 + 1) * LANES]
        vg = zv[:, (g % 2) * LANES:(g % 2 + 1) * LANES]
        mu = jnp.mean(vg, axis=-1, keepdims=True)
        xc = vg - mu
        var = jnp.mean(xc * xc, axis=-1, keepdims=True)
        vn = (xc * lax.rsqrt(var + EPS) * lng_ref[:, sl] + lnb_ref[:, sl]).astype(BF)
        for cidx in range(tm // CHUNK):
            rows = slice(cidx * CHUNK, (cidx + 1) * CHUNK)
            mixed = _dot(ws_ref[g], vn[rows]) + bs_ref[g]
            y = u[rows] * mixed
            gm_ref[0, rows, sl] = _rms_rows(y, og_ref[:, sl]).astype(BF)


def _inproj_call(x, mod, ng, w_in, qg, kg, cos, sin, gmat, lng, lnb, ws, bs, og, tm):
    b, l, d = x.shape
    nw = w_in.shape[1]
    full = lambda shape: pl.BlockSpec(shape, lambda bi, i: (0,) * len(shape))
    tok = pl.BlockSpec((1, tm, DA_WIDTH), lambda bi, i: (bi, i, 0))
    out = jax.ShapeDtypeStruct((b, l, DA_WIDTH), BF)
    return pl.pallas_call(
        _inproj_kernel,
        out_shape=(out, out, out, out),
        grid=(b, l // tm),
        in_specs=[pl.BlockSpec((1, tm, d), lambda bi, i: (bi, i, 0)),
                  pl.BlockSpec((1, 6, d), lambda bi, i: (bi, 0, 0)),
                  full((1, d)), full((d, nw)), full((1, LANES)), full((1, LANES)),
                  pl.BlockSpec((tm, LANES), lambda bi, i: (i, 0)),
                  pl.BlockSpec((tm, LANES), lambda bi, i: (i, 0)),
                  full((LANES, LANES)), full((1, GM_WIDTH)), full((1, GM_WIDTH)),
                  full((GM_HEADS, CHUNK, CHUNK)), full((GM_HEADS, CHUNK, GM_HEAD_DIM)),
                  full((1, GM_WIDTH))],
        out_specs=(tok, tok, tok, tok),
        compiler_params=pltpu.CompilerParams(dimension_semantics=("arbitrary", "arbitrary"),
                                             vmem_limit_bytes=VMEM_LIMIT),
        name="inproj",
    )(x, mod, ng, w_in, qg, kg, cos, sin, gmat, lng, lnb, ws, bs, og)


def _ctxproj_kernel(x_ref, mod_ref, ng_ref, w_ref, kg_ref, gmat_ref, k_ref, v_ref):
    x = x_ref[0]
    mod = mod_ref[0]
    h = _rms_rows(x, ng_ref[...]) * (1.0 + mod[1:2]) + mod[0:1]
    p = _dot(h.astype(BF), w_ref[...])
    gmat = gmat_ref[...]
    for j in range(DA_HEADS):
        sl = slice(j * LANES, (j + 1) * LANES)
        kj = p[:, sl]
        kn = kj * lax.rsqrt(_group_mean_sq(kj, gmat) + EPS) * kg_ref[...]
        k_ref[0, :, sl] = kn.astype(BF)
    v_ref[0] = p[:, DA_WIDTH:].astype(BF)


def _ctxproj_call(ctx, mod_ctx, ng, w_kv, kg, gmat):
    b, lc, d = ctx.shape
    full = lambda shape: pl.BlockSpec(shape, lambda bi: (0,) * len(shape))
    tok = pl.BlockSpec((1, lc, DA_WIDTH), lambda bi: (bi, 0, 0))
    out = jax.ShapeDtypeStruct((b, lc, DA_WIDTH), BF)
    return pl.pallas_call(
        _ctxproj_kernel,
        out_shape=(out, out),
        grid=(b,),
        in_specs=[pl.BlockSpec((1, lc, d), lambda bi: (bi, 0, 0)),
                  full((1, 6, d)), full((1, d)), full((d, 2 * DA_WIDTH)),
                  full((1, LANES)), full((LANES, LANES))],
        out_specs=(tok, tok),
        compiler_params=pltpu.CompilerParams(dimension_semantics=("arbitrary",),
                                             vmem_limit_bytes=VMEM_LIMIT),
        name="ctxproj",
    )(ctx, mod_ctx, ng, w_kv, kg, gmat)


def _attn_kernel(lam_ref, q_ref, kx_ref, kc_ref, vx_ref, vc_ref, sg_ref, o_ref,
                 k_scr, vt_scr, s_scr, m_scr, *, ck, out_scale):
    tq = q_ref.shape[1]
    lc = kc_ref.shape[1]
    lx = kx_ref.shape[1]
    nck = (lc + lx) // ck

    step = pl.program_id(2)
    last = pl.num_programs(2) - 1

    def stacked_queries():
        q = q_ref[0]
        lane = lax.broadcasted_iota(jnp.int32, q.shape, 1)
        zero = jnp.zeros_like(q)
        return jnp.concatenate([jnp.where(lane < DA_HEAD_DIM, q, zero),
                                jnp.where(lane >= DA_HEAD_DIM, q, zero)], axis=0)

    def score_chunk(qq, cidx, mrun):
        rows = slice(cidx * ck, (cidx + 1) * ck)
        st = _dot_nt(k_scr[rows, :], qq)
        s_scr[rows, :] = st
        return jnp.maximum(mrun, jnp.max(st.reshape(ck // 8, 8, 2 * tq), axis=0))

    def weight_chunk(m, cidx, acc, lrun):
        rows = slice(cidx * ck, (cidx + 1) * ck)
        pt = jnp.exp2(s_scr[rows, :] - m)
        acc = acc + _dot(vt_scr[cidx], pt.astype(BF))
        return acc, lrun + jnp.sum(pt.reshape(ck // 8, 8, 2 * tq), axis=0)

    def finish(acc, lrun):
        r = 1.0 / jnp.sum(lrun, axis=0, keepdims=True)
        ot = acc[:, :tq] * r[:, :tq] - lam_ref[0] * (acc[:, tq:] * r[:, tq:])
        ms = jnp.mean(ot * ot, axis=0, keepdims=True)
        on = ot * lax.rsqrt(ms + EPS) * sg_ref[...] * out_scale
        o_ref[0] = on.T.astype(BF)

    mrun0 = jnp.full((8, 2 * tq), -jnp.inf, F32)
    lrun0 = jnp.zeros((8, 2 * tq), F32)
    acc0 = jnp.zeros((DA_V_DIM, 2 * tq), F32)

    @pl.when(step == 0)
    def _first():
        k_scr[0:lc, :] = kc_ref[0]
        k_scr[lc:lc + lx, :] = kx_ref[0]
        for cidx in range(nck):
            lo = cidx * ck
            if lo < lc:
                blk = vc_ref[0, lo:lo + ck, :]
            else:
                blk = vx_ref[0, lo - lc:lo - lc + ck, :]
            vt_scr[cidx] = blk.astype(F32).T.astype(BF)
        qq = stacked_queries()
        mrun = mrun0
        for cidx in range(nck):
            mrun = score_chunk(qq, cidx, mrun)
        m_scr[...] = mrun

    @pl.when((step > 0) & (step < last))
    def _steady():
        qq = stacked_queries()
        m = jnp.max(m_scr[...], axis=0, keepdims=True)
        mrun, lrun, acc = mrun0, lrun0, acc0
        for cidx in range(nck):
            acc, lrun = weight_chunk(m, cidx, acc, lrun)
            mrun = score_chunk(qq, cidx, mrun)
        finish(acc, lrun)
        m_scr[...] = mrun

    @pl.when(step == last)
    def _last():
        m = jnp.max(m_scr[...], axis=0, keepdims=True)
        lrun, acc = lrun0, acc0
        for cidx in range(nck):
            acc, lrun = weight_chunk(m, cidx, acc, lrun)
        finish(acc, lrun)


def _attn_call(lam, q, kx, kc, vx, vc, subln_col, tq, ck, out_scale):
    b, l, _ = q.shape
    lc = kc.shape[1]
    lk = lc + l
    assert lk % ck == 0 and lc % ck == 0 and l % tq == 0
    nq = l // tq
    kern = functools.partial(_attn_kernel, ck=ck, out_scale=out_scale)
    return pl.pallas_call(
        kern,
        out_shape=jax.ShapeDtypeStruct((b, l, DA_WIDTH), BF),
        grid=(b, DA_HEADS, nq + 1),
        in_specs=[pl.BlockSpec(memory_space=pltpu.SMEM),
                  pl.BlockSpec((1, tq, LANES), lambda bi, h, i: (bi, jnp.minimum(i, nq - 1), h)),
                  pl.BlockSpec((1, l, LANES), lambda bi, h, i: (bi, 0, h)),
                  pl.BlockSpec((1, lc, LANES), lambda bi, h, i: (bi, 0, h)),
                  pl.BlockSpec((1, l, LANES), lambda bi, h, i: (bi, 0, h)),
                  pl.BlockSpec((1, lc, LANES), lambda bi, h, i: (bi, 0, h)),
                  pl.BlockSpec((DA_V_DIM, 1), lambda bi, h, i: (0, 0))],
        out_specs=pl.BlockSpec((1, tq, LANES), lambda bi, h, i: (bi, jnp.maximum(i - 1, 0), h)),
        scratch_shapes=[pltpu.VMEM((lk, LANES), BF),
                        pltpu.VMEM((lk // ck, DA_V_DIM, ck), BF),
                        pltpu.VMEM((lk, 2 * tq), F32),
                        pltpu.VMEM((8, 2 * tq), F32)],
        compiler_params=pltpu.CompilerParams(
            dimension_semantics=("arbitrary", "arbitrary", "arbitrary"),
            vmem_limit_bytes=VMEM_LIMIT),
        name="attn",
    )(lam, q, kx, kc, vx, vc, subln_col)


def _route(scores, rbias):
    tm = scores.shape[1]
    assert GROUP_SIZE == 8 and scores.shape[0] == N_EXPERTS
    row = lax.broadcasted_iota(jnp.int32, (GROUP_SIZE, tm), 0).astype(F32)
    neg = jnp.full((GROUP_SIZE, tm), -jnp.inf, F32)
    biased = scores + rbias
    groups = [slice(g * GROUP_SIZE, (g + 1) * GROUP_SIZE) for g in range(N_GROUPS)]
    vals = [biased[sl] for sl in groups]

    gscore = []
    for v in vals:
        m1 = jnp.max(v, axis=0, keepdims=True)
        i1 = jnp.min(jnp.where(v == m1, row, 1e9), axis=0, keepdims=True)
        m2 = jnp.max(jnp.where(row == i1, neg, v), axis=0, keepdims=True)
        gscore.append(m1 + m2)

    cur = []
    for g in range(N_GROUPS):
        beaten = jnp.zeros_like(gscore[g])
        for g2 in range(N_GROUPS):
            if g2 == g:
                continue
            beat = (gscore[g2] >= gscore[g]) if g2 < g else (gscore[g2] > gscore[g])
            beaten = beaten + jnp.where(beat, 1.0, 0.0)
        cur.append(jnp.where(beaten < TOPK_GROUPS, vals[g], neg))

    ids = [row + float(g * GROUP_SIZE) for g in range(N_GROUPS)]
    sel = [jnp.zeros((GROUP_SIZE, tm), F32) for _ in range(N_GROUPS)]
    for _ in range(TOP_K):
        best = functools.reduce(jnp.maximum, cur)
        mx = jnp.max(best, axis=0, keepdims=True)
        cand = functools.reduce(jnp.minimum, [jnp.where(c == mx, i, 1e9) for c, i in zip(cur, ids)])
        idx = jnp.min(cand, axis=0, keepdims=True)
        hits = [i == idx for i in ids]
        sel = [jnp.where(h, 1.0, s) for h, s in zip(hits, sel)]
        cur = [jnp.where(h, neg, c) for h, c in zip(hits, cur)]
    w = [s * scores[sl] for s, sl in zip(sel, groups)]
    total = jnp.sum(functools.reduce(lambda a, b: a + b, w), axis=0, keepdims=True)
    gates = [wg / total * ROUTED_SCALE for wg in w]
    return jnp.concatenate(gates, axis=0), jnp.concatenate(sel, axis=0)


def _outproj_kernel(x_ref, at_ref, gm_ref, mod_ref, wo_ref, nfg_ref, wr_ref, rb_ref,
                    wsg_ref, wsu_ref, wsd_ref, tri_ref,
                    y0_ref, fx_ref, gt_ref, pt_ref, cnt_ref):
    mod = mod_ref[0]
    mix = _dot(at_ref[0], wo_ref[0:DA_WIDTH, :]) + _dot(gm_ref[0], wo_ref[DA_WIDTH:, :])
    x1 = x_ref[0] + mod[2:3] * mix
    fx = (_rms_rows(x1, nfg_ref[...]) * (1.0 + mod[4:5]) + mod[3:4]).astype(BF)
    fx_ref[0] = fx
    scores = _sigmoid(_dot_nt(wr_ref[...], fx))
    gates, sel = _route(scores, rb_ref[...])
    rank = _dot(sel.astype(BF), tri_ref[...])
    gt_ref[0] = gates
    pt_ref[0] = jnp.where(sel > 0.5, rank, -1.0)
    cnt_ref[0] = jnp.sum(sel, axis=1, keepdims=True).astype(jnp.int32)
    sg = _dot(fx, wsg_ref[...])
    su = _dot(fx, wsu_ref[...])
    hs = (sg * _sigmoid(sg)) * su
    y0_ref[0] = x1 + mod[5:6] * _dot(hs.astype(BF), wsd_ref[...])


def _outproj_call(x, attn, gm, mod, w_out, nfg, wr, rb, wsg, wsu, wsd, tri, tm):
    b, l, d = x.shape
    nt = l // tm
    full = lambda shape: pl.BlockSpec(shape, lambda bi, i: (0,) * len(shape))
    tokd = pl.BlockSpec((1, tm, d), lambda bi, i: (bi, i, 0))
    tokh = pl.BlockSpec((1, tm, DA_WIDTH), lambda bi, i: (bi, i, 0))
    expt = pl.BlockSpec((1, N_EXPERTS, tm), lambda bi, i: (bi, 0, i))
    ds = wsg.shape[1]
    return pl.pallas_call(
        _outproj_kernel,
        out_shape=(jax.ShapeDtypeStruct((b, l, d), F32),
                   jax.ShapeDtypeStruct((b, l, d), BF),
                   jax.ShapeDtypeStruct((b, N_EXPERTS, l), F32),
                   jax.ShapeDtypeStruct((b, N_EXPERTS, l), F32),
                   jax.ShapeDtypeStruct((b * nt, N_EXPERTS, 1), jnp.int32)),
        grid=(b, nt),
        in_specs=[tokd, tokh, tokh,
                  pl.BlockSpec((1, 6, d), lambda bi, i: (bi, 0, 0)),
                  full((d, d)), full((1, d)), full((N_EXPERTS, d)), full((N_EXPERTS, 1)),
                  full((d, ds)), full((d, ds)), full((ds, d)), full((tm, tm))],
        out_specs=(tokd, tokd, expt, expt,
                   pl.BlockSpec((1, N_EXPERTS, 1), lambda bi, i: (bi * nt + i, 0, 0))),
        compiler_params=pltpu.CompilerParams(dimension_semantics=("arbitrary", "arbitrary"),
                                             vmem_limit_bytes=VMEM_LIMIT),
        name="outproj",
    )(x, attn, gm, mod, w_out, nfg, wr, rb, wsg, wsu, wsd, tri)


def _moe_kernel(cnt_ref, order_ref, fx_ref, gt_ref, pt_ref, *refs, sub, caps, epg):
    wg_ref, wu_ref, wd_ref = refs[0:epg], refs[epg:2 * epg], refs[2 * epg:3 * epg]
    y0_ref, mod_ref, o_ref = refs[3 * epg:]
    tt = fx_ref.shape[1]
    nsub = tt // sub
    pair = pl.program_id(2)
    experts = [order_ref[pl.program_id(0) * N_EXPERTS + pair * epg + k] for k in range(epg)]
    sub0 = (pl.program_id(0) * pl.num_programs(1) + pl.program_id(1)) * nsub

    @pl.when(pair == 0)
    def _zero():
        o_ref[...] = jnp.zeros_like(o_ref)

    cmax = jnp.int32(0)
    for s in range(nsub):
        for k in range(epg):
            cmax = jnp.maximum(cmax, cnt_ref[(sub0 + s) * N_EXPERTS + experts[k]])

    def do_round(r, cap):
        slot = lax.broadcasted_iota(jnp.int32, (cap, sub), 0).astype(F32) + (r * cap).astype(F32)
        picks, xrows, grows = [], [], []
        for s in range(nsub):
            cols = slice(s * sub, (s + 1) * sub)
            hit = [pt_ref[0, pl.ds(experts[k], 1), cols] == slot for k in range(epg)]
            grows.append([jnp.sum(jnp.where(hit[k], gt_ref[0, pl.ds(experts[k], 1), cols], 0.0),
                                  axis=-1, keepdims=True) for k in range(epg)])
            pick = jnp.concatenate([jnp.where(h, 1.0, 0.0) for h in hit], axis=0).astype(BF)
            picks.append(pick)
            xrows.append(_dot(pick, fx_ref[0, cols, :]).astype(BF))
        outs = []
        for k in range(epg):
            xk = jnp.concatenate([xrows[s][k * cap:(k + 1) * cap] for s in range(nsub)], axis=0)
            gk = jnp.concatenate([grows[s][k] for s in range(nsub)], axis=0)
            a = _dot(xk, wg_ref[k][0])
            bb = _dot(xk, wu_ref[k][0])
            hm = (a * _sigmoid(a)) * bb * gk
            outs.append(_dot(hm.astype(BF), wd_ref[k][0]).astype(BF))
        for s in range(nsub):
            cols = slice(s * sub, (s + 1) * sub)
            stacked = jnp.concatenate([outs[k][s * cap:(s + 1) * cap] for k in range(epg)], axis=0)
            o_ref[0, cols, :] += lax.dot_general(picks[s], stacked, (((0,), (0,)), ((), ())),
                                                 preferred_element_type=F32)

    below = 0
    for cap in caps:
        fits = (cmax <= cap) if below == 0 else ((cmax > below) & (cmax <= cap))
        pl.when(fits)(functools.partial(do_round, jnp.int32(0), cap))
        below = cap

    @pl.when(cmax > caps[-1])
    def _many_rounds():
        def body(r, carry):
            do_round(r, caps[-1])
            return carry
        lax.fori_loop(0, (cmax + (caps[-1] - 1)) // caps[-1], body, 0)

    @pl.when(pair == pl.num_programs(2) - 1)
    def _finish():
        o_ref[0] = y0_ref[0] + mod_ref[0][5:6] * o_ref[0]


def _moe_call(counts, order, fx, gt, pt, wg, wu, wd, y0, mod, tt, sub, caps, epg):
    b, l, d = fx.shape
    ne, _, de = wg.shape
    assert l % tt == 0 and tt % sub == 0 and ne % epg == 0
    assert all(c % 16 == 0 for c in caps) and list(caps) == sorted(caps)
    tokd = pl.BlockSpec((1, tt, d), lambda bi, i, p, cnt, order: (bi, i, 0))
    expt = pl.BlockSpec((1, ne, tt), lambda bi, i, p, cnt, order: (bi, 0, i))

    def expert_block(shape, k):
        return pl.BlockSpec((1,) + shape,
                            lambda bi, i, p, cnt, order: (order[bi * ne + p * epg + k], 0, 0))

    kern = functools.partial(_moe_kernel, sub=sub, caps=tuple(caps), epg=epg)
    grid_spec = pltpu.PrefetchScalarGridSpec(
        num_scalar_prefetch=2,
        grid=(b, l // tt, ne // epg),
        in_specs=([tokd, expt, expt]
                  + [expert_block((d, de), k) for k in range(epg)]
                  + [expert_block((d, de), k) for k in range(epg)]
                  + [expert_block((de, d), k) for k in range(epg)]
                  + [tokd, pl.BlockSpec((1, 6, d), lambda bi, i, p, cnt, order: (bi, 0, 0))]),
        out_specs=tokd)
    return pl.pallas_call(
        kern,
        out_shape=jax.ShapeDtypeStruct((b, l, d), F32),
        grid_spec=grid_spec,
        compiler_params=pltpu.CompilerParams(
            dimension_semantics=("arbitrary", "arbitrary", "arbitrary"),
            vmem_limit_bytes=VMEM_LIMIT),
        name="moe",
    )(counts, order, fx, gt, pt, *([wg] * epg), *([wu] * epg), *([wd] * epg), y0, mod)


def _rope_tables(n_tokens):
    rows = n_tokens // GRID_W
    row = jnp.repeat(jnp.arange(rows, dtype=F32), GRID_W)
    col = jnp.tile(jnp.arange(GRID_W, dtype=F32), rows)
    half = DA_HEAD_DIM // 2
    inv_freq = ROPE_THETA ** (-jnp.arange(0, half, 2, dtype=F32) / half)
    ang = jnp.concatenate([row[:, None] * inv_freq, col[:, None] * inv_freq], axis=-1)
    cos, sin = jnp.cos(ang), jnp.sin(ang)
    cos64 = jnp.repeat(cos, 2, axis=-1)
    sin64 = jnp.stack([-sin, sin], axis=-1).reshape(n_tokens, DA_HEAD_DIM)
    return jnp.tile(cos64, (1, 2)), jnp.tile(sin64, (1, 2))


def kernel(x, c, ctx, c_ctx, w_ada, b_ada, norm_mix_g, w_in, q_norm_g, k_norm_g, da_lambda, subln_g, gm_ln_g, gm_ln_b, gm_ws, gm_bs, gm_out_g, w_out, norm_ffn_g, w_router, router_bias, we_gate, we_up, we_down, ws_gate, ws_up, ws_down):
    assert w_ada.shape[0] == 1, "single-layer kernel"
    b, l, d = x.shape
    lambda_init = 0.8 - 0.6 * math.exp(-0.3 * 0)
    lp = da_lambda[0].astype(F32)
    lam = (jnp.exp(jnp.sum(lp[0] * lp[1])) - jnp.exp(jnp.sum(lp[2] * lp[3])) + lambda_init).reshape(1)

    cond_rows = 16
    cond = jnp.zeros((cond_rows, d), F32).at[:b].set(c).at[b].set(c_ctx)
    ada = _ada_call(cond, w_ada[0], b_ada[0][None, :]).reshape(cond_rows, 6, d)
    mod = ada[:b]
    mod_ctx = ada[b:b + 1]

    cos, sin = _rope_tables(l)
    half = jnp.arange(LANES) // DA_HEAD_DIM
    gmat = (half[:, None] == half[None, :]).astype(BF)
    qg = jnp.tile(q_norm_g[0], 2)[None, :]
    kg = jnp.tile(k_norm_g[0], 2)[None, :]
    w_in_bf = w_in[0].astype(BF)
    bs_full = jnp.broadcast_to(gm_bs[0][:, :, None], (GM_HEADS, CHUNK, GM_HEAD_DIM))

    tm = min(ROUTE_TILE, l)
    q, k, v, gm = _inproj_call(
        x, mod, norm_mix_g, w_in_bf, qg, kg, cos, sin, gmat,
        gm_ln_g[0].reshape(1, GM_WIDTH), gm_ln_b[0].reshape(1, GM_WIDTH),
        gm_ws[0].astype(BF), bs_full, gm_out_g[0].reshape(1, GM_WIDTH), min(INPROJ_TILE, l))
    kc, vc = _ctxproj_call(ctx, mod_ctx, norm_mix_g, w_in_bf[:, DA_WIDTH:3 * DA_WIDTH], kg, gmat)

    attn = _attn_call(lam, q, k, kc, v, vc, subln_g[0][:, None],
                      tq=min(ATTN_Q_TILE, l), ck=min(ATTN_KEY_CHUNK, ctx.shape[1]),
                      out_scale=1.0 - lambda_init)

    wr = w_router[0].T.astype(BF)
    rb = router_bias[0][:, None]
    tok = jnp.arange(tm)
    tri = (tok[:, None] < tok[None, :]).astype(BF)
    y0, fx, gt, pt, counts = _outproj_call(
        x, attn, gm, mod, w_out[0].astype(BF), norm_ffn_g, wr, rb,
        ws_gate[0].astype(BF), ws_up[0].astype(BF), ws_down[0].astype(BF), tri, tm)

    totals = jnp.sum(counts.reshape(b, -1, N_EXPERTS), axis=1)
    order = jnp.argsort(-totals, axis=1).astype(jnp.int32).reshape(-1)
    return _moe_call(counts.reshape(-1), order, fx, gt, pt,
                     we_gate[0].astype(BF), we_up[0].astype(BF), we_down[0].astype(BF),
                     y0, mod, tt=min(MOE_TILE, l), sub=tm, caps=MOE_CAPS, epg=MOE_EXPERTS_PER_STEP)
```

```python
import functools
import math

import jax
import jax.numpy as jnp
from jax import lax
from jax.experimental import pallas as pl
from jax.experimental.pallas import tpu as pltpu

BF = jnp.bfloat16
F32 = jnp.float32

EPS = 1e-6
GRID_W = 64
DA_HEADS = 4
DA_HEAD_DIM = 64
DA_V_DIM = 128
DA_WIDTH = 512
GM_HEADS = 4
GM_HEAD_DIM = 128
GM_WIDTH = 512
CHUNK = 128
ROPE_THETA = 10000.0
Q_SCALE = DA_HEAD_DIM ** -0.5 * math.log2(math.e)
N_EXPERTS = 32
TOP_K = 4
N_GROUPS = 4
TOPK_GROUPS = 2
GROUP_SIZE = N_EXPERTS // N_GROUPS
ROUTED_SCALE = 2.5
ATTN_Q_TILE = 256
ATTN_KEY_CHUNK = 256
INPROJ_TILE = 512
ROUTE_TILE = 512
MOE_TILE = 2048
MOE_CAPS = (32, 64, 96, 128, 192, 256)
MOE_EXPERTS_PER_STEP = 2
LANES = 128
VMEM_LIMIT = 56 * 1024 * 1024


def _sigmoid(x):
    return 1.0 / (1.0 + jnp.exp(-x))


def _dot(a, b):
    return jnp.dot(a, b, preferred_element_type=F32)


def _dot_nt(a, b):
    return lax.dot_general(a, b, (((1,), (1,)), ((), ())), preferred_element_type=F32)


def _rms_rows(x, g):
    ms = jnp.mean(x * x, axis=-1, keepdims=True)
    return x * lax.rsqrt(ms + EPS) * g


def _group_mean_sq(y, gmat):
    y2 = y * y
    hi = y2.astype(BF)
    lo = (y2 - hi.astype(F32)).astype(BF)
    return (_dot(hi, gmat) + _dot(lo, gmat)) * (1.0 / DA_HEAD_DIM)


def _swap_pairs(y):
    lane = lax.broadcasted_iota(jnp.int32, y.shape, 1)
    nxt = pltpu.roll(y, LANES - 1, 1)
    prv = pltpu.roll(y, 1, 1)
    return jnp.where((lane & 1) == 0, nxt, prv)


def _ada_kernel(cond_ref, w_ref, b_ref, o_ref):
    c = cond_ref[...]
    s = c * _sigmoid(c)
    o_ref[...] = _dot(s.astype(BF), w_ref[...].astype(BF)) + b_ref[...]


def _ada_call(cond, w_ada, b_ada):
    rows, d = cond.shape
    n = w_ada.shape[1]
    bn = 1024
    return pl.pallas_call(
        _ada_kernel,
        out_shape=jax.ShapeDtypeStruct((rows, n), F32),
        grid=(n // bn,),
        in_specs=[pl.BlockSpec((rows, d), lambda j: (0, 0)),
                  pl.BlockSpec((d, bn), lambda j: (0, j)),
                  pl.BlockSpec((1, bn), lambda j: (0, j))],
        out_specs=pl.BlockSpec((rows, bn), lambda j: (0, j)),
        compiler_params=pltpu.CompilerParams(dimension_semantics=("arbitrary",),
                                             vmem_limit_bytes=VMEM_LIMIT),
        name="ada",
    )(cond, w_ada, b_ada)


def _inproj_kernel(x_ref, mod_ref, ng_ref, w_ref, qg_ref, kg_ref, cos_ref, sin_ref, gmat_ref,
                   lng_ref, lnb_ref, ws_ref, bs_ref, og_ref,
                   q_ref, k_ref, v_ref, gm_ref):
    tm = x_ref.shape[1]
    x = x_ref[0]
    mod = mod_ref[0]
    h = (_rms_rows(x, ng_ref[...]) * (1.0 + mod[1:2]) + mod[0:1]).astype(BF)

    def proj(col0, width=2 * LANES):
        return _dot(h, w_ref[:, col0:col0 + width])

    def gelu(t):
        return 0.5 * t * (1.0 + lax.erf(t * math.sqrt(0.5)))

    gmat = gmat_ref[...]
    cos = cos_ref[...]
    sin = sin_ref[...]
    for jb in range(DA_HEADS // 2):
        pq = proj(jb * 2 * LANES)
        pk = proj(DA_WIDTH + jb * 2 * LANES)
        for jj in range(2):
            sl = slice((2 * jb + jj) * LANES, (2 * jb + jj + 1) * LANES)
            qj = pq[:, jj * LANES:(jj + 1) * LANES]
            qn = qj * lax.rsqrt(_group_mean_sq(qj, gmat) + EPS) * qg_ref[...]
            qr = qn * cos + _swap_pairs(qn) * sin
            q_ref[0, :, sl] = (qr * Q_SCALE).astype(BF)
            kj = pk[:, jj * LANES:(jj + 1) * LANES]
            kn = kj * lax.rsqrt(_group_mean_sq(kj, gmat) + EPS) * kg_ref[...]
            kr = kn * cos + _swap_pairs(kn) * sin
            k_ref[0, :, sl] = kr.astype(BF)
    v_ref[0] = proj(2 * DA_WIDTH, DA_WIDTH).astype(BF)

    for g in range(GM_HEADS):
        sl = slice(g * LANES, (g + 1) * LANES)
        if g % 2 == 0:
            zu = gelu(proj(3 * DA_WIDTH + g * LANES))
            zv = gelu(proj(3 * DA_WIDTH + GM_WIDTH + g * LANES))
        u = zu[:, (g % 2) * LANES:(g % 2 + 1) * LANES]
        vg = zv[:, (g % 2) * LANES:(g % 2 + 1) * LANES]
        mu = jnp.mean(vg, axis=-1, keepdims=True)
        xc = vg - mu
        var = jnp.mean(xc * xc, axis=-1, keepdims=True)
        vn = (xc * lax.rsqrt(var + EPS) * lng_ref[:, sl] + lnb_ref[:, sl]).astype(BF)
        for cidx in range(tm // CHUNK):
            rows = slice(cidx * CHUNK, (cidx + 1) * CHUNK)
            mixed = _dot(ws_ref[g], vn[rows]) + bs_ref[g]
            y = u[rows] * mixed
            gm_ref[0, rows, sl] = _rms_rows(y, og_ref[:, sl]).astype(BF)


def _inproj_call(x, mod, ng, w_in, qg, kg, cos, sin, gmat, lng, lnb, ws, bs, og, tm):
    b, l, d = x.shape
    nw = w_in.shape[1]
    full = lambda shape: pl.BlockSpec(shape, lambda bi, i: (0,) * len(shape))
    tok = pl.BlockSpec((1, tm, DA_WIDTH), lambda bi, i: (bi, i, 0))
    out = jax.ShapeDtypeStruct((b, l, DA_WIDTH), BF)
    return pl.pallas_call(
        _inproj_kernel,
        out_shape=(out, out, out, out),
        grid=(b, l // tm),
        in_specs=[pl.BlockSpec((1, tm, d), lambda bi, i: (bi, i, 0)),
                  pl.BlockSpec((1, 6, d), lambda bi, i: (bi, 0, 0)),
                  full((1, d)), full((d, nw)), full((1, LANES)), full((1, LANES)),
                  pl.BlockSpec((tm, LANES), lambda bi, i: (i, 0)),
                  pl.BlockSpec((tm, LANES), lambda bi, i: (i, 0)),
                  full((LANES, LANES)), full((1, GM_WIDTH)), full((1, GM_WIDTH)),
                  full((GM_HEADS, CHUNK, CHUNK)), full((GM_HEADS, CHUNK, GM_HEAD_DIM)),
                  full((1, GM_WIDTH))],
        out_specs=(tok, tok, tok, tok),
        compiler_params=pltpu.CompilerParams(dimension_semantics=("arbitrary", "arbitrary"),
                                             vmem_limit_bytes=VMEM_LIMIT),
        name="inproj",
    )(x, mod, ng, w_in, qg, kg, cos, sin, gmat, lng, lnb, ws, bs, og)


def _ctxproj_kernel(x_ref, mod_ref, ng_ref, w_ref, kg_ref, gmat_ref, k_ref, v_ref):
    x = x_ref[0]
    mod = mod_ref[0]
    h = _rms_rows(x, ng_ref[...]) * (1.0 + mod[1:2]) + mod[0:1]
    p = _dot(h.astype(BF), w_ref[...])
    gmat = gmat_ref[...]
    for j in range(DA_HEADS):
        sl = slice(j * LANES, (j + 1) * LANES)
        kj = p[:, sl]
        kn = kj * lax.rsqrt(_group_mean_sq(kj, gmat) + EPS) * kg_ref[...]
        k_ref[0, :, sl] = kn.astype(BF)
    v_ref[0] = p[:, DA_WIDTH:].astype(BF)


def _ctxproj_call(ctx, mod_ctx, ng, w_kv, kg, gmat):
    b, lc, d = ctx.shape
    full = lambda shape: pl.BlockSpec(shape, lambda bi: (0,) * len(shape))
    tok = pl.BlockSpec((1, lc, DA_WIDTH), lambda bi: (bi, 0, 0))
    out = jax.ShapeDtypeStruct((b, lc, DA_WIDTH), BF)
    return pl.pallas_call(
        _ctxproj_kernel,
        out_shape=(out, out),
        grid=(b,),
        in_specs=[pl.BlockSpec((1, lc, d), lambda bi: (bi, 0, 0)),
                  full((1, 6, d)), full((1, d)), full((d, 2 * DA_WIDTH)),
                  full((1, LANES)), full((LANES, LANES))],
        out_specs=(tok, tok),
        compiler_params=pltpu.CompilerParams(dimension_semantics=("arbitrary",),
                                             vmem_limit_bytes=VMEM_LIMIT),
        name="ctxproj",
    )(ctx, mod_ctx, ng, w_kv, kg, gmat)


def _attn_kernel(lam_ref, q_ref, kx_ref, kc_ref, vx_ref, vc_ref, sg_ref, o_ref,
                 k_scr, vt_scr, s_scr, m_scr, *, ck, out_scale):
    tq = q_ref.shape[1]
    lc = kc_ref.shape[1]
    lx = kx_ref.shape[1]
    nck = (lc + lx) // ck

    step = pl.program_id(2)
    last = pl.num_programs(2) - 1

    def stacked_queries():
        q = q_ref[0]
        lane = lax.broadcasted_iota(jnp.int32, q.shape, 1)
        zero = jnp.zeros_like(q)
        return jnp.concatenate([jnp.where(lane < DA_HEAD_DIM, q, zero),
                                jnp.where(lane >= DA_HEAD_DIM, q, zero)], axis=0)

    def score_chunk(qq, cidx, mrun):
        rows = slice(cidx * ck, (cidx + 1) * ck)
        st = _dot_nt(k_scr[rows, :], qq)
        s_scr[rows, :] = st
        return jnp.maximum(mrun, jnp.max(st.reshape(ck // 8, 8, 2 * tq), axis=0))

    def weight_chunk(m, cidx, acc, lrun):
        rows = slice(cidx * ck, (cidx + 1) * ck)
        pt = jnp.exp2(s_scr[rows, :] - m)
        acc = acc + _dot(vt_scr[cidx], pt.astype(BF))
        return acc, lrun + jnp.sum(pt.reshape(ck // 8, 8, 2 * tq), axis=0)

    def finish(acc, lrun):
        r = 1.0 / jnp.sum(lrun, axis=0, keepdims=True)
        ot = acc[:, :tq] * r[:, :tq] - lam_ref[0] * (acc[:, tq:] * r[:, tq:])
        ms = jnp.mean(ot * ot, axis=0, keepdims=True)
        on = ot * lax.rsqrt(ms + EPS) * sg_ref[...] * out_scale
        o_ref[0] = on.T.astype(BF)

    mrun0 = jnp.full((8, 2 * tq), -jnp.inf, F32)
    lrun0 = jnp.zeros((8, 2 * tq), F32)
    acc0 = jnp.zeros((DA_V_DIM, 2 * tq), F32)

    @pl.when(step == 0)
    def _first():
        k_scr[0:lc, :] = kc_ref[0]
        k_scr[lc:lc + lx, :] = kx_ref[0]
        for cidx in range(nck):
            lo = cidx * ck
            if lo < lc:
                blk = vc_ref[0, lo:lo + ck, :]
            else:
                blk = vx_ref[0, lo - lc:lo - lc + ck, :]
            vt_scr[cidx] = blk.astype(F32).T.astype(BF)
        qq = stacked_queries()
        mrun = mrun0
        for cidx in range(nck):
            mrun = score_chunk(qq, cidx, mrun)
        m_scr[...] = mrun

    @pl.when((step > 0) & (step < last))
    def _steady():
        qq = stacked_queries()
        m = jnp.max(m_scr[...], axis=0, keepdims=True)
        mrun, lrun, acc = mrun0, lrun0, acc0
        for cidx in range(nck):
            acc, lrun = weight_chunk(m, cidx, acc, lrun)
            mrun = score_chunk(qq, cidx, mrun)
        finish(acc, lrun)
        m_scr[...] = mrun

    @pl.when(step == last)
    def _last():
        m = jnp.max(m_scr[...], axis=0, keepdims=True)
        lrun, acc = lrun0, acc0
        for cidx in range(nck):
            acc, lrun = weight_chunk(m, cidx, acc, lrun)
        finish(acc, lrun)


def _attn_call(lam, q, kx, kc, vx, vc, subln_col, tq, ck, out_scale):
    b, l, _ = q.shape
    lc = kc.shape[1]
    lk = lc + l
    assert lk % ck == 0 and lc % ck == 0 and l % tq == 0
    nq = l // tq
    kern = functools.partial(_attn_kernel, ck=ck, out_scale=out_scale)
    return pl.pallas_call(
        kern,
        out_shape=jax.ShapeDtypeStruct((b, l, DA_WIDTH), BF),
        grid=(b, DA_HEADS, nq + 1),
        in_specs=[pl.BlockSpec(memory_space=pltpu.SMEM),
                  pl.BlockSpec((1, tq, LANES), lambda bi, h, i: (bi, jnp.minimum(i, nq - 1), h)),
                  pl.BlockSpec((1, l, LANES), lambda bi, h, i: (bi, 0, h)),
                  pl.BlockSpec((1, lc, LANES), lambda bi, h, i: (bi, 0, h)),
                  pl.BlockSpec((1, l, LANES), lambda bi, h, i: (bi, 0, h)),
                  pl.BlockSpec((1, lc, LANES), lambda bi, h, i: (bi, 0, h)),
                  pl.BlockSpec((DA_V_DIM, 1), lambda bi, h, i: (0, 0))],
        out_specs=pl.BlockSpec((1, tq, LANES), lambda bi, h, i: (bi, jnp.maximum(i - 1, 0), h)),
        scratch_shapes=[pltpu.VMEM((lk, LANES), BF),
                        pltpu.VMEM((lk // ck, DA_V_DIM, ck), BF),
                        pltpu.VMEM((lk, 2 * tq), F32),
                        pltpu.VMEM((8, 2 * tq), F32)],
        compiler_params=pltpu.CompilerParams(
            dimension_semantics=("arbitrary", "arbitrary", "arbitrary"),
            vmem_limit_bytes=VMEM_LIMIT),
        name="attn",
    )(lam, q, kx, kc, vx, vc, subln_col)


def _route(scores, rbias):
    tm = scores.shape[1]
    assert GROUP_SIZE == 8 and scores.shape[0] == N_EXPERTS
    row = lax.broadcasted_iota(jnp.int32, (GROUP_SIZE, tm), 0).astype(F32)
    neg = jnp.full((GROUP_SIZE, tm), -jnp.inf, F32)
    biased = scores + rbias
    groups = [slice(g * GROUP_SIZE, (g + 1) * GROUP_SIZE) for g in range(N_GROUPS)]
    vals = [biased[sl] for sl in groups]

    gscore = []
    for v in vals:
        m1 = jnp.max(v, axis=0, keepdims=True)
        i1 = jnp.min(jnp.where(v == m1, row, 1e9), axis=0, keepdims=True)
        m2 = jnp.max(jnp.where(row == i1, neg, v), axis=0, keepdims=True)
        gscore.append(m1 + m2)

    cur = []
    for g in range(N_GROUPS):
        beaten = jnp.zeros_like(gscore[g])
        for g2 in range(N_GROUPS):
            if g2 == g:
                continue
            beat = (gscore[g2] >= gscore[g]) if g2 < g else (gscore[g2] > gscore[g])
            beaten = beaten + jnp.where(beat, 1.0, 0.0)
        cur.append(jnp.where(beaten < TOPK_GROUPS, vals[g], neg))

    ids = [row + float(g * GROUP_SIZE) for g in range(N_GROUPS)]
    sel = [jnp.zeros((GROUP_SIZE, tm), F32) for _ in range(N_GROUPS)]
    for _ in range(TOP_K):
        best = functools.reduce(jnp.maximum, cur)
        mx = jnp.max(best, axis=0, keepdims=True)
        cand = functools.reduce(jnp.minimum, [jnp.where(c == mx, i, 1e9) for c, i in zip(cur, ids)])
        idx = jnp.min(cand, axis=0, keepdims=True)
        hits = [i == idx for i in ids]
        sel = [jnp.where(h, 1.0, s) for h, s in zip(hits, sel)]
        cur = [jnp.where(h, neg, c) for h, c in zip(hits, cur)]
    w = [s * scores[sl] for s, sl in zip(sel, groups)]
    total = jnp.sum(functools.reduce(lambda a, b: a + b, w), axis=0, keepdims=True)
    gates = [wg / total * ROUTED_SCALE for wg in w]
    return jnp.concatenate(gates, axis=0), jnp.concatenate(sel, axis=0)


def _outproj_kernel(x_ref, at_ref, gm_ref, mod_ref, wo_ref, nfg_ref, wr_ref, rb_ref,
                    wsg_ref, wsu_ref, wsd_ref, tri_ref,
                    y0_ref, fx_ref, gt_ref, pt_ref, cnt_ref):
    mod = mod_ref[0]
    mix = _dot(at_ref[0], wo_ref[0:DA_WIDTH, :]) + _dot(gm_ref[0], wo_ref[DA_WIDTH:, :])
    x1 = x_ref[0] + mod[2:3] * mix
    fx = (_rms_rows(x1, nfg_ref[...]) * (1.0 + mod[4:5]) + mod[3:4]).astype(BF)
    fx_ref[0] = fx
    scores = _sigmoid(_dot_nt(wr_ref[...], fx))
    gates, sel = _route(scores, rb_ref[...])
    rank = _dot(sel.astype(BF), tri_ref[...])
    gt_ref[0] = gates
    pt_ref[0] = jnp.where(sel > 0.5, rank, -1.0)
    cnt_ref[0] = jnp.sum(sel, axis=1, keepdims=True).astype(jnp.int32)
    sg = _dot(fx, wsg_ref[...])
    su = _dot(fx, wsu_ref[...])
    hs = (sg * _sigmoid(sg)) * su
    y0_ref[0] = x1 + mod[5:6] * _dot(hs.astype(BF), wsd_ref[...])


def _outproj_call(x, attn, gm, mod, w_out, nfg, wr, rb, wsg, wsu, wsd, tri, tm):
    b, l, d = x.shape
    nt = l // tm
    full = lambda shape: pl.BlockSpec(shape, lambda bi, i: (0,) * len(shape))
    tokd = pl.BlockSpec((1, tm, d), lambda bi, i: (bi, i, 0))
    tokh = pl.BlockSpec((1, tm, DA_WIDTH), lambda bi, i: (bi, i, 0))
    expt = pl.BlockSpec((1, N_EXPERTS, tm), lambda bi, i: (bi, 0, i))
    ds = wsg.shape[1]
    return pl.pallas_call(
        _outproj_kernel,
        out_shape=(jax.ShapeDtypeStruct((b, l, d), F32),
                   jax.ShapeDtypeStruct((b, l, d), BF),
                   jax.ShapeDtypeStruct((b, N_EXPERTS, l), F32),
                   jax.ShapeDtypeStruct((b, N_EXPERTS, l), F32),
                   jax.ShapeDtypeStruct((b * nt, N_EXPERTS, 1), jnp.int32)),
        grid=(b, nt),
        in_specs=[tokd, tokh, tokh,
                  pl.BlockSpec((1, 6, d), lambda bi, i: (bi, 0, 0)),
                  full((d, d)), full((1, d)), full((N_EXPERTS, d)), full((N_EXPERTS, 1)),
                  full((d, ds)), full((d, ds)), full((ds, d)), full((tm, tm))],
        out_specs=(tokd, tokd, expt, expt,
                   pl.BlockSpec((1, N_EXPERTS, 1), lambda bi, i: (bi * nt + i, 0, 0))),
        compiler_params=pltpu.CompilerParams(dimension_semantics=("arbitrary", "arbitrary"),
                                             vmem_limit_bytes=VMEM_LIMIT),
        name="outproj",
    )(x, attn, gm, mod, w_out, nfg, wr, rb, wsg, wsu, wsd, tri)


def _moe_kernel(cnt_ref, order_ref, fx_ref, gt_ref, pt_ref, *refs, sub, caps, epg):
    wg_ref, wu_ref, wd_ref = refs[0:epg], refs[epg:2 * epg], refs[2 * epg:3 * epg]
    y0_ref, mod_ref, o_ref = refs[3 * epg:]
    tt = fx_ref.shape[1]
    nsub = tt // sub
    pair = pl.program_id(2)
    experts = [order_ref[pl.program_id(0) * N_EXPERTS + pair * epg + k] for k in range(epg)]
    sub0 = (pl.program_id(0) * pl.num_programs(1) + pl.program_id(1)) * nsub

    @pl.when(pair == 0)
    def _zero():
        o_ref[...] = jnp.zeros_like(o_ref)

    cmax = jnp.int32(0)
    for s in range(nsub):
        for k in range(epg):
            cmax = jnp.maximum(cmax, cnt_ref[(sub0 + s) * N_EXPERTS + experts[k]])

    def do_round(r, cap):
        slot = lax.broadcasted_iota(jnp.int32, (cap, sub), 0).astype(F32) + (r * cap).astype(F32)
        picks, xrows, grows = [], [], []
        for s in range(nsub):
            cols = slice(s * sub, (s + 1) * sub)
            hit = [pt_ref[0, pl.ds(experts[k], 1), cols] == slot for k in range(epg)]
            grows.append([jnp.sum(jnp.where(hit[k], gt_ref[0, pl.ds(experts[k], 1), cols], 0.0),
                                  axis=-1, keepdims=True) for k in range(epg)])
            pick = jnp.concatenate([jnp.where(h, 1.0, 0.0) for h in hit], axis=0).astype(BF)
            picks.append(pick)
            xrows.append(_dot(pick, fx_ref[0, cols, :]).astype(BF))
        outs = []
        for k in range(epg):
            xk = jnp.concatenate([xrows[s][k * cap:(k + 1) * cap] for s in range(nsub)], axis=0)
            gk = jnp.concatenate([grows[s][k] for s in range(nsub)], axis=0)
            a = _dot(xk, wg_ref[k][0])
            bb = _dot(xk, wu_ref[k][0])
            hm = (a * _sigmoid(a)) * bb * gk
            outs.append(_dot(hm.astype(BF), wd_ref[k][0]).astype(BF))
        for s in range(nsub):
            cols = slice(s * sub, (s + 1) * sub)
            stacked = jnp.concatenate([outs[k][s * cap:(s + 1) * cap] for k in range(epg)], axis=0)
            o_ref[0, cols, :] += lax.dot_general(picks[s], stacked, (((0,), (0,)), ((), ())),
                                                 preferred_element_type=F32)

    below = 0
    for cap in caps:
        fits = (cmax <= cap) if below == 0 else ((cmax > below) & (cmax <= cap))
        pl.when(fits)(functools.partial(do_round, jnp.int32(0), cap))
        below = cap

    @pl.when(cmax > caps[-1])
    def _many_rounds():
        def body(r, carry):
            do_round(r, caps[-1])
            return carry
        lax.fori_loop(0, (cmax + (caps[-1] - 1)) // caps[-1], body, 0)

    @pl.when(pair == pl.num_programs(2) - 1)
    def _finish():
        o_ref[0] = y0_ref[0] + mod_ref[0][5:6] * o_ref[0]


def _moe_call(counts, order, fx, gt, pt, wg, wu, wd, y0, mod, tt, sub, caps, epg):
    b, l, d = fx.shape
    ne, _, de = wg.shape
    assert l % tt == 0 and tt % sub == 0 and ne % epg == 0
    assert all(c % 16 == 0 for c in caps) and list(caps) == sorted(caps)
    tokd = pl.BlockSpec((1, tt, d), lambda bi, i, p, cnt, order: (bi, i, 0))
    expt = pl.BlockSpec((1, ne, tt), lambda bi, i, p, cnt, order: (bi, 0, i))

    def expert_block(shape, k):
        return pl.BlockSpec((1,) + shape,
                            lambda bi, i, p, cnt, order: (order[bi * ne + p * epg + k], 0, 0))

    kern = functools.partial(_moe_kernel, sub=sub, caps=tuple(caps), epg=epg)
    grid_spec = pltpu.PrefetchScalarGridSpec(
        num_scalar_prefetch=2,
        grid=(b, l // tt, ne // epg),
        in_specs=([tokd, expt, expt]
                  + [expert_block((d, de), k) for k in range(epg)]
                  + [expert_block((d, de), k) for k in range(epg)]
                  + [expert_block((de, d), k) for k in range(epg)]
                  + [tokd, pl.BlockSpec((1, 6, d), lambda bi, i, p, cnt, order: (bi, 0, 0))]),
        out_specs=tokd)
    return pl.pallas_call(
        kern,
        out_shape=jax.ShapeDtypeStruct((b, l, d), F32),
        grid_spec=grid_spec,
        compiler_params=pltpu.CompilerParams(
            dimension_semantics=("arbitrary", "arbitrary", "arbitrary"),
            vmem_limit_bytes=VMEM_LIMIT),
        name="moe",
    )(counts, order, fx, gt, pt, *([wg] * epg), *([wu] * epg), *([wd] * epg), y0, mod)


def _rope_tables(n_tokens):
    rows = n_tokens // GRID_W
    row = jnp.repeat(jnp.arange(rows, dtype=F32), GRID_W)
    col = jnp.tile(jnp.arange(GRID_W, dtype=F32), rows)
    half = DA_HEAD_DIM // 2
    inv_freq = ROPE_THETA ** (-jnp.arange(0, half, 2, dtype=F32) / half)
    ang = jnp.concatenate([row[:, None] * inv_freq, col[:, None] * inv_freq], axis=-1)
    cos, sin = jnp.cos(ang), jnp.sin(ang)
    cos64 = jnp.repeat(cos, 2, axis=-1)
    sin64 = jnp.stack([-sin, sin], axis=-1).reshape(n_tokens, DA_HEAD_DIM)
    return jnp.tile(cos64, (1, 2)), jnp.tile(sin64, (1, 2))


def kernel(x, c, ctx, c_ctx, w_ada, b_ada, norm_mix_g, w_in, q_norm_g, k_norm_g, da_lambda, subln_g, gm_ln_g, gm_ln_b, gm_ws, gm_bs, gm_out_g, w_out, norm_ffn_g, w_router, router_bias, we_gate, we_up, we_down, ws_gate, ws_up, ws_down):
    assert w_ada.shape[0] == 1, "single-layer kernel"
    b, l, d = x.shape
    lambda_init = 0.8 - 0.6 * math.exp(-0.3 * 0)
    lp = da_lambda[0].astype(F32)
    lam = (jnp.exp(jnp.sum(lp[0] * lp[1])) - jnp.exp(jnp.sum(lp[2] * lp[3])) + lambda_init).reshape(1)

    cond_rows = 16
    cond = jnp.zeros((cond_rows, d), F32).at[:b].set(c).at[b].set(c_ctx)
    ada = _ada_call(cond, w_ada[0], b_ada[0][None, :]).reshape(cond_rows, 6, d)
    mod = ada[:b]
    mod_ctx = ada[b:b + 1]

    cos, sin = _rope_tables(l)
    half = jnp.arange(LANES) // DA_HEAD_DIM
    gmat = (half[:, None] == half[None, :]).astype(BF)
    qg = jnp.tile(q_norm_g[0], 2)[None, :]
    kg = jnp.tile(k_norm_g[0], 2)[None, :]
    w_in_bf = w_in[0].astype(BF)
    bs_full = jnp.broadcast_to(gm_bs[0][:, :, None], (GM_HEADS, CHUNK, GM_HEAD_DIM))

    tm = min(ROUTE_TILE, l)
    q, k, v, gm = _inproj_call(
        x, mod, norm_mix_g, w_in_bf, qg, kg, cos, sin, gmat,
        gm_ln_g[0].reshape(1, GM_WIDTH), gm_ln_b[0].reshape(1, GM_WIDTH),
        gm_ws[0].astype(BF), bs_full, gm_out_g[0].reshape(1, GM_WIDTH), min(INPROJ_TILE, l))
    kc, vc = _ctxproj_call(ctx, mod_ctx, norm_mix_g, w_in_bf[:, DA_WIDTH:3 * DA_WIDTH], kg, gmat)

    attn = _attn_call(lam, q, k, kc, v, vc, subln_g[0][:, None],
                      tq=min(ATTN_Q_TILE, l), ck=min(ATTN_KEY_CHUNK, ctx.shape[1]),
                      out_scale=1.0 - lambda_init)

    wr = w_router[0].T.astype(BF)
    rb = router_bias[0][:, None]
    tok = jnp.arange(tm)
    tri = (tok[:, None] < tok[None, :]).astype(BF)
    y0, fx, gt, pt, counts = _outproj_call(
        x, attn, gm, mod, w_out[0].astype(BF), norm_ffn_g, wr, rb,
        ws_gate[0].astype(BF), ws_up[0].astype(BF), ws_down[0].astype(BF), tri, tm)

    totals = jnp.sum(counts.reshape(b, -1, N_EXPERTS), axis=1)
    order = jnp.argsort(-totals, axis=1).astype(jnp.int32).reshape(-1)
    return _moe_call(counts.reshape(-1), order, fx, gt, pt,
                     we_gate[0].astype(BF), we_up[0].astype(BF), we_down[0].astype(BF),
                     y0, mod, tt=min(MOE_TILE, l), sub=tm, caps=MOE_CAPS, epg=MOE_EXPERTS_PER_STEP)
```

```python
import functools
import math

import jax
import jax.numpy as jnp
from jax import lax
from jax.experimental import pallas as pl
from jax.experimental.pallas import tpu as pltpu

BF = jnp.bfloat16
F32 = jnp.float32

EPS = 1e-6
GRID_W = 64
DA_HEADS = 4
DA_HEAD_DIM = 64
DA_V_DIM = 128
DA_WIDTH = 512
GM_HEADS = 4
GM_HEAD_DIM = 128
GM_WIDTH = 512
CHUNK = 128
ROPE_THETA = 10000.0
Q_SCALE = DA_HEAD_DIM ** -0.5 * math.log2(math.e)
N_EXPERTS = 32
TOP_K = 4
N_GROUPS = 4
TOPK_GROUPS = 2
GROUP_SIZE = N_EXPERTS // N_GROUPS
ROUTED_SCALE = 2.5
ATTN_Q_TILE = 256
ATTN_KEY_CHUNK = 256
INPROJ_TILE = 512
ROUTE_TILE = 512
MOE_TILE = 2048
MOE_CAPS = (48, 80, 128, 256)
MOE_EXPERTS_PER_STEP = 2
LANES = 128
VMEM_LIMIT = 56 * 1024 * 1024


def _sigmoid(x):
    return 1.0 / (1.0 + jnp.exp(-x))


def _dot(a, b):
    return jnp.dot(a, b, preferred_element_type=F32)


def _dot_nt(a, b):
    return lax.dot_general(a, b, (((1,), (1,)), ((), ())), preferred_element_type=F32)


def _rms_rows(x, g):
    ms = jnp.mean(x * x, axis=-1, keepdims=True)
    return x * lax.rsqrt(ms + EPS) * g


def _group_mean_sq(y, gmat):
    y2 = y * y
    hi = y2.astype(BF)
    lo = (y2 - hi.astype(F32)).astype(BF)
    return (_dot(hi, gmat) + _dot(lo, gmat)) * (1.0 / DA_HEAD_DIM)


def _swap_pairs(y):
    lane = lax.broadcasted_iota(jnp.int32, y.shape, 1)
    nxt = pltpu.roll(y, LANES - 1, 1)
    prv = pltpu.roll(y, 1, 1)
    return jnp.where((lane & 1) == 0, nxt, prv)


def _ada_kernel(cond_ref, w_ref, b_ref, o_ref):
    c = cond_ref[...]
    s = c * _sigmoid(c)
    o_ref[...] = _dot(s.astype(BF), w_ref[...].astype(BF)) + b_ref[...]


def _ada_call(cond, w_ada, b_ada):
    rows, d = cond.shape
    n = w_ada.shape[1]
    bn = 1024
    return pl.pallas_call(
        _ada_kernel,
        out_shape=jax.ShapeDtypeStruct((rows, n), F32),
        grid=(n // bn,),
        in_specs=[pl.BlockSpec((rows, d), lambda j: (0, 0)),
                  pl.BlockSpec((d, bn), lambda j: (0, j)),
                  pl.BlockSpec((1, bn), lambda j: (0, j))],
        out_specs=pl.BlockSpec((rows, bn), lambda j: (0, j)),
        compiler_params=pltpu.CompilerParams(dimension_semantics=("arbitrary",),
                                             vmem_limit_bytes=VMEM_LIMIT),
        name="ada",
    )(cond, w_ada, b_ada)


def _inproj_kernel(x_ref, mod_ref, ng_ref, w_ref, qg_ref, kg_ref, cos_ref, sin_ref, gmat_ref,
                   lng_ref, lnb_ref, ws_ref, bs_ref, og_ref,
                   q_ref, k_ref, v_ref, gm_ref):
    tm = x_ref.shape[1]
    x = x_ref[0]
    mod = mod_ref[0]
    h = (_rms_rows(x, ng_ref[...]) * (1.0 + mod[1:2]) + mod[0:1]).astype(BF)

    def proj(col0, width=2 * LANES):
        return _dot(h, w_ref[:, col0:col0 + width])

    def gelu(t):
        return 0.5 * t * (1.0 + lax.erf(t * math.sqrt(0.5)))

    gmat = gmat_ref[...]
    cos = cos_ref[...]
    sin = sin_ref[...]
    for jb in range(DA_HEADS // 2):
        pq = proj(jb * 2 * LANES)
        pk = proj(DA_WIDTH + jb * 2 * LANES)
        for jj in range(2):
            sl = slice((2 * jb + jj) * LANES, (2 * jb + jj + 1) * LANES)
            qj = pq[:, jj * LANES:(jj + 1) * LANES]
            qn = qj * lax.rsqrt(_group_mean_sq(qj, gmat) + EPS) * qg_ref[...]
            qr = qn * cos + _swap_pairs(qn) * sin
            q_ref[0, :, sl] = (qr * Q_SCALE).astype(BF)
            kj = pk[:, jj * LANES:(jj + 1) * LANES]
            kn = kj * lax.rsqrt(_group_mean_sq(kj, gmat) + EPS) * kg_ref[...]
            kr = kn * cos + _swap_pairs(kn) * sin
            k_ref[0, :, sl] = kr.astype(BF)
    v_ref[0] = proj(2 * DA_WIDTH, DA_WIDTH).astype(BF)

    for g in range(GM_HEADS):
        sl = slice(g * LANES, (g + 1) * LANES)
        if g % 2 == 0:
            zu = gelu(proj(3 * DA_WIDTH + g * LANES))
            zv = gelu(proj(3 * DA_WIDTH + GM_WIDTH + g * LANES))
        u = zu[:, (g % 2) * LANES:(g % 2 + 1) * LANES]
        vg = zv[:, (g % 2) * LANES:(g % 2 + 1) * LANES]
        mu = jnp.mean(vg, axis=-1, keepdims=True)
        xc = vg - mu
        var = jnp.mean(xc * xc, axis=-1, keepdims=True)
        vn = (xc * lax.rsqrt(var + EPS) * lng_ref[:, sl] + lnb_ref[:, sl]).astype(BF)
        for cidx in range(tm // CHUNK):
            rows = slice(cidx * CHUNK, (cidx + 1) * CHUNK)
            mixed = _dot(ws_ref[g], vn[rows]) + bs_ref[g]
            y = u[rows] * mixed
            gm_ref[0, rows, sl] = _rms_rows(y, og_ref[:, sl]).astype(BF)


def _inproj_call(x, mod, ng, w_in, qg, kg, cos, sin, gmat, lng, lnb, ws, bs, og, tm):
    b, l, d = x.shape
    nw = w_in.shape[1]
    full = lambda shape: pl.BlockSpec(shape, lambda bi, i: (0,) * len(shape))
    tok = pl.BlockSpec((1, tm, DA_WIDTH), lambda bi, i: (bi, i, 0))
    out = jax.ShapeDtypeStruct((b, l, DA_WIDTH), BF)
    return pl.pallas_call(
        _inproj_kernel,
        out_shape=(out, out, out, out),
        grid=(b, l // tm),
        in_specs=[pl.BlockSpec((1, tm, d), lambda bi, i: (bi, i, 0)),
                  pl.BlockSpec((1, 6, d), lambda bi, i: (bi, 0, 0)),
                  full((1, d)), full((d, nw)), full((1, LANES)), full((1, LANES)),
                  pl.BlockSpec((tm, LANES), lambda bi, i: (i, 0)),
                  pl.BlockSpec((tm, LANES), lambda bi, i: (i, 0)),
                  full((LANES, LANES)), full((1, GM_WIDTH)), full((1, GM_WIDTH)),
                  full((GM_HEADS, CHUNK, CHUNK)), full((GM_HEADS, CHUNK, GM_HEAD_DIM)),
                  full((1, GM_WIDTH))],
        out_specs=(tok, tok, tok, tok),
        compiler_params=pltpu.CompilerParams(dimension_semantics=("arbitrary", "arbitrary"),
                                             vmem_limit_bytes=VMEM_LIMIT),
        name="inproj",
    )(x, mod, ng, w_in, qg, kg, cos, sin, gmat, lng, lnb, ws, bs, og)


def _ctxproj_kernel(x_ref, mod_ref, ng_ref, w_ref, kg_ref, gmat_ref, k_ref, v_ref):
    x = x_ref[0]
    mod = mod_ref[0]
    h = _rms_rows(x, ng_ref[...]) * (1.0 + mod[1:2]) + mod[0:1]
    p = _dot(h.astype(BF), w_ref[...])
    gmat = gmat_ref[...]
    for j in range(DA_HEADS):
        sl = slice(j * LANES, (j + 1) * LANES)
        kj = p[:, sl]
        kn = kj * lax.rsqrt(_group_mean_sq(kj, gmat) + EPS) * kg_ref[...]
        k_ref[0, :, sl] = kn.astype(BF)
    v_ref[0] = p[:, DA_WIDTH:].astype(BF)


def _ctxproj_call(ctx, mod_ctx, ng, w_kv, kg, gmat):
    b, lc, d = ctx.shape
    full = lambda shape: pl.BlockSpec(shape, lambda bi: (0,) * len(shape))
    tok = pl.BlockSpec((1, lc, DA_WIDTH), lambda bi: (bi, 0, 0))
    out = jax.ShapeDtypeStruct((b, lc, DA_WIDTH), BF)
    return pl.pallas_call(
        _ctxproj_kernel,
        out_shape=(out, out),
        grid=(b,),
        in_specs=[pl.BlockSpec((1, lc, d), lambda bi: (bi, 0, 0)),
                  full((1, 6, d)), full((1, d)), full((d, 2 * DA_WIDTH)),
                  full((1, LANES)), full((LANES, LANES))],
        out_specs=(tok, tok),
        compiler_params=pltpu.CompilerParams(dimension_semantics=("arbitrary",),
                                             vmem_limit_bytes=VMEM_LIMIT),
        name="ctxproj",
    )(ctx, mod_ctx, ng, w_kv, kg, gmat)


def _attn_kernel(lam_ref, q_ref, kx_ref, kc_ref, vx_ref, vc_ref, sg_ref, o_ref,
                 k_scr, vt_scr, s_scr, m_scr, *, ck, out_scale):
    tq = q_ref.shape[1]
    lc = kc_ref.shape[1]
    lx = kx_ref.shape[1]
    nck = (lc + lx) // ck

    step = pl.program_id(2)
    last = pl.num_programs(2) - 1

    def stacked_queries():
        q = q_ref[0]
        lane = lax.broadcasted_iota(jnp.int32, q.shape, 1)
        zero = jnp.zeros_like(q)
        return jnp.concatenate([jnp.where(lane < DA_HEAD_DIM, q, zero),
                                jnp.where(lane >= DA_HEAD_DIM, q, zero)], axis=0)

    def score_chunk(qq, cidx, mrun):
        rows = slice(cidx * ck, (cidx + 1) * ck)
        st = _dot_nt(k_scr[rows, :], qq)
        s_scr[rows, :] = st
        return jnp.maximum(mrun, jnp.max(st.reshape(ck // 8, 8, 2 * tq), axis=0))

    def weight_chunk(m, cidx, acc, lrun):
        rows = slice(cidx * ck, (cidx + 1) * ck)
        pt = jnp.exp2(s_scr[rows, :] - m)
        acc = acc + _dot(vt_scr[cidx], pt.astype(BF))
        return acc, lrun + jnp.sum(pt.reshape(ck // 8, 8, 2 * tq), axis=0)

    def finish(acc, lrun):
        r = 1.0 / jnp.sum(lrun, axis=0, keepdims=True)
        ot = acc[:, :tq] * r[:, :tq] - lam_ref[0] * (acc[:, tq:] * r[:, tq:])
        ms = jnp.mean(ot * ot, axis=0, keepdims=True)
        on = ot * lax.rsqrt(ms + EPS) * sg_ref[...] * out_scale
        o_ref[0] = on.T.astype(BF)

    mrun0 = jnp.full((8, 2 * tq), -jnp.inf, F32)
    lrun0 = jnp.zeros((8, 2 * tq), F32)
    acc0 = jnp.zeros((DA_V_DIM, 2 * tq), F32)

    @pl.when(step == 0)
    def _first():
        k_scr[0:lc, :] = kc_ref[0]
        k_scr[lc:lc + lx, :] = kx_ref[0]
        for cidx in range(nck):
            lo = cidx * ck
            if lo < lc:
                blk = vc_ref[0, lo:lo + ck, :]
            else:
                blk = vx_ref[0, lo - lc:lo - lc + ck, :]
            vt_scr[cidx] = blk.astype(F32).T.astype(BF)
        qq = stacked_queries()
        mrun = mrun0
        for cidx in range(nck):
            mrun = score_chunk(qq, cidx, mrun)
        m_scr[...] = mrun

    @pl.when((step > 0) & (step < last))
    def _steady():
        qq = stacked_queries()
        m = jnp.max(m_scr[...], axis=0, keepdims=True)
        mrun, lrun, acc = mrun0, lrun0, acc0
        for cidx in range(nck):
            acc, lrun = weight_chunk(m, cidx, acc, lrun)
            mrun = score_chunk(qq, cidx, mrun)
        finish(acc, lrun)
        m_scr[...] = mrun

    @pl.when(step == last)
    def _last():
        m = jnp.max(m_scr[...], axis=0, keepdims=True)
        lrun, acc = lrun0, acc0
        for cidx in range(nck):
            acc, lrun = weight_chunk(m, cidx, acc, lrun)
        finish(acc, lrun)


def _attn_call(lam, q, kx, kc, vx, vc, subln_col, tq, ck, out_scale):
    b, l, _ = q.shape
    lc = kc.shape[1]
    lk = lc + l
    assert lk % ck == 0 and lc % ck == 0 and l % tq == 0
    nq = l // tq
    kern = functools.partial(_attn_kernel, ck=ck, out_scale=out_scale)
    return pl.pallas_call(
        kern,
        out_shape=jax.ShapeDtypeStruct((b, l, DA_WIDTH), BF),
        grid=(b, DA_HEADS, nq + 1),
        in_specs=[pl.BlockSpec(memory_space=pltpu.SMEM),
                  pl.BlockSpec((1, tq, LANES), lambda bi, h, i: (bi, jnp.minimum(i, nq - 1), h)),
                  pl.BlockSpec((1, l, LANES), lambda bi, h, i: (bi, 0, h)),
                  pl.BlockSpec((1, lc, LANES), lambda bi, h, i: (bi, 0, h)),
                  pl.BlockSpec((1, l, LANES), lambda bi, h, i: (bi, 0, h)),
                  pl.BlockSpec((1, lc, LANES), lambda bi, h, i: (bi, 0, h)),
                  pl.BlockSpec((DA_V_DIM, 1), lambda bi, h, i: (0, 0))],
        out_specs=pl.BlockSpec((1, tq, LANES), lambda bi, h, i: (bi, jnp.maximum(i - 1, 0), h)),
        scratch_shapes=[pltpu.VMEM((lk, LANES), BF),
                        pltpu.VMEM((lk // ck, DA_V_DIM, ck), BF),
                        pltpu.VMEM((lk, 2 * tq), F32),
                        pltpu.VMEM((8, 2 * tq), F32)],
        compiler_params=pltpu.CompilerParams(
            dimension_semantics=("arbitrary", "arbitrary", "arbitrary"),
            vmem_limit_bytes=VMEM_LIMIT),
        name="attn",
    )(lam, q, kx, kc, vx, vc, subln_col)


def _route(scores, rbias):
    tm = scores.shape[1]
    assert GROUP_SIZE == 8 and scores.shape[0] == N_EXPERTS
    row = lax.broadcasted_iota(jnp.int32, (GROUP_SIZE, tm), 0).astype(F32)
    neg = jnp.full((GROUP_SIZE, tm), -jnp.inf, F32)
    biased = scores + rbias
    groups = [slice(g * GROUP_SIZE, (g + 1) * GROUP_SIZE) for g in range(N_GROUPS)]
    vals = [biased[sl] for sl in groups]

    gscore = []
    for v in vals:
        m1 = jnp.max(v, axis=0, keepdims=True)
        i1 = jnp.min(jnp.where(v == m1, row, 1e9), axis=0, keepdims=True)
        m2 = jnp.max(jnp.where(row == i1, neg, v), axis=0, keepdims=True)
        gscore.append(m1 + m2)

    cur = []
    for g in range(N_GROUPS):
        beaten = jnp.zeros_like(gscore[g])
        for g2 in range(N_GROUPS):
            if g2 == g:
                continue
            beat = (gscore[g2] >= gscore[g]) if g2 < g else (gscore[g2] > gscore[g])
            beaten = beaten + jnp.where(beat, 1.0, 0.0)
        cur.append(jnp.where(beaten < TOPK_GROUPS, vals[g], neg))

    ids = [row + float(g * GROUP_SIZE) for g in range(N_GROUPS)]
    sel = [jnp.zeros((GROUP_SIZE, tm), F32) for _ in range(N_GROUPS)]
    for _ in range(TOP_K):
        best = functools.reduce(jnp.maximum, cur)
        mx = jnp.max(best, axis=0, keepdims=True)
        cand = functools.reduce(jnp.minimum, [jnp.where(c == mx, i, 1e9) for c, i in zip(cur, ids)])
        idx = jnp.min(cand, axis=0, keepdims=True)
        hits = [i == idx for i in ids]
        sel = [jnp.where(h, 1.0, s) for h, s in zip(hits, sel)]
        cur = [jnp.where(h, neg, c) for h, c in zip(hits, cur)]
    w = [s * scores[sl] for s, sl in zip(sel, groups)]
    total = jnp.sum(functools.reduce(lambda a, b: a + b, w), axis=0, keepdims=True)
    gates = [wg / total * ROUTED_SCALE for wg in w]
    return jnp.concatenate(gates, axis=0), jnp.concatenate(sel, axis=0)


def _outproj_kernel(x_ref, at_ref, gm_ref, mod_ref, wo_ref, nfg_ref, wr_ref, rb_ref,
                    wsg_ref, wsu_ref, wsd_ref, tri_ref,
                    y0_ref, fx_ref, gt_ref, pt_ref, cnt_ref):
    mod = mod_ref[0]
    mix = _dot(at_ref[0], wo_ref[0:DA_WIDTH, :]) + _dot(gm_ref[0], wo_ref[DA_WIDTH:, :])
    x1 = x_ref[0] + mod[2:3] * mix
    fx = (_rms_rows(x1, nfg_ref[...]) * (1.0 + mod[4:5]) + mod[3:4]).astype(BF)
    fx_ref[0] = fx
    scores = _sigmoid(_dot_nt(wr_ref[...], fx))
    gates, sel = _route(scores, rb_ref[...])
    rank = _dot(sel.astype(BF), tri_ref[...])
    gt_ref[0] = gates
    pt_ref[0] = jnp.where(sel > 0.5, rank, -1.0)
    cnt_ref[0] = jnp.sum(sel, axis=1, keepdims=True).astype(jnp.int32)
    sg = _dot(fx, wsg_ref[...])
    su = _dot(fx, wsu_ref[...])
    hs = (sg * _sigmoid(sg)) * su
    y0_ref[0] = x1 + mod[5:6] * _dot(hs.astype(BF), wsd_ref[...])


def _outproj_call(x, attn, gm, mod, w_out, nfg, wr, rb, wsg, wsu, wsd, tri, tm):
    b, l, d = x.shape
    nt = l // tm
    full = lambda shape: pl.BlockSpec(shape, lambda bi, i: (0,) * len(shape))
    tokd = pl.BlockSpec((1, tm, d), lambda bi, i: (bi, i, 0))
    tokh = pl.BlockSpec((1, tm, DA_WIDTH), lambda bi, i: (bi, i, 0))
    expt = pl.BlockSpec((1, N_EXPERTS, tm), lambda bi, i: (bi, 0, i))
    ds = wsg.shape[1]
    return pl.pallas_call(
        _outproj_kernel,
        out_shape=(jax.ShapeDtypeStruct((b, l, d), F32),
                   jax.ShapeDtypeStruct((b, l, d), BF),
                   jax.ShapeDtypeStruct((b, N_EXPERTS, l), F32),
                   jax.ShapeDtypeStruct((b, N_EXPERTS, l), F32),
                   jax.ShapeDtypeStruct((b * nt, N_EXPERTS, 1), jnp.int32)),
        grid=(b, nt),
        in_specs=[tokd, tokh, tokh,
                  pl.BlockSpec((1, 6, d), lambda bi, i: (bi, 0, 0)),
                  full((d, d)), full((1, d)), full((N_EXPERTS, d)), full((N_EXPERTS, 1)),
                  full((d, ds)), full((d, ds)), full((ds, d)), full((tm, tm))],
        out_specs=(tokd, tokd, expt, expt,
                   pl.BlockSpec((1, N_EXPERTS, 1), lambda bi, i: (bi * nt + i, 0, 0))),
        compiler_params=pltpu.CompilerParams(dimension_semantics=("arbitrary", "arbitrary"),
                                             vmem_limit_bytes=VMEM_LIMIT),
        name="outproj",
    )(x, attn, gm, mod, w_out, nfg, wr, rb, wsg, wsu, wsd, tri)


def _moe_kernel(cnt_ref, order_ref, fx_ref, gt_ref, pt_ref, *refs, sub, caps, epg):
    wg_ref, wu_ref, wd_ref = refs[0:epg], refs[epg:2 * epg], refs[2 * epg:3 * epg]
    y0_ref, mod_ref, o_ref = refs[3 * epg:]
    tt = fx_ref.shape[1]
    nsub = tt // sub
    pair = pl.program_id(2)
    experts = [order_ref[pl.program_id(0) * N_EXPERTS + pair * epg + k] for k in range(epg)]
    sub0 = (pl.program_id(0) * pl.num_programs(1) + pl.program_id(1)) * nsub

    @pl.when(pair == 0)
    def _zero():
        o_ref[...] = jnp.zeros_like(o_ref)

    cmax = jnp.int32(0)
    for s in range(nsub):
        for k in range(epg):
            cmax = jnp.maximum(cmax, cnt_ref[(sub0 + s) * N_EXPERTS + experts[k]])

    def do_round(r, cap):
        slot = lax.broadcasted_iota(jnp.int32, (cap, sub), 0).astype(F32) + (r * cap).astype(F32)
        picks, xrows, grows = [], [], []
        for s in range(nsub):
            cols = slice(s * sub, (s + 1) * sub)
            hit = [pt_ref[0, pl.ds(experts[k], 1), cols] == slot for k in range(epg)]
            grows.append([jnp.sum(jnp.where(hit[k], gt_ref[0, pl.ds(experts[k], 1), cols], 0.0),
                                  axis=-1, keepdims=True) for k in range(epg)])
            pick = jnp.concatenate([jnp.where(h, 1.0, 0.0) for h in hit], axis=0).astype(BF)
            picks.append(pick)
            xrows.append(_dot(pick, fx_ref[0, cols, :]).astype(BF))
        outs = []
        for k in range(epg):
            xk = jnp.concatenate([xrows[s][k * cap:(k + 1) * cap] for s in range(nsub)], axis=0)
            gk = jnp.concatenate([grows[s][k] for s in range(nsub)], axis=0)
            a = _dot(xk, wg_ref[k][0])
            bb = _dot(xk, wu_ref[k][0])
            hm = (a * _sigmoid(a)) * bb * gk
            outs.append(_dot(hm.astype(BF), wd_ref[k][0]).astype(BF))
        for s in range(nsub):
            cols = slice(s * sub, (s + 1) * sub)
            stacked = jnp.concatenate([outs[k][s * cap:(s + 1) * cap] for k in range(epg)], axis=0)
            o_ref[0, cols, :] += lax.dot_general(picks[s], stacked, (((0,), (0,)), ((), ())),
                                                 preferred_element_type=F32)

    below = 0
    for cap in caps:
        fits = (cmax <= cap) if below == 0 else ((cmax > below) & (cmax <= cap))
        pl.when(fits)(functools.partial(do_round, jnp.int32(0), cap))
        below = cap

    @pl.when(cmax > caps[-1])
    def _many_rounds():
        def body(r, carry):
            do_round(r, caps[-1])
            return carry
        lax.fori_loop(0, (cmax + (caps[-1] - 1)) // caps[-1], body, 0)

    @pl.when(pair == pl.num_programs(2) - 1)
    def _finish():
        o_ref[0] = y0_ref[0] + mod_ref[0][5:6] * o_ref[0]


def _moe_call(counts, order, fx, gt, pt, wg, wu, wd, y0, mod, tt, sub, caps, epg):
    b, l, d = fx.shape
    ne, _, de = wg.shape
    assert l % tt == 0 and tt % sub == 0 and ne % epg == 0
    assert all(c % 16 == 0 for c in caps) and list(caps) == sorted(caps)
    tokd = pl.BlockSpec((1, tt, d), lambda bi, i, p, cnt, order: (bi, i, 0))
    expt = pl.BlockSpec((1, ne, tt), lambda bi, i, p, cnt, order: (bi, 0, i))

    def expert_block(shape, k):
        return pl.BlockSpec((1,) + shape,
                            lambda bi, i, p, cnt, order: (order[bi * ne + p * epg + k], 0, 0))

    kern = functools.partial(_moe_kernel, sub=sub, caps=tuple(caps), epg=epg)
    grid_spec = pltpu.PrefetchScalarGridSpec(
        num_scalar_prefetch=2,
        grid=(b, l // tt, ne // epg),
        in_specs=([tokd, expt, expt]
                  + [expert_block((d, de), k) for k in range(epg)]
                  + [expert_block((d, de), k) for k in range(epg)]
                  + [expert_block((de, d), k) for k in range(epg)]
                  + [tokd, pl.BlockSpec((1, 6, d), lambda bi, i, p, cnt, order: (bi, 0, 0))]),
        out_specs=tokd)
    return pl.pallas_call(
        kern,
        out_shape=jax.ShapeDtypeStruct((b, l, d), F32),
        grid_spec=grid_spec,
        compiler_params=pltpu.CompilerParams(
            dimension_semantics=("arbitrary", "arbitrary", "arbitrary"),
            vmem_limit_bytes=VMEM_LIMIT),
        name="moe",
    )(counts, order, fx, gt, pt, *([wg] * epg), *([wu] * epg), *([wd] * epg), y0, mod)


def _rope_tables(n_tokens):
    rows = n_tokens // GRID_W
    row = jnp.repeat(jnp.arange(rows, dtype=F32), GRID_W)
    col = jnp.tile(jnp.arange(GRID_W, dtype=F32), rows)
    half = DA_HEAD_DIM // 2
    inv_freq = ROPE_THETA ** (-jnp.arange(0, half, 2, dtype=F32) / half)
    ang = jnp.concatenate([row[:, None] * inv_freq, col[:, None] * inv_freq], axis=-1)
    cos, sin = jnp.cos(ang), jnp.sin(ang)
    cos64 = jnp.repeat(cos, 2, axis=-1)
    sin64 = jnp.stack([-sin, sin], axis=-1).reshape(n_tokens, DA_HEAD_DIM)
    return jnp.tile(cos64, (1, 2)), jnp.tile(sin64, (1, 2))


def kernel(x, c, ctx, c_ctx, w_ada, b_ada, norm_mix_g, w_in, q_norm_g, k_norm_g, da_lambda, subln_g, gm_ln_g, gm_ln_b, gm_ws, gm_bs, gm_out_g, w_out, norm_ffn_g, w_router, router_bias, we_gate, we_up, we_down, ws_gate, ws_up, ws_down):
    assert w_ada.shape[0] == 1, "single-layer kernel"
    b, l, d = x.shape
    lambda_init = 0.8 - 0.6 * math.exp(-0.3 * 0)
    lp = da_lambda[0].astype(F32)
    lam = (jnp.exp(jnp.sum(lp[0] * lp[1])) - jnp.exp(jnp.sum(lp[2] * lp[3])) + lambda_init).reshape(1)

    cond_rows = 16
    cond = jnp.zeros((cond_rows, d), F32).at[:b].set(c).at[b].set(c_ctx)
    ada = _ada_call(cond, w_ada[0], b_ada[0][None, :]).reshape(cond_rows, 6, d)
    mod = ada[:b]
    mod_ctx = ada[b:b + 1]

    cos, sin = _rope_tables(l)
    half = jnp.arange(LANES) // DA_HEAD_DIM
    gmat = (half[:, None] == half[None, :]).astype(BF)
    qg = jnp.tile(q_norm_g[0], 2)[None, :]
    kg = jnp.tile(k_norm_g[0], 2)[None, :]
    w_in_bf = w_in[0].astype(BF)
    bs_full = jnp.broadcast_to(gm_bs[0][:, :, None], (GM_HEADS, CHUNK, GM_HEAD_DIM))

    tm = min(ROUTE_TILE, l)
    q, k, v, gm = _inproj_call(
        x, mod, norm_mix_g, w_in_bf, qg, kg, cos, sin, gmat,
        gm_ln_g[0].reshape(1, GM_WIDTH), gm_ln_b[0].reshape(1, GM_WIDTH),
        gm_ws[0].astype(BF), bs_full, gm_out_g[0].reshape(1, GM_WIDTH), min(INPROJ_TILE, l))
    kc, vc = _ctxproj_call(ctx, mod_ctx, norm_mix_g, w_in_bf[:, DA_WIDTH:3 * DA_WIDTH], kg, gmat)

    attn = _attn_call(lam, q, k, kc, v, vc, subln_g[0][:, None],
                      tq=min(ATTN_Q_TILE, l), ck=min(ATTN_KEY_CHUNK, ctx.shape[1]),
                      out_scale=1.0 - lambda_init)

    wr = w_router[0].T.astype(BF)
    rb = router_bias[0][:, None]
    tok = jnp.arange(tm)
    tri = (tok[:, None] < tok[None, :]).astype(BF)
    y0, fx, gt, pt, counts = _outproj_call(
        x, attn, gm, mod, w_out[0].astype(BF), norm_ffn_g, wr, rb,
        ws_gate[0].astype(BF), ws_up[0].astype(BF), ws_down[0].astype(BF), tri, tm)

    totals = jnp.sum(counts.reshape(b, -1, N_EXPERTS), axis=1)
    order = jnp.argsort(-totals, axis=1).astype(jnp.int32).reshape(-1)
    return _moe_call(counts.reshape(-1), order, fx, gt, pt,
                     we_gate[0].astype(BF), we_up[0].astype(BF), we_down[0].astype(BF),
                     y0, mod, tt=min(MOE_TILE, l), sub=tm, caps=MOE_CAPS, epg=MOE_EXPERTS_PER_STEP)
```

```python
import functools
import math

import jax
import jax.numpy as jnp
from jax import lax
from jax.experimental import pallas as pl
from jax.experimental.pallas import tpu as pltpu

BF = jnp.bfloat16
F32 = jnp.float32

EPS = 1e-6
GRID_W = 64
DA_HEADS = 4
DA_HEAD_DIM = 64
DA_V_DIM = 128
DA_WIDTH = 512
GM_HEADS = 4
GM_HEAD_DIM = 128
GM_WIDTH = 512
CHUNK = 128
ROPE_THETA = 10000.0
Q_SCALE = DA_HEAD_DIM ** -0.5 * math.log2(math.e)
N_EXPERTS = 32
TOP_K = 4
N_GROUPS = 4
TOPK_GROUPS = 2
GROUP_SIZE = N_EXPERTS // N_GROUPS
ROUTED_SCALE = 2.5
ATTN_Q_TILE = 512
ATTN_KEY_CHUNK = 256
INPROJ_TILE = 512
ROUTE_TILE = 512
MOE_TILE = 2048
MOE_CAPS = (32, 64, 96, 128, 192, 256)
MOE_EXPERTS_PER_STEP = 2
LANES = 128
VMEM_LIMIT = 56 * 1024 * 1024


def _sigmoid(x):
    return 1.0 / (1.0 + jnp.exp(-x))


def _dot(a, b):
    return jnp.dot(a, b, preferred_element_type=F32)


def _dot_nt(a, b):
    return lax.dot_general(a, b, (((1,), (1,)), ((), ())), preferred_element_type=F32)


def _rms_rows(x, g):
    ms = jnp.mean(x * x, axis=-1, keepdims=True)
    return x * lax.rsqrt(ms + EPS) * g


def _group_mean_sq(y, gmat):
    y2 = y * y
    hi = y2.astype(BF)
    lo = (y2 - hi.astype(F32)).astype(BF)
    return (_dot(hi, gmat) + _dot(lo, gmat)) * (1.0 / DA_HEAD_DIM)


def _swap_pairs(y):
    lane = lax.broadcasted_iota(jnp.int32, y.shape, 1)
    nxt = pltpu.roll(y, LANES - 1, 1)
    prv = pltpu.roll(y, 1, 1)
    return jnp.where((lane & 1) == 0, nxt, prv)


def _ada_kernel(cond_ref, w_ref, b_ref, o_ref):
    c = cond_ref[...]
    s = c * _sigmoid(c)
    o_ref[...] = _dot(s.astype(BF), w_ref[...].astype(BF)) + b_ref[...]


def _ada_call(cond, w_ada, b_ada):
    rows, d = cond.shape
    n = w_ada.shape[1]
    bn = 1024
    return pl.pallas_call(
        _ada_kernel,
        out_shape=jax.ShapeDtypeStruct((rows, n), F32),
        grid=(n // bn,),
        in_specs=[pl.BlockSpec((rows, d), lambda j: (0, 0)),
                  pl.BlockSpec((d, bn), lambda j: (0, j)),
                  pl.BlockSpec((1, bn), lambda j: (0, j))],
        out_specs=pl.BlockSpec((rows, bn), lambda j: (0, j)),
        compiler_params=pltpu.CompilerParams(dimension_semantics=("arbitrary",),
                                             vmem_limit_bytes=VMEM_LIMIT),
        name="ada",
    )(cond, w_ada, b_ada)


def _inproj_kernel(x_ref, mod_ref, ng_ref, w_ref, qg_ref, kg_ref, cos_ref, sin_ref, gmat_ref,
                   lng_ref, lnb_ref, ws_ref, bs_ref, og_ref,
                   q_ref, k_ref, v_ref, gm_ref):
    tm = x_ref.shape[1]
    x = x_ref[0]
    mod = mod_ref[0]
    h = (_rms_rows(x, ng_ref[...]) * (1.0 + mod[1:2]) + mod[0:1]).astype(BF)

    def proj(col0, width=2 * LANES):
        return _dot(h, w_ref[:, col0:col0 + width])

    def gelu(t):
        return 0.5 * t * (1.0 + lax.erf(t * math.sqrt(0.5)))

    gmat = gmat_ref[...]
    cos = cos_ref[...]
    sin = sin_ref[...]
    for jb in range(DA_HEADS // 2):
        pq = proj(jb * 2 * LANES)
        pk = proj(DA_WIDTH + jb * 2 * LANES)
        for jj in range(2):
            sl = slice((2 * jb + jj) * LANES, (2 * jb + jj + 1) * LANES)
            qj = pq[:, jj * LANES:(jj + 1) * LANES]
            qn = qj * lax.rsqrt(_group_mean_sq(qj, gmat) + EPS) * qg_ref[...]
            qr = qn * cos + _swap_pairs(qn) * sin
            q_ref[0, :, sl] = (qr * Q_SCALE).astype(BF)
            kj = pk[:, jj * LANES:(jj + 1) * LANES]
            kn = kj * lax.rsqrt(_group_mean_sq(kj, gmat) + EPS) * kg_ref[...]
            kr = kn * cos + _swap_pairs(kn) * sin
            k_ref[0, :, sl] = kr.astype(BF)
    v_ref[0] = proj(2 * DA_WIDTH, DA_WIDTH).astype(BF)

    for g in range(GM_HEADS):
        sl = slice(g * LANES, (g + 1) * LANES)
        if g % 2 == 0:
            zu = gelu(proj(3 * DA_WIDTH + g * LANES))
            zv = gelu(proj(3 * DA_WIDTH + GM_WIDTH + g * LANES))
        u = zu[:, (g % 2) * LANES:(g % 2 + 1) * LANES]
        vg = zv[:, (g % 2) * LANES:(g % 2 + 1) * LANES]
        mu = jnp.mean(vg, axis=-1, keepdims=True)
        xc = vg - mu
        var = jnp.mean(xc * xc, axis=-1, keepdims=True)
        vn = (xc * lax.rsqrt(var + EPS) * lng_ref[:, sl] + lnb_ref[:, sl]).astype(BF)
        for cidx in range(tm // CHUNK):
            rows = slice(cidx * CHUNK, (cidx + 1) * CHUNK)
            mixed = _dot(ws_ref[g], vn[rows]) + bs_ref[g]
            y = u[rows] * mixed
            gm_ref[0, rows, sl] = _rms_rows(y, og_ref[:, sl]).astype(BF)


def _inproj_call(x, mod, ng, w_in, qg, kg, cos, sin, gmat, lng, lnb, ws, bs, og, tm):
    b, l, d = x.shape
    nw = w_in.shape[1]
    full = lambda shape: pl.BlockSpec(shape, lambda bi, i: (0,) * len(shape))
    tok = pl.BlockSpec((1, tm, DA_WIDTH), lambda bi, i: (bi, i, 0))
    out = jax.ShapeDtypeStruct((b, l, DA_WIDTH), BF)
    return pl.pallas_call(
        _inproj_kernel,
        out_shape=(out, out, out, out),
        grid=(b, l // tm),
        in_specs=[pl.BlockSpec((1, tm, d), lambda bi, i: (bi, i, 0)),
                  pl.BlockSpec((1, 6, d), lambda bi, i: (bi, 0, 0)),
                  full((1, d)), full((d, nw)), full((1, LANES)), full((1, LANES)),
                  pl.BlockSpec((tm, LANES), lambda bi, i: (i, 0)),
                  pl.BlockSpec((tm, LANES), lambda bi, i: (i, 0)),
                  full((LANES, LANES)), full((1, GM_WIDTH)), full((1, GM_WIDTH)),
                  full((GM_HEADS, CHUNK, CHUNK)), full((GM_HEADS, CHUNK, GM_HEAD_DIM)),
                  full((1, GM_WIDTH))],
        out_specs=(tok, tok, tok, tok),
        compiler_params=pltpu.CompilerParams(dimension_semantics=("arbitrary", "arbitrary"),
                                             vmem_limit_bytes=VMEM_LIMIT),
        name="inproj",
    )(x, mod, ng, w_in, qg, kg, cos, sin, gmat, lng, lnb, ws, bs, og)


def _ctxproj_kernel(x_ref, mod_ref, ng_ref, w_ref, kg_ref, gmat_ref, k_ref, v_ref):
    x = x_ref[0]
    mod = mod_ref[0]
    h = _rms_rows(x, ng_ref[...]) * (1.0 + mod[1:2]) + mod[0:1]
    p = _dot(h.astype(BF), w_ref[...])
    gmat = gmat_ref[...]
    for j in range(DA_HEADS):
        sl = slice(j * LANES, (j + 1) * LANES)
        kj = p[:, sl]
        kn = kj * lax.rsqrt(_group_mean_sq(kj, gmat) + EPS) * kg_ref[...]
        k_ref[0, :, sl] = kn.astype(BF)
    v_ref[0] = p[:, DA_WIDTH:].astype(BF)


def _ctxproj_call(ctx, mod_ctx, ng, w_kv, kg, gmat):
    b, lc, d = ctx.shape
    full = lambda shape: pl.BlockSpec(shape, lambda bi: (0,) * len(shape))
    tok = pl.BlockSpec((1, lc, DA_WIDTH), lambda bi: (bi, 0, 0))
    out = jax.ShapeDtypeStruct((b, lc, DA_WIDTH), BF)
    return pl.pallas_call(
        _ctxproj_kernel,
        out_shape=(out, out),
        grid=(b,),
        in_specs=[pl.BlockSpec((1, lc, d), lambda bi: (bi, 0, 0)),
                  full((1, 6, d)), full((1, d)), full((d, 2 * DA_WIDTH)),
                  full((1, LANES)), full((LANES, LANES))],
        out_specs=(tok, tok),
        compiler_params=pltpu.CompilerParams(dimension_semantics=("arbitrary",),
                                             vmem_limit_bytes=VMEM_LIMIT),
        name="ctxproj",
    )(ctx, mod_ctx, ng, w_kv, kg, gmat)


def _attn_kernel(lam_ref, q_ref, kx_ref, kc_ref, vx_ref, vc_ref, sg_ref, o_ref,
                 k_scr, vt_scr, s_scr, m_scr, *, ck, out_scale):
    tq = q_ref.shape[1]
    lc = kc_ref.shape[1]
    lx = kx_ref.shape[1]
    nck = (lc + lx) // ck

    step = pl.program_id(2)
    last = pl.num_programs(2) - 1

    def stacked_queries():
        q = q_ref[0]
        lane = lax.broadcasted_iota(jnp.int32, q.shape, 1)
        zero = jnp.zeros_like(q)
        return jnp.concatenate([jnp.where(lane < DA_HEAD_DIM, q, zero),
                                jnp.where(lane >= DA_HEAD_DIM, q, zero)], axis=0)

    def score_chunk(qq, cidx, mrun):
        rows = slice(cidx * ck, (cidx + 1) * ck)
        st = _dot_nt(k_scr[rows, :], qq)
        s_scr[rows, :] = st
        return jnp.maximum(mrun, jnp.max(st.reshape(ck // 8, 8, 2 * tq), axis=0))

    def weight_chunk(m, cidx, acc, lrun):
        rows = slice(cidx * ck, (cidx + 1) * ck)
        pt = jnp.exp2(s_scr[rows, :] - m)
        acc = acc + _dot(vt_scr[cidx], pt.astype(BF))
        return acc, lrun + jnp.sum(pt.reshape(ck // 8, 8, 2 * tq), axis=0)

    def finish(acc, lrun):
        r = 1.0 / jnp.sum(lrun, axis=0, keepdims=True)
        ot = acc[:, :tq] * r[:, :tq] - lam_ref[0] * (acc[:, tq:] * r[:, tq:])
        ms = jnp.mean(ot * ot, axis=0, keepdims=True)
        on = ot * lax.rsqrt(ms + EPS) * sg_ref[...] * out_scale
        o_ref[0] = on.T.astype(BF)

    mrun0 = jnp.full((8, 2 * tq), -jnp.inf, F32)
    lrun0 = jnp.zeros((8, 2 * tq), F32)
    acc0 = jnp.zeros((DA_V_DIM, 2 * tq), F32)

    @pl.when(step == 0)
    def _first():
        k_scr[0:lc, :] = kc_ref[0]
        k_scr[lc:lc + lx, :] = kx_ref[0]
        for cidx in range(nck):
            lo = cidx * ck
            if lo < lc:
                blk = vc_ref[0, lo:lo + ck, :]
            else:
                blk = vx_ref[0, lo - lc:lo - lc + ck, :]
            vt_scr[cidx] = blk.astype(F32).T.astype(BF)
        qq = stacked_queries()
        mrun = mrun0
        for cidx in range(nck):
            mrun = score_chunk(qq, cidx, mrun)
        m_scr[...] = mrun

    @pl.when((step > 0) & (step < last))
    def _steady():
        qq = stacked_queries()
        m = jnp.max(m_scr[...], axis=0, keepdims=True)
        mrun, lrun, acc = mrun0, lrun0, acc0
        for cidx in range(nck):
            acc, lrun = weight_chunk(m, cidx, acc, lrun)
            mrun = score_chunk(qq, cidx, mrun)
        finish(acc, lrun)
        m_scr[...] = mrun

    @pl.when(step == last)
    def _last():
        m = jnp.max(m_scr[...], axis=0, keepdims=True)
        lrun, acc = lrun0, acc0
        for cidx in range(nck):
            acc, lrun = weight_chunk(m, cidx, acc, lrun)
        finish(acc, lrun)


def _attn_call(lam, q, kx, kc, vx, vc, subln_col, tq, ck, out_scale):
    b, l, _ = q.shape
    lc = kc.shape[1]
    lk = lc + l
    assert lk % ck == 0 and lc % ck == 0 and l % tq == 0
    nq = l // tq
    kern = functools.partial(_attn_kernel, ck=ck, out_scale=out_scale)
    return pl.pallas_call(
        kern,
        out_shape=jax.ShapeDtypeStruct((b, l, DA_WIDTH), BF),
        grid=(b, DA_HEADS, nq + 1),
        in_specs=[pl.BlockSpec(memory_space=pltpu.SMEM),
                  pl.BlockSpec((1, tq, LANES), lambda bi, h, i: (bi, jnp.minimum(i, nq - 1), h)),
                  pl.BlockSpec((1, l, LANES), lambda bi, h, i: (bi, 0, h)),
                  pl.BlockSpec((1, lc, LANES), lambda bi, h, i: (bi, 0, h)),
                  pl.BlockSpec((1, l, LANES), lambda bi, h, i: (bi, 0, h)),
                  pl.BlockSpec((1, lc, LANES), lambda bi, h, i: (bi, 0, h)),
                  pl.BlockSpec((DA_V_DIM, 1), lambda bi, h, i: (0, 0))],
        out_specs=pl.BlockSpec((1, tq, LANES), lambda bi, h, i: (bi, jnp.maximum(i - 1, 0), h)),
        scratch_shapes=[pltpu.VMEM((lk, LANES), BF),
                        pltpu.VMEM((lk // ck, DA_V_DIM, ck), BF),
                        pltpu.VMEM((lk, 2 * tq), F32),
                        pltpu.VMEM((8, 2 * tq), F32)],
        compiler_params=pltpu.CompilerParams(
            dimension_semantics=("arbitrary", "arbitrary", "arbitrary"),
            vmem_limit_bytes=VMEM_LIMIT),
        name="attn",
    )(lam, q, kx, kc, vx, vc, subln_col)


def _route(scores, rbias):
    tm = scores.shape[1]
    assert GROUP_SIZE == 8 and scores.shape[0] == N_EXPERTS
    row = lax.broadcasted_iota(jnp.int32, (GROUP_SIZE, tm), 0).astype(F32)
    neg = jnp.full((GROUP_SIZE, tm), -jnp.inf, F32)
    biased = scores + rbias
    groups = [slice(g * GROUP_SIZE, (g + 1) * GROUP_SIZE) for g in range(N_GROUPS)]
    vals = [biased[sl] for sl in groups]

    gscore = []
    for v in vals:
        m1 = jnp.max(v, axis=0, keepdims=True)
        i1 = jnp.min(jnp.where(v == m1, row, 1e9), axis=0, keepdims=True)
        m2 = jnp.max(jnp.where(row == i1, neg, v), axis=0, keepdims=True)
        gscore.append(m1 + m2)

    cur = []
    for g in range(N_GROUPS):
        beaten = jnp.zeros_like(gscore[g])
        for g2 in range(N_GROUPS):
            if g2 == g:
                continue
            beat = (gscore[g2] >= gscore[g]) if g2 < g else (gscore[g2] > gscore[g])
            beaten = beaten + jnp.where(beat, 1.0, 0.0)
        cur.append(jnp.where(beaten < TOPK_GROUPS, vals[g], neg))

    ids = [row + float(g * GROUP_SIZE) for g in range(N_GROUPS)]
    sel = [jnp.zeros((GROUP_SIZE, tm), F32) for _ in range(N_GROUPS)]
    for _ in range(TOP_K):
        best = functools.reduce(jnp.maximum, cur)
        mx = jnp.max(best, axis=0, keepdims=True)
        cand = functools.reduce(jnp.minimum, [jnp.where(c == mx, i, 1e9) for c, i in zip(cur, ids)])
        idx = jnp.min(cand, axis=0, keepdims=True)
        hits = [i == idx for i in ids]
        sel = [jnp.where(h, 1.0, s) for h, s in zip(hits, sel)]
        cur = [jnp.where(h, neg, c) for h, c in zip(hits, cur)]
    w = [s * scores[sl] for s, sl in zip(sel, groups)]
    total = jnp.sum(functools.reduce(lambda a, b: a + b, w), axis=0, keepdims=True)
    gates = [wg / total * ROUTED_SCALE for wg in w]
    return jnp.concatenate(gates, axis=0), jnp.concatenate(sel, axis=0)


def _outproj_kernel(x_ref, at_ref, gm_ref, mod_ref, wo_ref, nfg_ref, wr_ref, rb_ref,
                    wsg_ref, wsu_ref, wsd_ref, tri_ref,
                    y0_ref, fx_ref, gt_ref, pt_ref, cnt_ref):
    mod = mod_ref[0]
    mix = _dot(at_ref[0], wo_ref[0:DA_WIDTH, :]) + _dot(gm_ref[0], wo_ref[DA_WIDTH:, :])
    x1 = x_ref[0] + mod[2:3] * mix
    fx = (_rms_rows(x1, nfg_ref[...]) * (1.0 + mod[4:5]) + mod[3:4]).astype(BF)
    fx_ref[0] = fx
    scores = _sigmoid(_dot_nt(wr_ref[...], fx))
    gates, sel = _route(scores, rb_ref[...])
    rank = _dot(sel.astype(BF), tri_ref[...])
    gt_ref[0] = gates
    pt_ref[0] = jnp.where(sel > 0.5, rank, -1.0)
    cnt_ref[0] = jnp.sum(sel, axis=1, keepdims=True).astype(jnp.int32)
    sg = _dot(fx, wsg_ref[...])
    su = _dot(fx, wsu_ref[...])
    hs = (sg * _sigmoid(sg)) * su
    y0_ref[0] = x1 + mod[5:6] * _dot(hs.astype(BF), wsd_ref[...])


def _outproj_call(x, attn, gm, mod, w_out, nfg, wr, rb, wsg, wsu, wsd, tri, tm):
    b, l, d = x.shape
    nt = l // tm
    full = lambda shape: pl.BlockSpec(shape, lambda bi, i: (0,) * len(shape))
    tokd = pl.BlockSpec((1, tm, d), lambda bi, i: (bi, i, 0))
    tokh = pl.BlockSpec((1, tm, DA_WIDTH), lambda bi, i: (bi, i, 0))
    expt = pl.BlockSpec((1, N_EXPERTS, tm), lambda bi, i: (bi, 0, i))
    ds = wsg.shape[1]
    return pl.pallas_call(
        _outproj_kernel,
        out_shape=(jax.ShapeDtypeStruct((b, l, d), F32),
                   jax.ShapeDtypeStruct((b, l, d), BF),
                   jax.ShapeDtypeStruct((b, N_EXPERTS, l), F32),
                   jax.ShapeDtypeStruct((b, N_EXPERTS, l), F32),
                   jax.ShapeDtypeStruct((b * nt, N_EXPERTS, 1), jnp.int32)),
        grid=(b, nt),
        in_specs=[tokd, tokh, tokh,
                  pl.BlockSpec((1, 6, d), lambda bi, i: (bi, 0, 0)),
                  full((d, d)), full((1, d)), full((N_EXPERTS, d)), full((N_EXPERTS, 1)),
                  full((d, ds)), full((d, ds)), full((ds, d)), full((tm, tm))],
        out_specs=(tokd, tokd, expt, expt,
                   pl.BlockSpec((1, N_EXPERTS, 1), lambda bi, i: (bi * nt + i, 0, 0))),
        compiler_params=pltpu.CompilerParams(dimension_semantics=("arbitrary", "arbitrary"),
                                             vmem_limit_bytes=VMEM_LIMIT),
        name="outproj",
    )(x, attn, gm, mod, w_out, nfg, wr, rb, wsg, wsu, wsd, tri)


def _moe_kernel(cnt_ref, order_ref, fx_ref, gt_ref, pt_ref, *refs, sub, caps, epg):
    wg_ref, wu_ref, wd_ref = refs[0:epg], refs[epg:2 * epg], refs[2 * epg:3 * epg]
    y0_ref, mod_ref, o_ref = refs[3 * epg:]
    tt = fx_ref.shape[1]
    nsub = tt // sub
    pair = pl.program_id(2)
    experts = [order_ref[pl.program_id(0) * N_EXPERTS + pair * epg + k] for k in range(epg)]
    sub0 = (pl.program_id(0) * pl.num_programs(1) + pl.program_id(1)) * nsub

    @pl.when(pair == 0)
    def _zero():
        o_ref[...] = jnp.zeros_like(o_ref)

    cmax = jnp.int32(0)
    for s in range(nsub):
        for k in range(epg):
            cmax = jnp.maximum(cmax, cnt_ref[(sub0 + s) * N_EXPERTS + experts[k]])

    def do_round(r, cap):
        slot = lax.broadcasted_iota(jnp.int32, (cap, sub), 0).astype(F32) + (r * cap).astype(F32)
        picks, xrows, grows = [], [], []
        for s in range(nsub):
            cols = slice(s * sub, (s + 1) * sub)
            hit = [pt_ref[0, pl.ds(experts[k], 1), cols] == slot for k in range(epg)]
            grows.append([jnp.sum(jnp.where(hit[k], gt_ref[0, pl.ds(experts[k], 1), cols], 0.0),
                                  axis=-1, keepdims=True) for k in range(epg)])
            pick = jnp.concatenate([jnp.where(h, 1.0, 0.0) for h in hit], axis=0).astype(BF)
            picks.append(pick)
            xrows.append(_dot(pick, fx_ref[0, cols, :]).astype(BF))
        outs = []
        for k in range(epg):
            xk = jnp.concatenate([xrows[s][k * cap:(k + 1) * cap] for s in range(nsub)], axis=0)
            gk = jnp.concatenate([grows[s][k] for s in range(nsub)], axis=0)
            a = _dot(xk, wg_ref[k][0])
            bb = _dot(xk, wu_ref[k][0])
            hm = (a * _sigmoid(a)) * bb * gk
            outs.append(_dot(hm.astype(BF), wd_ref[k][0]).astype(BF))
        for s in range(nsub):
            cols = slice(s * sub, (s + 1) * sub)
            stacked = jnp.concatenate([outs[k][s * cap:(s + 1) * cap] for k in range(epg)], axis=0)
            o_ref[0, cols, :] += lax.dot_general(picks[s], stacked, (((0,), (0,)), ((), ())),
                                                 preferred_element_type=F32)

    below = 0
    for cap in caps:
        fits = (cmax <= cap) if below == 0 else ((cmax > below) & (cmax <= cap))
        pl.when(fits)(functools.partial(do_round, jnp.int32(0), cap))
        below = cap

    @pl.when(cmax > caps[-1])
    def _many_rounds():
        def body(r, carry):
            do_round(r, caps[-1])
            return carry
        lax.fori_loop(0, (cmax + (caps[-1] - 1)) // caps[-1], body, 0)

    @pl.when(pair == pl.num_programs(2) - 1)
    def _finish():
        o_ref[0] = y0_ref[0] + mod_ref[0][5:6] * o_ref[0]


def _moe_call(counts, order, fx, gt, pt, wg, wu, wd, y0, mod, tt, sub, caps, epg):
    b, l, d = fx.shape
    ne, _, de = wg.shape
    assert l % tt == 0 and tt % sub == 0 and ne % epg == 0
    assert all(c % 16 == 0 for c in caps) and list(caps) == sorted(caps)
    tokd = pl.BlockSpec((1, tt, d), lambda bi, i, p, cnt, order: (bi, i, 0))
    expt = pl.BlockSpec((1, ne, tt), lambda bi, i, p, cnt, order: (bi, 0, i))

    def expert_block(shape, k):
        return pl.BlockSpec((1,) + shape,
                            lambda bi, i, p, cnt, order: (order[bi * ne + p * epg + k], 0, 0))

    kern = functools.partial(_moe_kernel, sub=sub, caps=tuple(caps), epg=epg)
    grid_spec = pltpu.PrefetchScalarGridSpec(
        num_scalar_prefetch=2,
        grid=(b, l // tt, ne // epg),
        in_specs=([tokd, expt, expt]
                  + [expert_block((d, de), k) for k in range(epg)]
                  + [expert_block((d, de), k) for k in range(epg)]
                  + [expert_block((de, d), k) for k in range(epg)]
                  + [tokd, pl.BlockSpec((1, 6, d), lambda bi, i, p, cnt, order: (bi, 0, 0))]),
        out_specs=tokd)
    return pl.pallas_call(
        kern,
        out_shape=jax.ShapeDtypeStruct((b, l, d), F32),
        grid_spec=grid_spec,
        compiler_params=pltpu.CompilerParams(
            dimension_semantics=("arbitrary", "arbitrary", "arbitrary"),
            vmem_limit_bytes=VMEM_LIMIT),
        name="moe",
    )(counts, order, fx, gt, pt, *([wg] * epg), *([wu] * epg), *([wd] * epg), y0, mod)


def _rope_tables(n_tokens):
    rows = n_tokens // GRID_W
    row = jnp.repeat(jnp.arange(rows, dtype=F32), GRID_W)
    col = jnp.tile(jnp.arange(GRID_W, dtype=F32), rows)
    half = DA_HEAD_DIM // 2
    inv_freq = ROPE_THETA ** (-jnp.arange(0, half, 2, dtype=F32) / half)
    ang = jnp.concatenate([row[:, None] * inv_freq, col[:, None] * inv_freq], axis=-1)
    cos, sin = jnp.cos(ang), jnp.sin(ang)
    cos64 = jnp.repeat(cos, 2, axis=-1)
    sin64 = jnp.stack([-sin, sin], axis=-1).reshape(n_tokens, DA_HEAD_DIM)
    return jnp.tile(cos64, (1, 2)), jnp.tile(sin64, (1, 2))


def kernel(x, c, ctx, c_ctx, w_ada, b_ada, norm_mix_g, w_in, q_norm_g, k_norm_g, da_lambda, subln_g, gm_ln_g, gm_ln_b, gm_ws, gm_bs, gm_out_g, w_out, norm_ffn_g, w_router, router_bias, we_gate, we_up, we_down, ws_gate, ws_up, ws_down):
    assert w_ada.shape[0] == 1, "single-layer kernel"
    b, l, d = x.shape
    lambda_init = 0.8 - 0.6 * math.exp(-0.3 * 0)
    lp = da_lambda[0].astype(F32)
    lam = (jnp.exp(jnp.sum(lp[0] * lp[1])) - jnp.exp(jnp.sum(lp[2] * lp[3])) + lambda_init).reshape(1)

    cond_rows = 16
    cond = jnp.zeros((cond_rows, d), F32).at[:b].set(c).at[b].set(c_ctx)
    ada = _ada_call(cond, w_ada[0], b_ada[0][None, :]).reshape(cond_rows, 6, d)
    mod = ada[:b]
    mod_ctx = ada[b:b + 1]

    cos, sin = _rope_tables(l)
    half = jnp.arange(LANES) // DA_HEAD_DIM
    gmat = (half[:, None] == half[None, :]).astype(BF)
    qg = jnp.tile(q_norm_g[0], 2)[None, :]
    kg = jnp.tile(k_norm_g[0], 2)[None, :]
    w_in_bf = w_in[0].astype(BF)
    bs_full = jnp.broadcast_to(gm_bs[0][:, :, None], (GM_HEADS, CHUNK, GM_HEAD_DIM))

    tm = min(ROUTE_TILE, l)
    q, k, v, gm = _inproj_call(
        x, mod, norm_mix_g, w_in_bf, qg, kg, cos, sin, gmat,
        gm_ln_g[0].reshape(1, GM_WIDTH), gm_ln_b[0].reshape(1, GM_WIDTH),
        gm_ws[0].astype(BF), bs_full, gm_out_g[0].reshape(1, GM_WIDTH), min(INPROJ_TILE, l))
    kc, vc = _ctxproj_call(ctx, mod_ctx, norm_mix_g, w_in_bf[:, DA_WIDTH:3 * DA_WIDTH], kg, gmat)

    attn = _attn_call(lam, q, k, kc, v, vc, subln_g[0][:, None],
                      tq=min(ATTN_Q_TILE, l), ck=min(ATTN_KEY_CHUNK, ctx.shape[1]),
                      out_scale=1.0 - lambda_init)

    wr = w_router[0].T.astype(BF)
    rb = router_bias[0][:, None]
    tok = jnp.arange(tm)
    tri = (tok[:, None] < tok[None, :]).astype(BF)
    y0, fx, gt, pt, counts = _outproj_call(
        x, attn, gm, mod, w_out[0].astype(BF), norm_ffn_g, wr, rb,
        ws_gate[0].astype(BF), ws_up[0].astype(BF), ws_down[0].astype(BF), tri, tm)

    totals = jnp.sum(counts.reshape(b, -1, N_EXPERTS), axis=1)
    order = jnp.argsort(-totals, axis=1).astype(jnp.int32).reshape(-1)
    return _moe_call(counts.reshape(-1), order, fx, gt, pt,
                     we_gate[0].astype(BF), we_up[0].astype(BF), we_down[0].astype(BF),
                     y0, mod, tt=min(MOE_TILE, l), sub=tm, caps=MOE_CAPS, epg=MOE_EXPERTS_PER_STEP)
```

```python
import functools
import math

import jax
import jax.numpy as jnp
from jax import lax
from jax.experimental import pallas as pl
from jax.experimental.pallas import tpu as pltpu

BF = jnp.bfloat16
F32 = jnp.float32

EPS = 1e-6
GRID_W = 64
DA_HEADS = 4
DA_HEAD_DIM = 64
DA_V_DIM = 128
DA_WIDTH = 512
GM_HEADS = 4
GM_HEAD_DIM = 128
GM_WIDTH = 512
CHUNK = 128
ROPE_THETA = 10000.0
Q_SCALE = DA_HEAD_DIM ** -0.5 * math.log2(math.e)
N_EXPERTS = 32
TOP_K = 4
N_GROUPS = 4
TOPK_GROUPS = 2
GROUP_SIZE = N_EXPERTS // N_GROUPS
ROUTED_SCALE = 2.5
ATTN_Q_TILE = 256
ATTN_KEY_CHUNK = 256
INPROJ_TILE = 512
ROUTE_TILE = 512
MOE_TILE = 2048
MOE_CAPS = (32, 64, 96, 128, 192, 256)
MOE_EXPERTS_PER_STEP = 2
ADA_BLOCK = 1024
COND_ROWS = 16
LANES = 128
VMEM_LIMIT = 56 * 1024 * 1024


def _sigmoid(x):
    return 1.0 / (1.0 + jnp.exp(-x))


def _dot(a, b):
    return jnp.dot(a, b, preferred_element_type=F32)


def _dot_nt(a, b):
    return lax.dot_general(a, b, (((1,), (1,)), ((), ())), preferred_element_type=F32)


def _rms_rows(x, g):
    ms = jnp.mean(x * x, axis=-1, keepdims=True)
    return x * lax.rsqrt(ms + EPS) * g


def _group_mean_sq(y, gmat):
    y2 = y * y
    hi = y2.astype(BF)
    lo = (y2 - hi.astype(F32)).astype(BF)
    return (_dot(hi, gmat) + _dot(lo, gmat)) * (1.0 / DA_HEAD_DIM)


def _swap_pairs(y):
    lane = lax.broadcasted_iota(jnp.int32, y.shape, 1)
    nxt = pltpu.roll(y, LANES - 1, 1)
    prv = pltpu.roll(y, 1, 1)
    return jnp.where((lane & 1) == 0, nxt, prv)


def _ada_kernel(cond_ref, w_ref, b_ref, o_ref):
    c = cond_ref[...]
    s = c * _sigmoid(c)
    o_ref[...] = _dot(s.astype(BF), w_ref[...].astype(BF)) + b_ref[...]


def _ada_call(cond, w_ada, b_ada):
    rows, d = cond.shape
    n = w_ada.shape[1]
    bn = ADA_BLOCK
    return pl.pallas_call(
        _ada_kernel,
        out_shape=jax.ShapeDtypeStruct((rows, n), F32),
        grid=(n // bn,),
        in_specs=[pl.BlockSpec((rows, d), lambda j: (0, 0)),
                  pl.BlockSpec((d, bn), lambda j: (0, j)),
                  pl.BlockSpec((1, bn), lambda j: (0, j))],
        out_specs=pl.BlockSpec((rows, bn), lambda j: (0, j)),
        compiler_params=pltpu.CompilerParams(dimension_semantics=("arbitrary",),
                                             vmem_limit_bytes=VMEM_LIMIT),
        name="ada",
    )(cond, w_ada, b_ada)


def _inproj_kernel(x_ref, mod_ref, ng_ref, w_ref, qg_ref, kg_ref, cos_ref, sin_ref, gmat_ref,
                   lng_ref, lnb_ref, ws_ref, bs_ref, og_ref,
                   q_ref, k_ref, v_ref, gm_ref):
    tm = x_ref.shape[1]
    x = x_ref[0]
    mod = mod_ref[0]
    h = (_rms_rows(x, ng_ref[...]) * (1.0 + mod[1:2]) + mod[0:1]).astype(BF)

    def proj(col0, width=2 * LANES):
        return _dot(h, w_ref[:, col0:col0 + width])

    def gelu(t):
        return 0.5 * t * (1.0 + lax.erf(t * math.sqrt(0.5)))

    gmat = gmat_ref[...]
    cos = cos_ref[...]
    sin = sin_ref[...]
    for jb in range(DA_HEADS // 2):
        pq = proj(jb * 2 * LANES)
        pk = proj(DA_WIDTH + jb * 2 * LANES)
        for jj in range(2):
            sl = slice((2 * jb + jj) * LANES, (2 * jb + jj + 1) * LANES)
            qj = pq[:, jj * LANES:(jj + 1) * LANES]
            qn = qj * lax.rsqrt(_group_mean_sq(qj, gmat) + EPS) * qg_ref[...]
            qr = qn * cos + _swap_pairs(qn) * sin
            q_ref[0, :, sl] = (qr * Q_SCALE).astype(BF)
            kj = pk[:, jj * LANES:(jj + 1) * LANES]
            kn = kj * lax.rsqrt(_group_mean_sq(kj, gmat) + EPS) * kg_ref[...]
            kr = kn * cos + _swap_pairs(kn) * sin
            k_ref[0, :, sl] = kr.astype(BF)
    v_ref[0] = proj(2 * DA_WIDTH, DA_WIDTH).astype(BF)

    for g in range(GM_HEADS):
        sl = slice(g * LANES, (g + 1) * LANES)
        if g % 2 == 0:
            zu = gelu(proj(3 * DA_WIDTH + g * LANES))
            zv = gelu(proj(3 * DA_WIDTH + GM_WIDTH + g * LANES))
        u = zu[:, (g % 2) * LANES:(g % 2 + 1) * LANES]
        vg = zv[:, (g % 2) * LANES:(g % 2 + 1) * LANES]
        mu = jnp.mean(vg, axis=-1, keepdims=True)
        xc = vg - mu
        var = jnp.mean(xc * xc, axis=-1, keepdims=True)
        vn = (xc * lax.rsqrt(var + EPS) * lng_ref[:, sl] + lnb_ref[:, sl]).astype(BF)
        for cidx in range(tm // CHUNK):
            rows = slice(cidx * CHUNK, (cidx + 1) * CHUNK)
            mixed = _dot(ws_ref[g], vn[rows]) + bs_ref[g]
            y = u[rows] * mixed
            gm_ref[0, rows, sl] = _rms_rows(y, og_ref[:, sl]).astype(BF)


def _inproj_call(x, mod, ng, w_in, qg, kg, cos, sin, gmat, lng, lnb, ws, bs, og, tm):
    b, l, d = x.shape
    nw = w_in.shape[1]
    full = lambda shape: pl.BlockSpec(shape, lambda bi, i: (0,) * len(shape))
    tok = pl.BlockSpec((1, tm, DA_WIDTH), lambda bi, i: (bi, i, 0))
    out = jax.ShapeDtypeStruct((b, l, DA_WIDTH), BF)
    return pl.pallas_call(
        _inproj_kernel,
        out_shape=(out, out, out, out),
        grid=(b, l // tm),
        in_specs=[pl.BlockSpec((1, tm, d), lambda bi, i: (bi, i, 0)),
                  pl.BlockSpec((1, 6, d), lambda bi, i: (bi, 0, 0)),
                  full((1, d)), full((d, nw)), full((1, LANES)), full((1, LANES)),
                  pl.BlockSpec((tm, LANES), lambda bi, i: (i, 0)),
                  pl.BlockSpec((tm, LANES), lambda bi, i: (i, 0)),
                  full((LANES, LANES)), full((1, GM_WIDTH)), full((1, GM_WIDTH)),
                  full((GM_HEADS, CHUNK, CHUNK)), full((GM_HEADS, CHUNK, GM_HEAD_DIM)),
                  full((1, GM_WIDTH))],
        out_specs=(tok, tok, tok, tok),
        compiler_params=pltpu.CompilerParams(dimension_semantics=("arbitrary", "arbitrary"),
                                             vmem_limit_bytes=VMEM_LIMIT),
        name="inproj",
    )(x, mod, ng, w_in, qg, kg, cos, sin, gmat, lng, lnb, ws, bs, og)


def _ctxproj_kernel(x_ref, mod_ref, ng_ref, w_ref, kg_ref, gmat_ref, k_ref, v_ref):
    x = x_ref[0]
    mod = mod_ref[0]
    h = _rms_rows(x, ng_ref[...]) * (1.0 + mod[1:2]) + mod[0:1]
    p = _dot(h.astype(BF), w_ref[...])
    gmat = gmat_ref[...]
    for j in range(DA_HEADS):
        sl = slice(j * LANES, (j + 1) * LANES)
        kj = p[:, sl]
        kn = kj * lax.rsqrt(_group_mean_sq(kj, gmat) + EPS) * kg_ref[...]
        k_ref[0, :, sl] = kn.astype(BF)
    v_ref[0] = p[:, DA_WIDTH:].astype(BF)


def _ctxproj_call(ctx, mod_ctx, ng, w_kv, kg, gmat):
    b, lc, d = ctx.shape
    full = lambda shape: pl.BlockSpec(shape, lambda bi: (0,) * len(shape))
    tok = pl.BlockSpec((1, lc, DA_WIDTH), lambda bi: (bi, 0, 0))
    out = jax.ShapeDtypeStruct((b, lc, DA_WIDTH), BF)
    return pl.pallas_call(
        _ctxproj_kernel,
        out_shape=(out, out),
        grid=(b,),
        in_specs=[pl.BlockSpec((1, lc, d), lambda bi: (bi, 0, 0)),
                  full((1, 6, d)), full((1, d)), full((d, 2 * DA_WIDTH)),
                  full((1, LANES)), full((LANES, LANES))],
        out_specs=(tok, tok),
        compiler_params=pltpu.CompilerParams(dimension_semantics=("arbitrary",),
                                             vmem_limit_bytes=VMEM_LIMIT),
        name="ctxproj",
    )(ctx, mod_ctx, ng, w_kv, kg, gmat)


def _attn_kernel(lam_ref, q_ref, kx_ref, kc_ref, vx_ref, vc_ref, sg_ref, *refs,
                 ck, out_scale, n_cast):
    cast_src = refs[:n_cast]
    o_ref = refs[n_cast]
    cast_dst = refs[n_cast + 1:2 * n_cast + 1]
    k_scr, vt_scr, s_scr, m_scr = refs[2 * n_cast + 1:]
    tq = q_ref.shape[1]
    lc = kc_ref.shape[1]
    lx = kx_ref.shape[1]
    nck = (lc + lx) // ck

    step = pl.program_id(2)
    last = pl.num_programs(2) - 1

    def stacked_queries():
        q = q_ref[0]
        lane = lax.broadcasted_iota(jnp.int32, q.shape, 1)
        zero = jnp.zeros_like(q)
        return jnp.concatenate([jnp.where(lane < DA_HEAD_DIM, q, zero),
                                jnp.where(lane >= DA_HEAD_DIM, q, zero)], axis=0)

    def score_chunk(qq, cidx, mrun):
        rows = slice(cidx * ck, (cidx + 1) * ck)
        st = _dot_nt(k_scr[rows, :], qq)
        s_scr[rows, :] = st
        return jnp.maximum(mrun, jnp.max(st.reshape(ck // 8, 8, 2 * tq), axis=0))

    def weight_chunk(m, cidx, acc, lrun):
        rows = slice(cidx * ck, (cidx + 1) * ck)
        pt = jnp.exp2(s_scr[rows, :] - m)
        acc = acc + _dot(vt_scr[cidx], pt.astype(BF))
        return acc, lrun + jnp.sum(pt.reshape(ck // 8, 8, 2 * tq), axis=0)

    def finish(acc, lrun):
        r = 1.0 / jnp.sum(lrun, axis=0, keepdims=True)
        ot = acc[:, :tq] * r[:, :tq] - lam_ref[0] * (acc[:, tq:] * r[:, tq:])
        ms = jnp.mean(ot * ot, axis=0, keepdims=True)
        on = ot * lax.rsqrt(ms + EPS) * sg_ref[...] * out_scale
        o_ref[0] = on.T.astype(BF)

    mrun0 = jnp.full((8, 2 * tq), -jnp.inf, F32)
    lrun0 = jnp.zeros((8, 2 * tq), F32)
    acc0 = jnp.zeros((DA_V_DIM, 2 * tq), F32)

    @pl.when(step == 0)
    def _first():
        for src, dst in zip(cast_src, cast_dst):
            dst[...] = src[...].astype(BF)
        k_scr[0:lc, :] = kc_ref[0]
        k_scr[lc:lc + lx, :] = kx_ref[0]
        for cidx in range(nck):
            lo = cidx * ck
            if lo < lc:
                blk = vc_ref[0, lo:lo + ck, :]
            else:
                blk = vx_ref[0, lo - lc:lo - lc + ck, :]
            vt_scr[cidx] = blk.astype(F32).T.astype(BF)
        qq = stacked_queries()
        mrun = mrun0
        for cidx in range(nck):
            mrun = score_chunk(qq, cidx, mrun)
        m_scr[...] = mrun

    @pl.when((step > 0) & (step < last))
    def _steady():
        qq = stacked_queries()
        m = jnp.max(m_scr[...], axis=0, keepdims=True)
        mrun, lrun, acc = mrun0, lrun0, acc0
        for cidx in range(nck):
            acc, lrun = weight_chunk(m, cidx, acc, lrun)
            mrun = score_chunk(qq, cidx, mrun)
        finish(acc, lrun)
        m_scr[...] = mrun

    @pl.when(step == last)
    def _last():
        m = jnp.max(m_scr[...], axis=0, keepdims=True)
        lrun, acc = lrun0, acc0
        for cidx in range(nck):
            acc, lrun = weight_chunk(m, cidx, acc, lrun)
        finish(acc, lrun)


def _attn_call(lam, q, kx, kc, vx, vc, subln_col, tq, ck, out_scale, cast=()):
    b, l, _ = q.shape
    lc = kc.shape[1]
    lk = lc + l
    assert lk % ck == 0 and lc % ck == 0 and l % tq == 0
    nq = l // tq
    kern = functools.partial(_attn_kernel, ck=ck, out_scale=out_scale, n_cast=len(cast))
    cast_specs = [pl.BlockSpec((w.shape[0] // (b * DA_HEADS),) + w.shape[1:],
                               lambda bi, h, i: (bi * DA_HEADS + h, 0, 0)) for w in cast]
    return pl.pallas_call(
        kern,
        out_shape=(jax.ShapeDtypeStruct((b, l, DA_WIDTH), BF),
                   *[jax.ShapeDtypeStruct(w.shape, BF) for w in cast]),
        grid=(b, DA_HEADS, nq + 1),
        in_specs=[pl.BlockSpec(memory_space=pltpu.SMEM),
                  pl.BlockSpec((1, tq, LANES), lambda bi, h, i: (bi, jnp.minimum(i, nq - 1), h)),
                  pl.BlockSpec((1, l, LANES), lambda bi, h, i: (bi, 0, h)),
                  pl.BlockSpec((1, lc, LANES), lambda bi, h, i: (bi, 0, h)),
                  pl.BlockSpec((1, l, LANES), lambda bi, h, i: (bi, 0, h)),
                  pl.BlockSpec((1, lc, LANES), lambda bi, h, i: (bi, 0, h)),
                  pl.BlockSpec((DA_V_DIM, 1), lambda bi, h, i: (0, 0)),
                  *cast_specs],
        out_specs=(pl.BlockSpec((1, tq, LANES), lambda bi, h, i: (bi, jnp.maximum(i - 1, 0), h)),
                   *cast_specs),
        scratch_shapes=[pltpu.VMEM((lk, LANES), BF),
                        pltpu.VMEM((lk // ck, DA_V_DIM, ck), BF),
                        pltpu.VMEM((lk, 2 * tq), F32),
                        pltpu.VMEM((8, 2 * tq), F32)],
        compiler_params=pltpu.CompilerParams(
            dimension_semantics=("arbitrary", "arbitrary", "arbitrary"),
            vmem_limit_bytes=VMEM_LIMIT),
        name="attn",
    )(lam, q, kx, kc, vx, vc, subln_col, *cast)


def _route(scores, rbias):
    tm = scores.shape[1]
    assert GROUP_SIZE == 8 and scores.shape[0] == N_EXPERTS
    row = lax.broadcasted_iota(jnp.int32, (GROUP_SIZE, tm), 0).astype(F32)
    neg = jnp.full((GROUP_SIZE, tm), -jnp.inf, F32)
    biased = scores + rbias
    groups = [slice(g * GROUP_SIZE, (g + 1) * GROUP_SIZE) for g in range(N_GROUPS)]
    vals = [biased[sl] for sl in groups]

    gscore = []
    for v in vals:
        m1 = jnp.max(v, axis=0, keepdims=True)
        i1 = jnp.min(jnp.where(v == m1, row, 1e9), axis=0, keepdims=True)
        m2 = jnp.max(jnp.where(row == i1, neg, v), axis=0, keepdims=True)
        gscore.append(m1 + m2)

    cur = []
    for g in range(N_GROUPS):
        beaten = jnp.zeros_like(gscore[g])
        for g2 in range(N_GROUPS):
            if g2 == g:
                continue
            beat = (gscore[g2] >= gscore[g]) if g2 < g else (gscore[g2] > gscore[g])
            beaten = beaten + jnp.where(beat, 1.0, 0.0)
        cur.append(jnp.where(beaten < TOPK_GROUPS, vals[g], neg))

    ids = [row + float(g * GROUP_SIZE) for g in range(N_GROUPS)]
    sel = [jnp.zeros((GROUP_SIZE, tm), F32) for _ in range(N_GROUPS)]
    for _ in range(TOP_K):
        best = functools.reduce(jnp.maximum, cur)
        mx = jnp.max(best, axis=0, keepdims=True)
        cand = functools.reduce(jnp.minimum, [jnp.where(c == mx, i, 1e9) for c, i in zip(cur, ids)])
        idx = jnp.min(cand, axis=0, keepdims=True)
        hits = [i == idx for i in ids]
        sel = [jnp.where(h, 1.0, s) for h, s in zip(hits, sel)]
        cur = [jnp.where(h, neg, c) for h, c in zip(hits, cur)]
    w = [s * scores[sl] for s, sl in zip(sel, groups)]
    total = jnp.sum(functools.reduce(lambda a, b: a + b, w), axis=0, keepdims=True)
    gates = [wg / total * ROUTED_SCALE for wg in w]
    return jnp.concatenate(gates, axis=0), jnp.concatenate(sel, axis=0)


def _outproj_kernel(x_ref, at_ref, gm_ref, mod_ref, wo_ref, nfg_ref, wr_ref, rb_ref,
                    wsg_ref, wsu_ref, wsd_ref, tri_ref,
                    y0_ref, fx_ref, gt_ref, pt_ref, cnt_ref):
    mod = mod_ref[0]
    mix = _dot(at_ref[0], wo_ref[0:DA_WIDTH, :]) + _dot(gm_ref[0], wo_ref[DA_WIDTH:, :])
    x1 = x_ref[0] + mod[2:3] * mix
    fx = (_rms_rows(x1, nfg_ref[...]) * (1.0 + mod[4:5]) + mod[3:4]).astype(BF)
    fx_ref[0] = fx
    scores = _sigmoid(_dot_nt(wr_ref[...], fx))
    gates, sel = _route(scores, rb_ref[...])
    rank = _dot(sel.astype(BF), tri_ref[...])
    gt_ref[0] = gates
    pt_ref[0] = jnp.where(sel > 0.5, rank, -1.0)
    cnt_ref[0] = jnp.sum(sel, axis=1, keepdims=True).astype(jnp.int32)
    sg = _dot(fx, wsg_ref[...])
    su = _dot(fx, wsu_ref[...])
    hs = (sg * _sigmoid(sg)) * su
    y0_ref[0] = x1 + mod[5:6] * _dot(hs.astype(BF), wsd_ref[...])


def _outproj_call(x, attn, gm, mod, w_out, nfg, wr, rb, wsg, wsu, wsd, tri, tm):
    b, l, d = x.shape
    nt = l // tm
    full = lambda shape: pl.BlockSpec(shape, lambda bi, i: (0,) * len(shape))
    tokd = pl.BlockSpec((1, tm, d), lambda bi, i: (bi, i, 0))
    tokh = pl.BlockSpec((1, tm, DA_WIDTH), lambda bi, i: (bi, i, 0))
    expt = pl.BlockSpec((1, N_EXPERTS, tm), lambda bi, i: (bi, 0, i))
    ds = wsg.shape[1]
    return pl.pallas_call(
        _outproj_kernel,
        out_shape=(jax.ShapeDtypeStruct((b, l, d), F32),
                   jax.ShapeDtypeStruct((b, l, d), BF),
                   jax.ShapeDtypeStruct((b, N_EXPERTS, l), F32),
                   jax.ShapeDtypeStruct((b, N_EXPERTS, l), F32),
                   jax.ShapeDtypeStruct((b * nt, N_EXPERTS, 1), jnp.int32)),
        grid=(b, nt),
        in_specs=[tokd, tokh, tokh,
                  pl.BlockSpec((1, 6, d), lambda bi, i: (bi, 0, 0)),
                  full((d, d)), full((1, d)), full((N_EXPERTS, d)), full((N_EXPERTS, 1)),
                  full((d, ds)), full((d, ds)), full((ds, d)), full((tm, tm))],
        out_specs=(tokd, tokd, expt, expt,
                   pl.BlockSpec((1, N_EXPERTS, 1), lambda bi, i: (bi * nt + i, 0, 0))),
        compiler_params=pltpu.CompilerParams(dimension_semantics=("arbitrary", "arbitrary"),
                                             vmem_limit_bytes=VMEM_LIMIT),
        name="outproj",
    )(x, attn, gm, mod, w_out, nfg, wr, rb, wsg, wsu, wsd, tri)


def _moe_kernel(cnt_ref, order_ref, fx_ref, gt_ref, pt_ref, *refs, sub, caps, epg):
    wg_ref, wu_ref, wd_ref = refs[0:epg], refs[epg:2 * epg], refs[2 * epg:3 * epg]
    y0_ref, mod_ref, o_ref = refs[3 * epg:]
    tt = fx_ref.shape[1]
    nsub = tt // sub
    pair = pl.program_id(2)
    experts = [order_ref[pl.program_id(0) * N_EXPERTS + pair * epg + k] for k in range(epg)]
    sub0 = (pl.program_id(0) * pl.num_programs(1) + pl.program_id(1)) * nsub

    @pl.when(pair == 0)
    def _zero():
        o_ref[...] = jnp.zeros_like(o_ref)

    cmax = jnp.int32(0)
    for s in range(nsub):
        for k in range(epg):
            cmax = jnp.maximum(cmax, cnt_ref[(sub0 + s) * N_EXPERTS + experts[k]])

    def do_round(r, cap):
        slot = lax.broadcasted_iota(jnp.int32, (cap, sub), 0).astype(F32) + (r * cap).astype(F32)
        picks, xrows, grows = [], [], []
        for s in range(nsub):
            cols = slice(s * sub, (s + 1) * sub)
            hit = [pt_ref[0, pl.ds(experts[k], 1), cols] == slot for k in range(epg)]
            grows.append([jnp.sum(jnp.where(hit[k], gt_ref[0, pl.ds(experts[k], 1), cols], 0.0),
                                  axis=-1, keepdims=True) for k in range(epg)])
            pick = jnp.concatenate([jnp.where(h, 1.0, 0.0) for h in hit], axis=0).astype(BF)
            picks.append(pick)
            xrows.append(_dot(pick, fx_ref[0, cols, :]).astype(BF))
        outs = []
        for k in range(epg):
            xk = jnp.concatenate([xrows[s][k * cap:(k + 1) * cap] for s in range(nsub)], axis=0)
            gk = jnp.concatenate([grows[s][k] for s in range(nsub)], axis=0)
            a = _dot(xk, wg_ref[k][0])
            bb = _dot(xk, wu_ref[k][0])
            hm = (a * _sigmoid(a)) * bb * gk
            outs.append(_dot(hm.astype(BF), wd_ref[k][0]).astype(BF))
        for s in range(nsub):
            cols = slice(s * sub, (s + 1) * sub)
            stacked = jnp.concatenate([outs[k][s * cap:(s + 1) * cap] for k in range(epg)], axis=0)
            o_ref[0, cols, :] += lax.dot_general(picks[s], stacked, (((0,), (0,)), ((), ())),
                                                 preferred_element_type=F32)

    below = 0
    for cap in caps:
        fits = (cmax <= cap) if below == 0 else ((cmax > below) & (cmax <= cap))
        pl.when(fits)(functools.partial(do_round, jnp.int32(0), cap))
        below = cap

    @pl.when(cmax > caps[-1])
    def _many_rounds():
        def body(r, carry):
            do_round(r, caps[-1])
            return carry
        lax.fori_loop(0, (cmax + (caps[-1] - 1)) // caps[-1], body, 0)

    @pl.when(pair == pl.num_programs(2) - 1)
    def _finish():
        o_ref[0] = y0_ref[0] + mod_ref[0][5:6] * o_ref[0]


def _moe_call(counts, order, fx, gt, pt, wg, wu, wd, y0, mod, tt, sub, caps, epg):
    b, l, d = fx.shape
    ne, _, de = wg.shape
    assert l % tt == 0 and tt % sub == 0 and ne % epg == 0
    assert all(c % 16 == 0 for c in caps) and list(caps) == sorted(caps)
    tokd = pl.BlockSpec((1, tt, d), lambda bi, i, p, cnt, order: (bi, i, 0))
    expt = pl.BlockSpec((1, ne, tt), lambda bi, i, p, cnt, order: (bi, 0, i))

    def expert_block(shape, k):
        return pl.BlockSpec((1,) + shape,
                            lambda bi, i, p, cnt, order: (order[bi * ne + p * epg + k], 0, 0))

    kern = functools.partial(_moe_kernel, sub=sub, caps=tuple(caps), epg=epg)
    grid_spec = pltpu.PrefetchScalarGridSpec(
        num_scalar_prefetch=2,
        grid=(b, l // tt, ne // epg),
        in_specs=([tokd, expt, expt]
                  + [expert_block((d, de), k) for k in range(epg)]
                  + [expert_block((d, de), k) for k in range(epg)]
                  + [expert_block((de, d), k) for k in range(epg)]
                  + [tokd, pl.BlockSpec((1, 6, d), lambda bi, i, p, cnt, order: (bi, 0, 0))]),
        out_specs=tokd)
    return pl.pallas_call(
        kern,
        out_shape=jax.ShapeDtypeStruct((b, l, d), F32),
        grid_spec=grid_spec,
        compiler_params=pltpu.CompilerParams(
            dimension_semantics=("arbitrary", "arbitrary", "arbitrary"),
            vmem_limit_bytes=VMEM_LIMIT),
        name="moe",
    )(counts, order, fx, gt, pt, *([wg] * epg), *([wu] * epg), *([wd] * epg), y0, mod)


def _rope_tables(n_tokens):
    rows = n_tokens // GRID_W
    row = jnp.repeat(jnp.arange(rows, dtype=F32), GRID_W)
    col = jnp.tile(jnp.arange(GRID_W, dtype=F32), rows)
    half = DA_HEAD_DIM // 2
    inv_freq = ROPE_THETA ** (-jnp.arange(0, half, 2, dtype=F32) / half)
    ang = jnp.concatenate([row[:, None] * inv_freq, col[:, None] * inv_freq], axis=-1)
    cos, sin = jnp.cos(ang), jnp.sin(ang)
    cos64 = jnp.repeat(cos, 2, axis=-1)
    sin64 = jnp.stack([-sin, sin], axis=-1).reshape(n_tokens, DA_HEAD_DIM)
    return jnp.tile(cos64, (1, 2)), jnp.tile(sin64, (1, 2))


def kernel(x, c, ctx, c_ctx, w_ada, b_ada, norm_mix_g, w_in, q_norm_g, k_norm_g, da_lambda, subln_g, gm_ln_g, gm_ln_b, gm_ws, gm_bs, gm_out_g, w_out, norm_ffn_g, w_router, router_bias, we_gate, we_up, we_down, ws_gate, ws_up, ws_down):
    assert w_ada.shape[0] == 1, "single-layer kernel"
    b, l, d = x.shape
    lambda_init = 0.8 - 0.6 * math.exp(-0.3 * 0)
    lp = da_lambda[0].astype(F32)
    lam = (jnp.exp(jnp.sum(lp[0] * lp[1])) - jnp.exp(jnp.sum(lp[2] * lp[3])) + lambda_init).reshape(1)

    assert b + 1 <= COND_ROWS
    cond = jnp.zeros((COND_ROWS, d), F32).at[:b].set(c).at[b].set(c_ctx)
    ada = _ada_call(cond, w_ada[0], b_ada[0][None, :]).reshape(COND_ROWS, 6, d)
    mod = ada[:b]
    mod_ctx = ada[b:b + 1]

    cos, sin = _rope_tables(l)
    half = jnp.arange(LANES) // DA_HEAD_DIM
    gmat = (half[:, None] == half[None, :]).astype(BF)
    qg = jnp.tile(q_norm_g[0], 2)[None, :]
    kg = jnp.tile(k_norm_g[0], 2)[None, :]
    w_in_bf = w_in[0].astype(BF)
    bs_full = jnp.broadcast_to(gm_bs[0][:, :, None], (GM_HEADS, CHUNK, GM_HEAD_DIM))

    tm = min(ROUTE_TILE, l)
    assert l % GRID_W == 0 and l % tm == 0 and l % min(INPROJ_TILE, l) == 0 and tm % CHUNK == 0
    q, k, v, gm = _inproj_call(
        x, mod, norm_mix_g, w_in_bf, qg, kg, cos, sin, gmat,
        gm_ln_g[0].reshape(1, GM_WIDTH), gm_ln_b[0].reshape(1, GM_WIDTH),
        gm_ws[0].astype(BF), bs_full, gm_out_g[0].reshape(1, GM_WIDTH), min(INPROJ_TILE, l))
    kc, vc = _ctxproj_call(ctx, mod_ctx, norm_mix_g, w_in_bf[:, DA_WIDTH:3 * DA_WIDTH], kg, gmat)

    experts = (we_gate[0], we_up[0], we_down[0])
    fused_cast = N_EXPERTS % (b * DA_HEADS) == 0
    attn, *experts_bf = _attn_call(lam, q, k, kc, v, vc, subln_g[0][:, None],
                                   tq=min(ATTN_Q_TILE, l), ck=min(ATTN_KEY_CHUNK, ctx.shape[1]),
                                   out_scale=1.0 - lambda_init, cast=experts if fused_cast else ())
    if not fused_cast:
        experts_bf = [w.astype(BF) for w in experts]

    wr = w_router[0].T.astype(BF)
    rb = router_bias[0][:, None]
    tok = jnp.arange(tm)
    tri = (tok[:, None] < tok[None, :]).astype(BF)
    y0, fx, gt, pt, counts = _outproj_call(
        x, attn, gm, mod, w_out[0].astype(BF), norm_ffn_g, wr, rb,
        ws_gate[0].astype(BF), ws_up[0].astype(BF), ws_down[0].astype(BF), tri, tm)

    totals = jnp.sum(counts.reshape(b, -1, N_EXPERTS), axis=1)
    order = jnp.argsort(-totals, axis=1).astype(jnp.int32).reshape(-1)
    return _moe_call(counts.reshape(-1), order, fx, gt, pt, *experts_bf,
                     y0, mod, tt=min(MOE_TILE, l), sub=tm, caps=MOE_CAPS, epg=MOE_EXPERTS_PER_STEP)
```

```python
import functools
import math

import jax
import jax.numpy as jnp
from jax import lax
from jax.experimental import pallas as pl
from jax.experimental.pallas import tpu as pltpu

BF = jnp.bfloat16
F32 = jnp.float32

EPS = 1e-6
GRID_W = 64
DA_HEADS = 4
DA_HEAD_DIM = 64
DA_V_DIM = 128
DA_WIDTH = 512
GM_HEADS = 4
GM_HEAD_DIM = 128
GM_WIDTH = 512
CHUNK = 128
ROPE_THETA = 10000.0
Q_SCALE = DA_HEAD_DIM ** -0.5 * math.log2(math.e)
N_EXPERTS = 32
TOP_K = 4
N_GROUPS = 4
TOPK_GROUPS = 2
GROUP_SIZE = N_EXPERTS // N_GROUPS
ROUTED_SCALE = 2.5
ATTN_Q_TILE = 256
ATTN_KEY_CHUNK = 256
INPROJ_TILE = 512
ROUTE_TILE = 512
MOE_TILE = 2048
MOE_CAPS = (48, 64, 80, 96, 128, 192, 256)
MOE_EXPERTS_PER_STEP = 2
ADA_BLOCK = 1024
COND_ROWS = 16
LANES = 128
VMEM_LIMIT = 56 * 1024 * 1024


def _sigmoid(x):
    return 1.0 / (1.0 + jnp.exp(-x))


def _dot(a, b):
    return jnp.dot(a, b, preferred_element_type=F32)


def _dot_nt(a, b):
    return lax.dot_general(a, b, (((1,), (1,)), ((), ())), preferred_element_type=F32)


def _rms_rows(x, g):
    ms = jnp.mean(x * x, axis=-1, keepdims=True)
    return x * lax.rsqrt(ms + EPS) * g


def _group_mean_sq(y, gmat):
    y2 = y * y
    hi = y2.astype(BF)
    lo = (y2 - hi.astype(F32)).astype(BF)
    return (_dot(hi, gmat) + _dot(lo, gmat)) * (1.0 / DA_HEAD_DIM)


def _swap_pairs(y):
    lane = lax.broadcasted_iota(jnp.int32, y.shape, 1)
    nxt = pltpu.roll(y, LANES - 1, 1)
    prv = pltpu.roll(y, 1, 1)
    return jnp.where((lane & 1) == 0, nxt, prv)


def _ada_kernel(cond_ref, w_ref, b_ref, o_ref):
    c = cond_ref[...]
    s = c * _sigmoid(c)
    o_ref[...] = _dot(s.astype(BF), w_ref[...].astype(BF)) + b_ref[...]


def _ada_call(cond, w_ada, b_ada):
    rows, d = cond.shape
    n = w_ada.shape[1]
    bn = ADA_BLOCK
    return pl.pallas_call(
        _ada_kernel,
        out_shape=jax.ShapeDtypeStruct((rows, n), F32),
        grid=(n // bn,),
        in_specs=[pl.BlockSpec((rows, d), lambda j: (0, 0)),
                  pl.BlockSpec((d, bn), lambda j: (0, j)),
                  pl.BlockSpec((1, bn), lambda j: (0, j))],
        out_specs=pl.BlockSpec((rows, bn), lambda j: (0, j)),
        compiler_params=pltpu.CompilerParams(dimension_semantics=("arbitrary",),
                                             vmem_limit_bytes=VMEM_LIMIT),
        name="ada",
    )(cond, w_ada, b_ada)


def _inproj_kernel(x_ref, mod_ref, ng_ref, w_ref, qg_ref, kg_ref, cos_ref, sin_ref, gmat_ref,
                   lng_ref, lnb_ref, ws_ref, bs_ref, og_ref,
                   q_ref, k_ref, v_ref, gm_ref):
    tm = x_ref.shape[1]
    x = x_ref[0]
    mod = mod_ref[0]
    h = (_rms_rows(x, ng_ref[...]) * (1.0 + mod[1:2]) + mod[0:1]).astype(BF)

    def proj(col0, width=2 * LANES):
        return _dot(h, w_ref[:, col0:col0 + width])

    def gelu(t):
        return 0.5 * t * (1.0 + lax.erf(t * math.sqrt(0.5)))

    gmat = gmat_ref[...]
    cos = cos_ref[...]
    sin = sin_ref[...]
    for jb in range(DA_HEADS // 2):
        pq = proj(jb * 2 * LANES)
        pk = proj(DA_WIDTH + jb * 2 * LANES)
        for jj in range(2):
            sl = slice((2 * jb + jj) * LANES, (2 * jb + jj + 1) * LANES)
            qj = pq[:, jj * LANES:(jj + 1) * LANES]
            qn = qj * lax.rsqrt(_group_mean_sq(qj, gmat) + EPS) * qg_ref[...]
            qr = qn * cos + _swap_pairs(qn) * sin
            q_ref[0, :, sl] = (qr * Q_SCALE).astype(BF)
            kj = pk[:, jj * LANES:(jj + 1) * LANES]
            kn = kj * lax.rsqrt(_group_mean_sq(kj, gmat) + EPS) * kg_ref[...]
            kr = kn * cos + _swap_pairs(kn) * sin
            k_ref[0, :, sl] = kr.astype(BF)
    v_ref[0] = proj(2 * DA_WIDTH, DA_WIDTH).astype(BF)

    for g in range(GM_HEADS):
        sl = slice(g * LANES, (g + 1) * LANES)
        if g % 2 == 0:
            zu = gelu(proj(3 * DA_WIDTH + g * LANES))
            zv = gelu(proj(3 * DA_WIDTH + GM_WIDTH + g * LANES))
        u = zu[:, (g % 2) * LANES:(g % 2 + 1) * LANES]
        vg = zv[:, (g % 2) * LANES:(g % 2 + 1) * LANES]
        mu = jnp.mean(vg, axis=-1, keepdims=True)
        xc = vg - mu
        var = jnp.mean(xc * xc, axis=-1, keepdims=True)
        vn = (xc * lax.rsqrt(var + EPS) * lng_ref[:, sl] + lnb_ref[:, sl]).astype(BF)
        for cidx in range(tm // CHUNK):
            rows = slice(cidx * CHUNK, (cidx + 1) * CHUNK)
            mixed = _dot(ws_ref[g], vn[rows]) + bs_ref[g]
            y = u[rows] * mixed
            gm_ref[0, rows, sl] = _rms_rows(y, og_ref[:, sl]).astype(BF)


def _inproj_call(x, mod, ng, w_in, qg, kg, cos, sin, gmat, lng, lnb, ws, bs, og, tm):
    b, l, d = x.shape
    nw = w_in.shape[1]
    full = lambda shape: pl.BlockSpec(shape, lambda bi, i: (0,) * len(shape))
    tok = pl.BlockSpec((1, tm, DA_WIDTH), lambda bi, i: (bi, i, 0))
    out = jax.ShapeDtypeStruct((b, l, DA_WIDTH), BF)
    return pl.pallas_call(
        _inproj_kernel,
        out_shape=(out, out, out, out),
        grid=(b, l // tm),
        in_specs=[pl.BlockSpec((1, tm, d), lambda bi, i: (bi, i, 0)),
                  pl.BlockSpec((1, 6, d), lambda bi, i: (bi, 0, 0)),
                  full((1, d)), full((d, nw)), full((1, LANES)), full((1, LANES)),
                  pl.BlockSpec((tm, LANES), lambda bi, i: (i, 0)),
                  pl.BlockSpec((tm, LANES), lambda bi, i: (i, 0)),
                  full((LANES, LANES)), full((1, GM_WIDTH)), full((1, GM_WIDTH)),
                  full((GM_HEADS, CHUNK, CHUNK)), full((GM_HEADS, CHUNK, GM_HEAD_DIM)),
                  full((1, GM_WIDTH))],
        out_specs=(tok, tok, tok, tok),
        compiler_params=pltpu.CompilerParams(dimension_semantics=("arbitrary", "arbitrary"),
                                             vmem_limit_bytes=VMEM_LIMIT),
        name="inproj",
    )(x, mod, ng, w_in, qg, kg, cos, sin, gmat, lng, lnb, ws, bs, og)


def _ctxproj_kernel(x_ref, mod_ref, ng_ref, w_ref, kg_ref, gmat_ref, k_ref, v_ref):
    x = x_ref[0]
    mod = mod_ref[0]
    h = _rms_rows(x, ng_ref[...]) * (1.0 + mod[1:2]) + mod[0:1]
    p = _dot(h.astype(BF), w_ref[...])
    gmat = gmat_ref[...]
    for j in range(DA_HEADS):
        sl = slice(j * LANES, (j + 1) * LANES)
        kj = p[:, sl]
        kn = kj * lax.rsqrt(_group_mean_sq(kj, gmat) + EPS) * kg_ref[...]
        k_ref[0, :, sl] = kn.astype(BF)
    v_ref[0] = p[:, DA_WIDTH:].astype(BF)


def _ctxproj_call(ctx, mod_ctx, ng, w_kv, kg, gmat):
    b, lc, d = ctx.shape
    full = lambda shape: pl.BlockSpec(shape, lambda bi: (0,) * len(shape))
    tok = pl.BlockSpec((1, lc, DA_WIDTH), lambda bi: (bi, 0, 0))
    out = jax.ShapeDtypeStruct((b, lc, DA_WIDTH), BF)
    return pl.pallas_call(
        _ctxproj_kernel,
        out_shape=(out, out),
        grid=(b,),
        in_specs=[pl.BlockSpec((1, lc, d), lambda bi: (bi, 0, 0)),
                  full((1, 6, d)), full((1, d)), full((d, 2 * DA_WIDTH)),
                  full((1, LANES)), full((LANES, LANES))],
        out_specs=(tok, tok),
        compiler_params=pltpu.CompilerParams(dimension_semantics=("arbitrary",),
                                             vmem_limit_bytes=VMEM_LIMIT),
        name="ctxproj",
    )(ctx, mod_ctx, ng, w_kv, kg, gmat)


def _attn_kernel(lam_ref, q_ref, kx_ref, kc_ref, vx_ref, vc_ref, sg_ref, o_ref,
                 k_scr, vt_scr, s_scr, m_scr, *, ck, out_scale):
    tq = q_ref.shape[1]
    lc = kc_ref.shape[1]
    lx = kx_ref.shape[1]
    nck = (lc + lx) // ck

    step = pl.program_id(2)
    last = pl.num_programs(2) - 1

    def stacked_queries():
        q = q_ref[0]
        lane = lax.broadcasted_iota(jnp.int32, q.shape, 1)
        zero = jnp.zeros_like(q)
        return jnp.concatenate([jnp.where(lane < DA_HEAD_DIM, q, zero),
                                jnp.where(lane >= DA_HEAD_DIM, q, zero)], axis=0)

    def score_chunk(qq, cidx, mrun):
        rows = slice(cidx * ck, (cidx + 1) * ck)
        st = _dot_nt(k_scr[rows, :], qq)
        s_scr[rows, :] = st
        return jnp.maximum(mrun, jnp.max(st.reshape(ck // 8, 8, 2 * tq), axis=0))

    def weight_chunk(m, cidx, acc, lrun):
        rows = slice(cidx * ck, (cidx + 1) * ck)
        pt = jnp.exp2(s_scr[rows, :] - m)
        acc = acc + _dot(vt_scr[cidx], pt.astype(BF))
        return acc, lrun + jnp.sum(pt.reshape(ck // 8, 8, 2 * tq), axis=0)

    def finish(acc, lrun):
        r = 1.0 / jnp.sum(lrun, axis=0, keepdims=True)
        ot = acc[:, :tq] * r[:, :tq] - lam_ref[0] * (acc[:, tq:] * r[:, tq:])
        ms = jnp.mean(ot * ot, axis=0, keepdims=True)
        on = ot * lax.rsqrt(ms + EPS) * sg_ref[...] * out_scale
        o_ref[0] = on.T.astype(BF)

    mrun0 = jnp.full((8, 2 * tq), -jnp.inf, F32)
    lrun0 = jnp.zeros((8, 2 * tq), F32)
    acc0 = jnp.zeros((DA_V_DIM, 2 * tq), F32)

    @pl.when(step == 0)
    def _first():
        k_scr[0:lc, :] = kc_ref[0]
        k_scr[lc:lc + lx, :] = kx_ref[0]
        for cidx in range(nck):
            lo = cidx * ck
            if lo < lc:
                blk = vc_ref[0, lo:lo + ck, :]
            else:
                blk = vx_ref[0, lo - lc:lo - lc + ck, :]
            vt_scr[cidx] = blk.astype(F32).T.astype(BF)
        qq = stacked_queries()
        mrun = mrun0
        for cidx in range(nck):
            mrun = score_chunk(qq, cidx, mrun)
        m_scr[...] = mrun

    @pl.when((step > 0) & (step < last))
    def _steady():
        qq = stacked_queries()
        m = jnp.max(m_scr[...], axis=0, keepdims=True)
        mrun, lrun, acc = mrun0, lrun0, acc0
        for cidx in range(nck):
            acc, lrun = weight_chunk(m, cidx, acc, lrun)
            mrun = score_chunk(qq, cidx, mrun)
        finish(acc, lrun)
        m_scr[...] = mrun

    @pl.when(step == last)
    def _last():
        m = jnp.max(m_scr[...], axis=0, keepdims=True)
        lrun, acc = lrun0, acc0
        for cidx in range(nck):
            acc, lrun = weight_chunk(m, cidx, acc, lrun)
        finish(acc, lrun)


def _attn_call(lam, q, kx, kc, vx, vc, subln_col, tq, ck, out_scale):
    b, l, _ = q.shape
    lc = kc.shape[1]
    lk = lc + l
    assert lk % ck == 0 and lc % ck == 0 and l % tq == 0
    nq = l // tq
    kern = functools.partial(_attn_kernel, ck=ck, out_scale=out_scale)
    return pl.pallas_call(
        kern,
        out_shape=jax.ShapeDtypeStruct((b, l, DA_WIDTH), BF),
        grid=(b, DA_HEADS, nq + 1),
        in_specs=[pl.BlockSpec(memory_space=pltpu.SMEM),
                  pl.BlockSpec((1, tq, LANES), lambda bi, h, i: (bi, jnp.minimum(i, nq - 1), h)),
                  pl.BlockSpec((1, l, LANES), lambda bi, h, i: (bi, 0, h)),
                  pl.BlockSpec((1, lc, LANES), lambda bi, h, i: (bi, 0, h)),
                  pl.BlockSpec((1, l, LANES), lambda bi, h, i: (bi, 0, h)),
                  pl.BlockSpec((1, lc, LANES), lambda bi, h, i: (bi, 0, h)),
                  pl.BlockSpec((DA_V_DIM, 1), lambda bi, h, i: (0, 0))],
        out_specs=pl.BlockSpec((1, tq, LANES), lambda bi, h, i: (bi, jnp.maximum(i - 1, 0), h)),
        scratch_shapes=[pltpu.VMEM((lk, LANES), BF),
                        pltpu.VMEM((lk // ck, DA_V_DIM, ck), BF),
                        pltpu.VMEM((lk, 2 * tq), F32),
                        pltpu.VMEM((8, 2 * tq), F32)],
        compiler_params=pltpu.CompilerParams(
            dimension_semantics=("arbitrary", "arbitrary", "arbitrary"),
            vmem_limit_bytes=VMEM_LIMIT),
        name="attn",
    )(lam, q, kx, kc, vx, vc, subln_col)


def _route(scores, rbias):
    tm = scores.shape[1]
    assert GROUP_SIZE == 8 and scores.shape[0] == N_EXPERTS
    row = lax.broadcasted_iota(jnp.int32, (GROUP_SIZE, tm), 0).astype(F32)
    neg = jnp.full((GROUP_SIZE, tm), -jnp.inf, F32)
    biased = scores + rbias
    groups = [slice(g * GROUP_SIZE, (g + 1) * GROUP_SIZE) for g in range(N_GROUPS)]
    vals = [biased[sl] for sl in groups]

    gscore = []
    for v in vals:
        m1 = jnp.max(v, axis=0, keepdims=True)
        i1 = jnp.min(jnp.where(v == m1, row, 1e9), axis=0, keepdims=True)
        m2 = jnp.max(jnp.where(row == i1, neg, v), axis=0, keepdims=True)
        gscore.append(m1 + m2)

    cur = []
    for g in range(N_GROUPS):
        beaten = jnp.zeros_like(gscore[g])
        for g2 in range(N_GROUPS):
            if g2 == g:
                continue
            beat = (gscore[g2] >= gscore[g]) if g2 < g else (gscore[g2] > gscore[g])
            beaten = beaten + jnp.where(beat, 1.0, 0.0)
        cur.append(jnp.where(beaten < TOPK_GROUPS, vals[g], neg))

    ids = [row + float(g * GROUP_SIZE) for g in range(N_GROUPS)]
    sel = [jnp.zeros((GROUP_SIZE, tm), F32) for _ in range(N_GROUPS)]
    for _ in range(TOP_K):
        best = functools.reduce(jnp.maximum, cur)
        mx = jnp.max(best, axis=0, keepdims=True)
        cand = functools.reduce(jnp.minimum, [jnp.where(c == mx, i, 1e9) for c, i in zip(cur, ids)])
        idx = jnp.min(cand, axis=0, keepdims=True)
        hits = [i == idx for i in ids]
        sel = [jnp.where(h, 1.0, s) for h, s in zip(hits, sel)]
        cur = [jnp.where(h, neg, c) for h, c in zip(hits, cur)]
    w = [s * scores[sl] for s, sl in zip(sel, groups)]
    total = jnp.sum(functools.reduce(lambda a, b: a + b, w), axis=0, keepdims=True)
    gates = [wg / total * ROUTED_SCALE for wg in w]
    return jnp.concatenate(gates, axis=0), jnp.concatenate(sel, axis=0)


def _outproj_kernel(x_ref, at_ref, gm_ref, mod_ref, wo_ref, nfg_ref, wr_ref, rb_ref,
                    wsg_ref, wsu_ref, wsd_ref, tri_ref,
                    y0_ref, fx_ref, gt_ref, pt_ref, cnt_ref):
    mod = mod_ref[0]
    mix = _dot(at_ref[0], wo_ref[0:DA_WIDTH, :]) + _dot(gm_ref[0], wo_ref[DA_WIDTH:, :])
    x1 = x_ref[0] + mod[2:3] * mix
    fx = (_rms_rows(x1, nfg_ref[...]) * (1.0 + mod[4:5]) + mod[3:4]).astype(BF)
    fx_ref[0] = fx
    scores = _sigmoid(_dot_nt(wr_ref[...], fx))
    gates, sel = _route(scores, rb_ref[...])
    rank = _dot(sel.astype(BF), tri_ref[...])
    gt_ref[0] = gates
    pt_ref[0] = jnp.where(sel > 0.5, rank, -1.0)
    cnt_ref[0] = jnp.sum(sel, axis=1, keepdims=True).astype(jnp.int32)
    sg = _dot(fx, wsg_ref[...])
    su = _dot(fx, wsu_ref[...])
    hs = (sg * _sigmoid(sg)) * su
    y0_ref[0] = x1 + mod[5:6] * _dot(hs.astype(BF), wsd_ref[...])


def _outproj_call(x, attn, gm, mod, w_out, nfg, wr, rb, wsg, wsu, wsd, tri, tm):
    b, l, d = x.shape
    nt = l // tm
    full = lambda shape: pl.BlockSpec(shape, lambda bi, i: (0,) * len(shape))
    tokd = pl.BlockSpec((1, tm, d), lambda bi, i: (bi, i, 0))
    tokh = pl.BlockSpec((1, tm, DA_WIDTH), lambda bi, i: (bi, i, 0))
    expt = pl.BlockSpec((1, N_EXPERTS, tm), lambda bi, i: (bi, 0, i))
    ds = wsg.shape[1]
    return pl.pallas_call(
        _outproj_kernel,
        out_shape=(jax.ShapeDtypeStruct((b, l, d), F32),
                   jax.ShapeDtypeStruct((b, l, d), BF),
                   jax.ShapeDtypeStruct((b, N_EXPERTS, l), F32),
                   jax.ShapeDtypeStruct((b, N_EXPERTS, l), F32),
                   jax.ShapeDtypeStruct((b * nt, N_EXPERTS, 1), jnp.int32)),
        grid=(b, nt),
        in_specs=[tokd, tokh, tokh,
                  pl.BlockSpec((1, 6, d), lambda bi, i: (bi, 0, 0)),
                  full((d, d)), full((1, d)), full((N_EXPERTS, d)), full((N_EXPERTS, 1)),
                  full((d, ds)), full((d, ds)), full((ds, d)), full((tm, tm))],
        out_specs=(tokd, tokd, expt, expt,
                   pl.BlockSpec((1, N_EXPERTS, 1), lambda bi, i: (bi * nt + i, 0, 0))),
        compiler_params=pltpu.CompilerParams(dimension_semantics=("arbitrary", "arbitrary"),
                                             vmem_limit_bytes=VMEM_LIMIT),
        name="outproj",
    )(x, attn, gm, mod, w_out, nfg, wr, rb, wsg, wsu, wsd, tri)


def _moe_kernel(cnt_ref, order_ref, fx_ref, gt_ref, pt_ref, *refs, sub, caps, epg):
    wg_ref, wu_ref, wd_ref = refs[0:epg], refs[epg:2 * epg], refs[2 * epg:3 * epg]
    y0_ref, mod_ref, o_ref = refs[3 * epg:]
    tt = fx_ref.shape[1]
    nsub = tt // sub
    pair = pl.program_id(2)
    experts = [order_ref[pl.program_id(0) * N_EXPERTS + pair * epg + k] for k in range(epg)]
    sub0 = (pl.program_id(0) * pl.num_programs(1) + pl.program_id(1)) * nsub

    @pl.when(pair == 0)
    def _zero():
        o_ref[...] = jnp.zeros_like(o_ref)

    cmax = jnp.int32(0)
    for s in range(nsub):
        for k in range(epg):
            cmax = jnp.maximum(cmax, cnt_ref[(sub0 + s) * N_EXPERTS + experts[k]])

    def do_round(r, cap):
        slot = lax.broadcasted_iota(jnp.int32, (cap, sub), 0).astype(F32) + (r * cap).astype(F32)
        picks, xrows, grows = [], [], []
        for s in range(nsub):
            cols = slice(s * sub, (s + 1) * sub)
            hit = [pt_ref[0, pl.ds(experts[k], 1), cols] == slot for k in range(epg)]
            grows.append([jnp.sum(jnp.where(hit[k], gt_ref[0, pl.ds(experts[k], 1), cols], 0.0),
                                  axis=-1, keepdims=True) for k in range(epg)])
            pick = jnp.concatenate([jnp.where(h, 1.0, 0.0) for h in hit], axis=0).astype(BF)
            picks.append(pick)
            xrows.append(_dot(pick, fx_ref[0, cols, :]).astype(BF))
        outs = []
        for k in range(epg):
            xk = jnp.concatenate([xrows[s][k * cap:(k + 1) * cap] for s in range(nsub)], axis=0)
            gk = jnp.concatenate([grows[s][k] for s in range(nsub)], axis=0)
            a = _dot(xk, wg_ref[k][0])
            bb = _dot(xk, wu_ref[k][0])
            hm = (a * _sigmoid(a)) * bb * gk
            outs.append(_dot(hm.astype(BF), wd_ref[k][0]).astype(BF))
        for s in range(nsub):
            cols = slice(s * sub, (s + 1) * sub)
            stacked = jnp.concatenate([outs[k][s * cap:(s + 1) * cap] for k in range(epg)], axis=0)
            o_ref[0, cols, :] += lax.dot_general(picks[s], stacked, (((0,), (0,)), ((), ())),
                                                 preferred_element_type=F32)

    below = 0
    for cap in caps:
        fits = (cmax <= cap) if below == 0 else ((cmax > below) & (cmax <= cap))
        pl.when(fits)(functools.partial(do_round, jnp.int32(0), cap))
        below = cap

    @pl.when(cmax > caps[-1])
    def _many_rounds():
        def body(r, carry):
            do_round(r, caps[-1])
            return carry
        lax.fori_loop(0, (cmax + (caps[-1] - 1)) // caps[-1], body, 0)

    @pl.when(pair == pl.num_programs(2) - 1)
    def _finish():
        o_ref[0] = y0_ref[0] + mod_ref[0][5:6] * o_ref[0]


def _moe_call(counts, order, fx, gt, pt, wg, wu, wd, y0, mod, tt, sub, caps, epg):
    b, l, d = fx.shape
    ne, _, de = wg.shape
    assert l % tt == 0 and tt % sub == 0 and ne % epg == 0
    assert all(c % 16 == 0 for c in caps) and list(caps) == sorted(caps)
    tokd = pl.BlockSpec((1, tt, d), lambda bi, i, p, cnt, order: (bi, i, 0))
    expt = pl.BlockSpec((1, ne, tt), lambda bi, i, p, cnt, order: (bi, 0, i))

    def expert_block(shape, k):
        return pl.BlockSpec((1,) + shape,
                            lambda bi, i, p, cnt, order: (order[bi * ne + p * epg + k], 0, 0))

    kern = functools.partial(_moe_kernel, sub=sub, caps=tuple(caps), epg=epg)
    grid_spec = pltpu.PrefetchScalarGridSpec(
        num_scalar_prefetch=2,
        grid=(b, l // tt, ne // epg),
        in_specs=([tokd, expt, expt]
                  + [expert_block((d, de), k) for k in range(epg)]
                  + [expert_block((d, de), k) for k in range(epg)]
                  + [expert_block((de, d), k) for k in range(epg)]
                  + [tokd, pl.BlockSpec((1, 6, d), lambda bi, i, p, cnt, order: (bi, 0, 0))]),
        out_specs=tokd)
    return pl.pallas_call(
        kern,
        out_shape=jax.ShapeDtypeStruct((b, l, d), F32),
        grid_spec=grid_spec,
        compiler_params=pltpu.CompilerParams(
            dimension_semantics=("arbitrary", "arbitrary", "arbitrary"),
            vmem_limit_bytes=VMEM_LIMIT),
        name="moe",
    )(counts, order, fx, gt, pt, *([wg] * epg), *([wu] * epg), *([wd] * epg), y0, mod)


def _rope_tables(n_tokens):
    rows = n_tokens // GRID_W
    row = jnp.repeat(jnp.arange(rows, dtype=F32), GRID_W)
    col = jnp.tile(jnp.arange(GRID_W, dtype=F32), rows)
    half = DA_HEAD_DIM // 2
    inv_freq = ROPE_THETA ** (-jnp.arange(0, half, 2, dtype=F32) / half)
    ang = jnp.concatenate([row[:, None] * inv_freq, col[:, None] * inv_freq], axis=-1)
    cos, sin = jnp.cos(ang), jnp.sin(ang)
    cos64 = jnp.repeat(cos, 2, axis=-1)
    sin64 = jnp.stack([-sin, sin], axis=-1).reshape(n_tokens, DA_HEAD_DIM)
    return jnp.tile(cos64, (1, 2)), jnp.tile(sin64, (1, 2))


def kernel(x, c, ctx, c_ctx, w_ada, b_ada, norm_mix_g, w_in, q_norm_g, k_norm_g, da_lambda, subln_g, gm_ln_g, gm_ln_b, gm_ws, gm_bs, gm_out_g, w_out, norm_ffn_g, w_router, router_bias, we_gate, we_up, we_down, ws_gate, ws_up, ws_down):
    assert w_ada.shape[0] == 1, "single-layer kernel"
    b, l, d = x.shape
    lambda_init = 0.8 - 0.6 * math.exp(-0.3 * 0)
    lp = da_lambda[0].astype(F32)
    lam = (jnp.exp(jnp.sum(lp[0] * lp[1])) - jnp.exp(jnp.sum(lp[2] * lp[3])) + lambda_init).reshape(1)

    assert b + 1 <= COND_ROWS
    cond = jnp.zeros((COND_ROWS, d), F32).at[:b].set(c).at[b].set(c_ctx)
    ada = _ada_call(cond, w_ada[0], b_ada[0][None, :]).reshape(COND_ROWS, 6, d)
    mod = ada[:b]
    mod_ctx = ada[b:b + 1]

    cos, sin = _rope_tables(l)
    half = jnp.arange(LANES) // DA_HEAD_DIM
    gmat = (half[:, None] == half[None, :]).astype(BF)
    qg = jnp.tile(q_norm_g[0], 2)[None, :]
    kg = jnp.tile(k_norm_g[0], 2)[None, :]
    w_in_bf = w_in[0].astype(BF)
    bs_full = jnp.broadcast_to(gm_bs[0][:, :, None], (GM_HEADS, CHUNK, GM_HEAD_DIM))

    tm = min(ROUTE_TILE, l)
    assert l % GRID_W == 0 and l % tm == 0 and l % min(INPROJ_TILE, l) == 0 and tm % CHUNK == 0
    q, k, v, gm = _inproj_call(
        x, mod, norm_mix_g, w_in_bf, qg, kg, cos, sin, gmat,
        gm_ln_g[0].reshape(1, GM_WIDTH), gm_ln_b[0].reshape(1, GM_WIDTH),
        gm_ws[0].astype(BF), bs_full, gm_out_g[0].reshape(1, GM_WIDTH), min(INPROJ_TILE, l))
    kc, vc = _ctxproj_call(ctx, mod_ctx, norm_mix_g, w_in_bf[:, DA_WIDTH:3 * DA_WIDTH], kg, gmat)

    attn = _attn_call(lam, q, k, kc, v, vc, subln_g[0][:, None],
                      tq=min(ATTN_Q_TILE, l), ck=min(ATTN_KEY_CHUNK, ctx.shape[1]),
                      out_scale=1.0 - lambda_init)

    wr = w_router[0].T.astype(BF)
    rb = router_bias[0][:, None]
    tok = jnp.arange(tm)
    tri = (tok[:, None] < tok[None, :]).astype(BF)
    y0, fx, gt, pt, counts = _outproj_call(
        x, attn, gm, mod, w_out[0].astype(BF), norm_ffn_g, wr, rb,
        ws_gate[0].astype(BF), ws_up[0].astype(BF), ws_down[0].astype(BF), tri, tm)

    totals = jnp.sum(counts.reshape(b, -1, N_EXPERTS), axis=1)
    order = jnp.argsort(-totals, axis=1).astype(jnp.int32).reshape(-1)
    return _moe_call(counts.reshape(-1), order, fx, gt, pt,
                     we_gate[0].astype(BF), we_up[0].astype(BF), we_down[0].astype(BF),
                     y0, mod, tt=min(MOE_TILE, l), sub=tm, caps=MOE_CAPS, epg=MOE_EXPERTS_PER_STEP)
```

```python
import functools
import math

import jax
import jax.numpy as jnp
from jax import lax
from jax.experimental import pallas as pl
from jax.experimental.pallas import tpu as pltpu

BF = jnp.bfloat16
F32 = jnp.float32

EPS = 1e-6
GRID_W = 64
DA_HEADS = 4
DA_HEAD_DIM = 64
DA_V_DIM = 128
DA_WIDTH = 512
GM_HEADS = 4
GM_HEAD_DIM = 128
GM_WIDTH = 512
CHUNK = 128
ROPE_THETA = 10000.0
Q_SCALE = DA_HEAD_DIM ** -0.5 * math.log2(math.e)
N_EXPERTS = 32
TOP_K = 4
N_GROUPS = 4
TOPK_GROUPS = 2
GROUP_SIZE = N_EXPERTS // N_GROUPS
ROUTED_SCALE = 2.5
ATTN_Q_TILE = 256
ATTN_KEY_CHUNK = 256
INPROJ_TILE = 512
ROUTE_TILE = 512
MOE_TILE = 2048
MOE_CAPS = (32, 64, 96, 128, 192, 256)
MOE_EXPERTS_PER_STEP = 2
ADA_BLOCK = 1024
COND_ROWS = 16
LANES = 128
VMEM_LIMIT = 56 * 1024 * 1024


def _sigmoid(x):
    return 1.0 / (1.0 + jnp.exp(-x))


def _dot(a, b):
    return jnp.dot(a, b, preferred_element_type=F32)


def _dot_nt(a, b):
    return lax.dot_general(a, b, (((1,), (1,)), ((), ())), preferred_element_type=F32)


def _rms_rows(x, g):
    ms = jnp.mean(x * x, axis=-1, keepdims=True)
    return x * lax.rsqrt(ms + EPS) * g


def _group_mean_sq(y, gmat):
    y2 = y * y
    hi = y2.astype(BF)
    lo = (y2 - hi.astype(F32)).astype(BF)
    return (_dot(hi, gmat) + _dot(lo, gmat)) * (1.0 / DA_HEAD_DIM)


def _swap_pairs(y):
    lane = lax.broadcasted_iota(jnp.int32, y.shape, 1)
    nxt = pltpu.roll(y, LANES - 1, 1)
    prv = pltpu.roll(y, 1, 1)
    return jnp.where((lane & 1) == 0, nxt, prv)


def _ada_kernel(cond_ref, w_ref, b_ref, o_ref):
    c = cond_ref[...]
    s = c * _sigmoid(c)
    o_ref[...] = _dot(s.astype(BF), w_ref[...].astype(BF)) + b_ref[...]


def _ada_call(cond, w_ada, b_ada):
    rows, d = cond.shape
    n = w_ada.shape[1]
    bn = ADA_BLOCK
    return pl.pallas_call(
        _ada_kernel,
        out_shape=jax.ShapeDtypeStruct((rows, n), F32),
        grid=(n // bn,),
        in_specs=[pl.BlockSpec((rows, d), lambda j: (0, 0)),
                  pl.BlockSpec((d, bn), lambda j: (0, j)),
                  pl.BlockSpec((1, bn), lambda j: (0, j))],
        out_specs=pl.BlockSpec((rows, bn), lambda j: (0, j)),
        compiler_params=pltpu.CompilerParams(dimension_semantics=("arbitrary",),
                                             vmem_limit_bytes=VMEM_LIMIT),
        name="ada",
    )(cond, w_ada, b_ada)


def _inproj_kernel(x_ref, mod_ref, ng_ref, w_ref, qg_ref, kg_ref, cos_ref, sin_ref, gmat_ref,
                   lng_ref, lnb_ref, ws_ref, bs_ref, og_ref,
                   q_ref, k_ref, v_ref, gm_ref):
    tm = x_ref.shape[1]
    x = x_ref[0]
    mod = mod_ref[0]
    h = (_rms_rows(x, ng_ref[...]) * (1.0 + mod[1:2]) + mod[0:1]).astype(BF)

    def proj(col0, width=2 * LANES):
        return _dot(h, w_ref[:, col0:col0 + width])

    def gelu(t):
        return 0.5 * t * (1.0 + lax.erf(t * math.sqrt(0.5)))

    gmat = gmat_ref[...]
    cos = cos_ref[...]
    sin = sin_ref[...]
    for jb in range(DA_HEADS // 2):
        pq = proj(jb * 2 * LANES)
        pk = proj(DA_WIDTH + jb * 2 * LANES)
        for jj in range(2):
            sl = slice((2 * jb + jj) * LANES, (2 * jb + jj + 1) * LANES)
            qj = pq[:, jj * LANES:(jj + 1) * LANES]
            qn = qj * lax.rsqrt(_group_mean_sq(qj, gmat) + EPS) * qg_ref[...]
            qr = qn * cos + _swap_pairs(qn) * sin
            q_ref[0, :, sl] = (qr * Q_SCALE).astype(BF)
            kj = pk[:, jj * LANES:(jj + 1) * LANES]
            kn = kj * lax.rsqrt(_group_mean_sq(kj, gmat) + EPS) * kg_ref[...]
            kr = kn * cos + _swap_pairs(kn) * sin
            k_ref[0, :, sl] = kr.astype(BF)
    v_ref[0] = proj(2 * DA_WIDTH, DA_WIDTH).astype(BF)

    for g in range(GM_HEADS):
        sl = slice(g * LANES, (g + 1) * LANES)
        if g % 2 == 0:
            zu = gelu(proj(3 * DA_WIDTH + g * LANES))
            zv = gelu(proj(3 * DA_WIDTH + GM_WIDTH + g * LANES))
        u = zu[:, (g % 2) * LANES:(g % 2 + 1) * LANES]
        vg = zv[:, (g % 2) * LANES:(g % 2 + 1) * LANES]
        mu = jnp.mean(vg, axis=-1, keepdims=True)
        xc = vg - mu
        var = jnp.mean(xc * xc, axis=-1, keepdims=True)
        vn = (xc * lax.rsqrt(var + EPS) * lng_ref[:, sl] + lnb_ref[:, sl]).astype(BF)
        for cidx in range(tm // CHUNK):
            rows = slice(cidx * CHUNK, (cidx + 1) * CHUNK)
            mixed = _dot(ws_ref[g], vn[rows]) + bs_ref[g]
            y = u[rows] * mixed
            gm_ref[0, rows, sl] = _rms_rows(y, og_ref[:, sl]).astype(BF)


def _inproj_call(x, mod, ng, w_in, qg, kg, cos, sin, gmat, lng, lnb, ws, bs, og, tm):
    b, l, d = x.shape
    nw = w_in.shape[1]
    full = lambda shape: pl.BlockSpec(shape, lambda bi, i: (0,) * len(shape))
    tok = pl.BlockSpec((1, tm, DA_WIDTH), lambda bi, i: (bi, i, 0))
    out = jax.ShapeDtypeStruct((b, l, DA_WIDTH), BF)
    return pl.pallas_call(
        _inproj_kernel,
        out_shape=(out, out, out, out),
        grid=(b, l // tm),
        in_specs=[pl.BlockSpec((1, tm, d), lambda bi, i: (bi, i, 0)),
                  pl.BlockSpec((1, 6, d), lambda bi, i: (bi, 0, 0)),
                  full((1, d)), full((d, nw)), full((1, LANES)), full((1, LANES)),
                  pl.BlockSpec((tm, LANES), lambda bi, i: (i, 0)),
                  pl.BlockSpec((tm, LANES), lambda bi, i: (i, 0)),
                  full((LANES, LANES)), full((1, GM_WIDTH)), full((1, GM_WIDTH)),
                  full((GM_HEADS, CHUNK, CHUNK)), full((GM_HEADS, CHUNK, GM_HEAD_DIM)),
                  full((1, GM_WIDTH))],
        out_specs=(tok, tok, tok, tok),
        compiler_params=pltpu.CompilerParams(dimension_semantics=("arbitrary", "arbitrary"),
                                             vmem_limit_bytes=VMEM_LIMIT),
        name="inproj",
    )(x, mod, ng, w_in, qg, kg, cos, sin, gmat, lng, lnb, ws, bs, og)


def _ctxproj_kernel(x_ref, mod_ref, ng_ref, w_ref, kg_ref, gmat_ref, k_ref, v_ref):
    x = x_ref[0]
    mod = mod_ref[0]
    h = _rms_rows(x, ng_ref[...]) * (1.0 + mod[1:2]) + mod[0:1]
    p = _dot(h.astype(BF), w_ref[...])
    gmat = gmat_ref[...]
    for j in range(DA_HEADS):
        sl = slice(j * LANES, (j + 1) * LANES)
        kj = p[:, sl]
        kn = kj * lax.rsqrt(_group_mean_sq(kj, gmat) + EPS) * kg_ref[...]
        k_ref[0, :, sl] = kn.astype(BF)
    v_ref[0] = p[:, DA_WIDTH:].astype(BF)


def _ctxproj_call(ctx, mod_ctx, ng, w_kv, kg, gmat):
    b, lc, d = ctx.shape
    full = lambda shape: pl.BlockSpec(shape, lambda bi: (0,) * len(shape))
    tok = pl.BlockSpec((1, lc, DA_WIDTH), lambda bi: (bi, 0, 0))
    out = jax.ShapeDtypeStruct((b, lc, DA_WIDTH), BF)
    return pl.pallas_call(
        _ctxproj_kernel,
        out_shape=(out, out),
        grid=(b,),
        in_specs=[pl.BlockSpec((1, lc, d), lambda bi: (bi, 0, 0)),
                  full((1, 6, d)), full((1, d)), full((d, 2 * DA_WIDTH)),
                  full((1, LANES)), full((LANES, LANES))],
        out_specs=(tok, tok),
        compiler_params=pltpu.CompilerParams(dimension_semantics=("arbitrary",),
                                             vmem_limit_bytes=VMEM_LIMIT),
        name="ctxproj",
    )(ctx, mod_ctx, ng, w_kv, kg, gmat)


def _attn_kernel(lam_ref, q_ref, kx_ref, kc_ref, vx_ref, vc_ref, sg_ref, o_ref,
                 k_scr, vt_scr, s_scr, m_scr, *, ck, out_scale):
    tq = q_ref.shape[1]
    lc = kc_ref.shape[1]
    lx = kx_ref.shape[1]
    nck = (lc + lx) // ck

    step = pl.program_id(2)
    last = pl.num_programs(2) - 1

    def stacked_queries():
        q = q_ref[0]
        lane = lax.broadcasted_iota(jnp.int32, q.shape, 1)
        zero = jnp.zeros_like(q)
        return jnp.concatenate([jnp.where(lane < DA_HEAD_DIM, q, zero),
                                jnp.where(lane >= DA_HEAD_DIM, q, zero)], axis=0)

    def score_chunk(qq, cidx, mrun):
        rows = slice(cidx * ck, (cidx + 1) * ck)
        st = _dot_nt(k_scr[rows, :], qq)
        s_scr[rows, :] = st
        return jnp.maximum(mrun, jnp.max(st.reshape(ck // 8, 8, 2 * tq), axis=0))

    def weight_chunk(m, cidx, acc, lrun):
        rows = slice(cidx * ck, (cidx + 1) * ck)
        pt = jnp.exp2(s_scr[rows, :] - m)
        acc = acc + _dot(vt_scr[cidx], pt.astype(BF))
        return acc, lrun + jnp.sum(pt.reshape(ck // 8, 8, 2 * tq), axis=0)

    def finish(acc, lrun):
        r = 1.0 / jnp.sum(lrun, axis=0, keepdims=True)
        ot = acc[:, :tq] * r[:, :tq] - lam_ref[0] * (acc[:, tq:] * r[:, tq:])
        ms = jnp.mean(ot * ot, axis=0, keepdims=True)
        on = ot * lax.rsqrt(ms + EPS) * sg_ref[...] * out_scale
        o_ref[0] = on.T.astype(BF)

    mrun0 = jnp.full((8, 2 * tq), -jnp.inf, F32)
    lrun0 = jnp.zeros((8, 2 * tq), F32)
    acc0 = jnp.zeros((DA_V_DIM, 2 * tq), F32)

    @pl.when(step == 0)
    def _first():
        k_scr[0:lc, :] = kc_ref[0]
        k_scr[lc:lc + lx, :] = kx_ref[0]
        for cidx in range(nck):
            lo = cidx * ck
            if lo < lc:
                blk = vc_ref[0, lo:lo + ck, :]
            else:
                blk = vx_ref[0, lo - lc:lo - lc + ck, :]
            vt_scr[cidx] = blk.astype(F32).T.astype(BF)
        qq = stacked_queries()
        mrun = mrun0
        for cidx in range(nck):
            mrun = score_chunk(qq, cidx, mrun)
        m_scr[...] = mrun

    @pl.when((step > 0) & (step < last))
    def _steady():
        qq = stacked_queries()
        m = jnp.max(m_scr[...], axis=0, keepdims=True)
        mrun, lrun, acc = mrun0, lrun0, acc0
        for cidx in range(nck):
            acc, lrun = weight_chunk(m, cidx, acc, lrun)
            mrun = score_chunk(qq, cidx, mrun)
        finish(acc, lrun)
        m_scr[...] = mrun

    @pl.when(step == last)
    def _last():
        m = jnp.max(m_scr[...], axis=0, keepdims=True)
        lrun, acc = lrun0, acc0
        for cidx in range(nck):
            acc, lrun = weight_chunk(m, cidx, acc, lrun)
        finish(acc, lrun)


def _attn_call(lam, q, kx, kc, vx, vc, subln_col, tq, ck, out_scale):
    b, l, _ = q.shape
    lc = kc.shape[1]
    lk = lc + l
    assert lk % ck == 0 and lc % ck == 0 and l % tq == 0
    nq = l // tq
    kern = functools.partial(_attn_kernel, ck=ck, out_scale=out_scale)
    return pl.pallas_call(
        kern,
        out_shape=jax.ShapeDtypeStruct((b, l, DA_WIDTH), BF),
        grid=(b, DA_HEADS, nq + 1),
        in_specs=[pl.BlockSpec(memory_space=pltpu.SMEM),
                  pl.BlockSpec((1, tq, LANES), lambda bi, h, i: (bi, jnp.minimum(i, nq - 1), h)),
                  pl.BlockSpec((1, l, LANES), lambda bi, h, i: (bi, 0, h)),
                  pl.BlockSpec((1, lc, LANES), lambda bi, h, i: (bi, 0, h)),
                  pl.BlockSpec((1, l, LANES), lambda bi, h, i: (bi, 0, h)),
                  pl.BlockSpec((1, lc, LANES), lambda bi, h, i: (bi, 0, h)),
                  pl.BlockSpec((DA_V_DIM, 1), lambda bi, h, i: (0, 0))],
        out_specs=pl.BlockSpec((1, tq, LANES), lambda bi, h, i: (bi, jnp.maximum(i - 1, 0), h)),
        scratch_shapes=[pltpu.VMEM((lk, LANES), BF),
                        pltpu.VMEM((lk // ck, DA_V_DIM, ck), BF),
                        pltpu.VMEM((lk, 2 * tq), F32),
                        pltpu.VMEM((8, 2 * tq), F32)],
        compiler_params=pltpu.CompilerParams(
            dimension_semantics=("arbitrary", "arbitrary", "arbitrary"),
            vmem_limit_bytes=VMEM_LIMIT),
        name="attn",
    )(lam, q, kx, kc, vx, vc, subln_col)


def _route(scores, rbias):
    tm = scores.shape[1]
    assert GROUP_SIZE == 8 and scores.shape[0] == N_EXPERTS
    row = lax.broadcasted_iota(jnp.int32, (GROUP_SIZE, tm), 0).astype(F32)
    neg = jnp.full((GROUP_SIZE, tm), -jnp.inf, F32)
    biased = scores + rbias
    groups = [slice(g * GROUP_SIZE, (g + 1) * GROUP_SIZE) for g in range(N_GROUPS)]
    vals = [biased[sl] for sl in groups]

    gscore = []
    for v in vals:
        m1 = jnp.max(v, axis=0, keepdims=True)
        i1 = jnp.min(jnp.where(v == m1, row, 1e9), axis=0, keepdims=True)
        m2 = jnp.max(jnp.where(row == i1, neg, v), axis=0, keepdims=True)
        gscore.append(m1 + m2)

    cur = []
    for g in range(N_GROUPS):
        beaten = jnp.zeros_like(gscore[g])
        for g2 in range(N_GROUPS):
            if g2 == g:
                continue
            beat = (gscore[g2] >= gscore[g]) if g2 < g else (gscore[g2] > gscore[g])
            beaten = beaten + jnp.where(beat, 1.0, 0.0)
        cur.append(jnp.where(beaten < TOPK_GROUPS, vals[g], neg))

    ids = [row + float(g * GROUP_SIZE) for g in range(N_GROUPS)]
    sel = [jnp.zeros((GROUP_SIZE, tm), F32) for _ in range(N_GROUPS)]
    for _ in range(TOP_K):
        best = functools.reduce(jnp.maximum, cur)
        mx = jnp.max(best, axis=0, keepdims=True)
        cand = functools.reduce(jnp.minimum, [jnp.where(c == mx, i, 1e9) for c, i in zip(cur, ids)])
        idx = jnp.min(cand, axis=0, keepdims=True)
        hits = [i == idx for i in ids]
        sel = [jnp.where(h, 1.0, s) for h, s in zip(hits, sel)]
        cur = [jnp.where(h, neg, c) for h, c in zip(hits, cur)]
    w = [s * scores[sl] for s, sl in zip(sel, groups)]
    total = jnp.sum(functools.reduce(lambda a, b: a + b, w), axis=0, keepdims=True)
    gates = [wg / total * ROUTED_SCALE for wg in w]
    return jnp.concatenate(gates, axis=0), jnp.concatenate(sel, axis=0)


def _outproj_kernel(x_ref, at_ref, gm_ref, mod_ref, wo_ref, nfg_ref, wr_ref, rb_ref,
                    wsg_ref, wsu_ref, wsd_ref, tri_ref,
                    y0_ref, fx_ref, gt_ref, pt_ref, cnt_ref):
    mod = mod_ref[0]
    mix = _dot(at_ref[0], wo_ref[0:DA_WIDTH, :]) + _dot(gm_ref[0], wo_ref[DA_WIDTH:, :])
    x1 = x_ref[0] + mod[2:3] * mix
    fx = (_rms_rows(x1, nfg_ref[...]) * (1.0 + mod[4:5]) + mod[3:4]).astype(BF)
    fx_ref[0] = fx
    scores = _sigmoid(_dot_nt(wr_ref[...], fx))
    gates, sel = _route(scores, rb_ref[...])
    rank = _dot(sel.astype(BF), tri_ref[...])
    gt_ref[0] = gates
    pt_ref[0] = jnp.where(sel > 0.5, rank, -1.0)
    cnt_ref[0] = jnp.sum(sel, axis=1, keepdims=True).astype(jnp.int32)
    sg = _dot(fx, wsg_ref[...])
    su = _dot(fx, wsu_ref[...])
    hs = (sg * _sigmoid(sg)) * su
    y0_ref[0] = x1 + mod[5:6] * _dot(hs.astype(BF), wsd_ref[...])


def _outproj_call(x, attn, gm, mod, w_out, nfg, wr, rb, wsg, wsu, wsd, tri, tm):
    b, l, d = x.shape
    nt = l // tm
    full = lambda shape: pl.BlockSpec(shape, lambda bi, i: (0,) * len(shape))
    tokd = pl.BlockSpec((1, tm, d), lambda bi, i: (bi, i, 0))
    tokh = pl.BlockSpec((1, tm, DA_WIDTH), lambda bi, i: (bi, i, 0))
    expt = pl.BlockSpec((1, N_EXPERTS, tm), lambda bi, i: (bi, 0, i))
    ds = wsg.shape[1]
    return pl.pallas_call(
        _outproj_kernel,
        out_shape=(jax.ShapeDtypeStruct((b, l, d), F32),
                   jax.ShapeDtypeStruct((b, l, d), BF),
                   jax.ShapeDtypeStruct((b, N_EXPERTS, l), F32),
                   jax.ShapeDtypeStruct((b, N_EXPERTS, l), F32),
                   jax.ShapeDtypeStruct((b * nt, N_EXPERTS, 1), jnp.int32)),
        grid=(b, nt),
        in_specs=[tokd, tokh, tokh,
                  pl.BlockSpec((1, 6, d), lambda bi, i: (bi, 0, 0)),
                  full((d, d)), full((1, d)), full((N_EXPERTS, d)), full((N_EXPERTS, 1)),
                  full((d, ds)), full((d, ds)), full((ds, d)), full((tm, tm))],
        out_specs=(tokd, tokd, expt, expt,
                   pl.BlockSpec((1, N_EXPERTS, 1), lambda bi, i: (bi * nt + i, 0, 0))),
        compiler_params=pltpu.CompilerParams(dimension_semantics=("arbitrary", "arbitrary"),
                                             vmem_limit_bytes=VMEM_LIMIT),
        name="outproj",
    )(x, attn, gm, mod, w_out, nfg, wr, rb, wsg, wsu, wsd, tri)


def _moe_kernel(cnt_ref, order_ref, fx_ref, gt_ref, pt_ref, *refs, sub, caps, epg):
    wg_ref, wu_ref, wd_ref = refs[0:epg], refs[epg:2 * epg], refs[2 * epg:3 * epg]
    y0_ref, mod_ref, o_ref = refs[3 * epg:]
    tt = fx_ref.shape[1]
    nsub = tt // sub
    pair = pl.program_id(2)
    experts = [order_ref[pl.program_id(0) * N_EXPERTS + pair * epg + k] for k in range(epg)]
    sub0 = (pl.program_id(0) * pl.num_programs(1) + pl.program_id(1)) * nsub

    @pl.when(pair == 0)
    def _zero():
        o_ref[...] = jnp.zeros_like(o_ref)

    cmax = jnp.int32(0)
    for s in range(nsub):
        for k in range(epg):
            cmax = jnp.maximum(cmax, cnt_ref[(sub0 + s) * N_EXPERTS + experts[k]])

    def do_round(r, cap):
        slot = lax.broadcasted_iota(jnp.int32, (cap, sub), 0).astype(F32) + (r * cap).astype(F32)
        picks, xrows, grows = [], [], []
        for s in range(nsub):
            cols = slice(s * sub, (s + 1) * sub)
            hit = [pt_ref[0, pl.ds(experts[k], 1), cols] == slot for k in range(epg)]
            grows.append([jnp.sum(jnp.where(hit[k], gt_ref[0, pl.ds(experts[k], 1), cols], 0.0),
                                  axis=-1, keepdims=True) for k in range(epg)])
            pick = jnp.concatenate([jnp.where(h, 1.0, 0.0) for h in hit], axis=0).astype(BF)
            picks.append(pick)
            xrows.append(_dot(pick, fx_ref[0, cols, :]).astype(BF))
        outs = []
        for k in range(epg):
            xk = jnp.concatenate([xrows[s][k * cap:(k + 1) * cap] for s in range(nsub)], axis=0)
            gk = jnp.concatenate([grows[s][k] for s in range(nsub)], axis=0)
            a = _dot(xk, wg_ref[k][0])
            bb = _dot(xk, wu_ref[k][0])
            hm = (a * _sigmoid(a)) * bb * gk
            outs.append(_dot(hm.astype(BF), wd_ref[k][0]).astype(BF))
        for s in range(nsub):
            cols = slice(s * sub, (s + 1) * sub)
            stacked = jnp.concatenate([outs[k][s * cap:(s + 1) * cap] for k in range(epg)], axis=0)
            o_ref[0, cols, :] += lax.dot_general(picks[s], stacked, (((0,), (0,)), ((), ())),
                                                 preferred_element_type=F32)

    below = 0
    for cap in caps:
        fits = (cmax <= cap) if below == 0 else ((cmax > below) & (cmax <= cap))
        pl.when(fits)(functools.partial(do_round, jnp.int32(0), cap))
        below = cap

    @pl.when(cmax > caps[-1])
    def _many_rounds():
        def body(r, carry):
            do_round(r, caps[-1])
            return carry
        lax.fori_loop(0, (cmax + (caps[-1] - 1)) // caps[-1], body, 0)

    @pl.when(pair == pl.num_programs(2) - 1)
    def _finish():
        o_ref[0] = y0_ref[0] + mod_ref[0][5:6] * o_ref[0]


def _moe_call(counts, order, fx, gt, pt, wg, wu, wd, y0, mod, tt, sub, caps, epg):
    b, l, d = fx.shape
    ne, _, de = wg.shape
    assert l % tt == 0 and tt % sub == 0 and ne % epg == 0
    assert all(c % 16 == 0 for c in caps) and list(caps) == sorted(caps)
    tokd = pl.BlockSpec((1, tt, d), lambda bi, i, p, cnt, order: (bi, i, 0))
    expt = pl.BlockSpec((1, ne, tt), lambda bi, i, p, cnt, order: (bi, 0, i))

    def expert_block(shape, k):
        return pl.BlockSpec((1,) + shape,
                            lambda bi, i, p, cnt, order: (order[bi * ne + p * epg + k], 0, 0))

    kern = functools.partial(_moe_kernel, sub=sub, caps=tuple(caps), epg=epg)
    grid_spec = pltpu.PrefetchScalarGridSpec(
        num_scalar_prefetch=2,
        grid=(b, l // tt, ne // epg),
        in_specs=([tokd, expt, expt]
                  + [expert_block((d, de), k) for k in range(epg)]
                  + [expert_block((d, de), k) for k in range(epg)]
                  + [expert_block((de, d), k) for k in range(epg)]
                  + [tokd, pl.BlockSpec((1, 6, d), lambda bi, i, p, cnt, order: (bi, 0, 0))]),
        out_specs=tokd)
    return pl.pallas_call(
        kern,
        out_shape=jax.ShapeDtypeStruct((b, l, d), F32),
        grid_spec=grid_spec,
        compiler_params=pltpu.CompilerParams(
            dimension_semantics=("arbitrary", "arbitrary", "arbitrary"),
            vmem_limit_bytes=VMEM_LIMIT),
        name="moe",
    )(counts, order, fx, gt, pt, *([wg] * epg), *([wu] * epg), *([wd] * epg), y0, mod)


def _rope_tables(n_tokens):
    rows = n_tokens // GRID_W
    row = jnp.repeat(jnp.arange(rows, dtype=F32), GRID_W)
    col = jnp.tile(jnp.arange(GRID_W, dtype=F32), rows)
    half = DA_HEAD_DIM // 2
    inv_freq = ROPE_THETA ** (-jnp.arange(0, half, 2, dtype=F32) / half)
    ang = jnp.concatenate([row[:, None] * inv_freq, col[:, None] * inv_freq], axis=-1)
    cos, sin = jnp.cos(ang), jnp.sin(ang)
    cos64 = jnp.repeat(cos, 2, axis=-1)
    sin64 = jnp.stack([-sin, sin], axis=-1).reshape(n_tokens, DA_HEAD_DIM)
    return jnp.tile(cos64, (1, 2)), jnp.tile(sin64, (1, 2))


def kernel(x, c, ctx, c_ctx, w_ada, b_ada, norm_mix_g, w_in, q_norm_g, k_norm_g, da_lambda, subln_g, gm_ln_g, gm_ln_b, gm_ws, gm_bs, gm_out_g, w_out, norm_ffn_g, w_router, router_bias, we_gate, we_up, we_down, ws_gate, ws_up, ws_down):
    assert w_ada.shape[0] == 1, "single-layer kernel"
    b, l, d = x.shape
    lambda_init = 0.8 - 0.6 * math.exp(-0.3 * 0)
    lp = da_lambda[0].astype(F32)
    lam = (jnp.exp(jnp.sum(lp[0] * lp[1])) - jnp.exp(jnp.sum(lp[2] * lp[3])) + lambda_init).reshape(1)

    assert b + 1 <= COND_ROWS
    cond = jnp.zeros((COND_ROWS, d), F32).at[:b].set(c).at[b].set(c_ctx)
    ada = _ada_call(cond, w_ada[0], b_ada[0][None, :]).reshape(COND_ROWS, 6, d)
    mod = ada[:b]
    mod_ctx = ada[b:b + 1]

    cos, sin = _rope_tables(l)
    half = jnp.arange(LANES) // DA_HEAD_DIM
    gmat = (half[:, None] == half[None, :]).astype(BF)
    qg = jnp.tile(q_norm_g[0], 2)[None, :]
    kg = jnp.tile(k_norm_g[0], 2)[None, :]
    w_in_bf = w_in[0].astype(BF)
    bs_full = jnp.broadcast_to(gm_bs[0][:, :, None], (GM_HEADS, CHUNK, GM_HEAD_DIM))

    tm = min(ROUTE_TILE, l)
    assert l % GRID_W == 0 and l % tm == 0 and l % min(INPROJ_TILE, l) == 0 and tm % CHUNK == 0
    q, k, v, gm = _inproj_call(
        x, mod, norm_mix_g, w_in_bf, qg, kg, cos, sin, gmat,
        gm_ln_g[0].reshape(1, GM_WIDTH), gm_ln_b[0].reshape(1, GM_WIDTH),
        gm_ws[0].astype(BF), bs_full, gm_out_g[0].reshape(1, GM_WIDTH), min(INPROJ_TILE, l))
    kc, vc = _ctxproj_call(ctx, mod_ctx, norm_mix_g, w_in_bf[:, DA_WIDTH:3 * DA_WIDTH], kg, gmat)

    attn = _attn_call(lam, q, k, kc, v, vc, subln_g[0][:, None],
                      tq=min(ATTN_Q_TILE, l), ck=min(ATTN_KEY_CHUNK, ctx.shape[1]),
                      out_scale=1.0 - lambda_init)

    wr = w_router[0].T.astype(BF)
    rb = router_bias[0][:, None]
    tok = jnp.arange(tm)
    tri = (tok[:, None] < tok[None, :]).astype(BF)
    y0, fx, gt, pt, counts = _outproj_call(
        x, attn, gm, mod, w_out[0].astype(BF), norm_ffn_g, wr, rb,
        ws_gate[0].astype(BF), ws_up[0].astype(BF), ws_down[0].astype(BF), tri, tm)

    totals = jnp.sum(counts.reshape(b, -1, N_EXPERTS), axis=1)
    order = jnp.argsort(-totals, axis=1).astype(jnp.int32).reshape(-1)
    return _moe_call(counts.reshape(-1), order, fx, gt, pt,
                     we_gate[0].astype(BF), we_up[0].astype(BF), we_down[0].astype(BF),
                     y0, mod, tt=min(MOE_TILE, l), sub=tm, caps=MOE_CAPS, epg=MOE_EXPERTS_PER_STEP)
```

```python
import functools
import math

import jax
import jax.numpy as jnp
from jax import lax
from jax.experimental import pallas as pl
from jax.experimental.pallas import tpu as pltpu

BF = jnp.bfloat16
F32 = jnp.float32

EPS = 1e-6
GRID_W = 64
DA_HEADS = 4
DA_HEAD_DIM = 64
DA_V_DIM = 128
DA_WIDTH = 512
GM_HEADS = 4
GM_HEAD_DIM = 128
GM_WIDTH = 512
CHUNK = 128
ROPE_THETA = 10000.0
Q_SCALE = DA_HEAD_DIM ** -0.5 * math.log2(math.e)
N_EXPERTS = 32
TOP_K = 4
N_GROUPS = 4
TOPK_GROUPS = 2
GROUP_SIZE = N_EXPERTS // N_GROUPS
ROUTED_SCALE = 2.5
ATTN_Q_TILE = 256
ATTN_KEY_CHUNK = 256
INPROJ_TILE = 512
ROUTE_TILE = 512
MOE_TILE = 2048
MOE_CAPS = (48, 64, 80, 96, 128, 192, 256)
MOE_EXPERTS_PER_STEP = 2
ADA_BLOCK = 1024
COND_ROWS = 16
LANES = 128
VMEM_LIMIT = 56 * 1024 * 1024


def _sigmoid(x):
    return 1.0 / (1.0 + jnp.exp(-x))


def _dot(a, b):
    return jnp.dot(a, b, preferred_element_type=F32)


def _dot_nt(a, b):
    return lax.dot_general(a, b, (((1,), (1,)), ((), ())), preferred_element_type=F32)


def _rms_rows(x, g):
    ms = jnp.mean(x * x, axis=-1, keepdims=True)
    return x * lax.rsqrt(ms + EPS) * g


def _group_mean_sq(y, gmat):
    y2 = y * y
    hi = y2.astype(BF)
    lo = (y2 - hi.astype(F32)).astype(BF)
    return (_dot(hi, gmat) + _dot(lo, gmat)) * (1.0 / DA_HEAD_DIM)


def _swap_pairs(y):
    lane = lax.broadcasted_iota(jnp.int32, y.shape, 1)
    nxt = pltpu.roll(y, LANES - 1, 1)
    prv = pltpu.roll(y, 1, 1)
    return jnp.where((lane & 1) == 0, nxt, prv)


def _ada_kernel(cond_ref, w_ref, b_ref, o_ref):
    c = cond_ref[...]
    s = c * _sigmoid(c)
    o_ref[...] = _dot(s.astype(BF), w_ref[...].astype(BF)) + b_ref[...]


def _ada_call(cond, w_ada, b_ada):
    rows, d = cond.shape
    n = w_ada.shape[1]
    bn = ADA_BLOCK
    return pl.pallas_call(
        _ada_kernel,
        out_shape=jax.ShapeDtypeStruct((rows, n), F32),
        grid=(n // bn,),
        in_specs=[pl.BlockSpec((rows, d), lambda j: (0, 0)),
                  pl.BlockSpec((d, bn), lambda j: (0, j)),
                  pl.BlockSpec((1, bn), lambda j: (0, j))],
        out_specs=pl.BlockSpec((rows, bn), lambda j: (0, j)),
        compiler_params=pltpu.CompilerParams(dimension_semantics=("arbitrary",),
                                             vmem_limit_bytes=VMEM_LIMIT),
        name="ada",
    )(cond, w_ada, b_ada)


def _inproj_kernel(x_ref, mod_ref, ng_ref, w_ref, qg_ref, kg_ref, cos_ref, sin_ref, gmat_ref,
                   lng_ref, lnb_ref, ws_ref, bs_ref, og_ref,
                   q_ref, k_ref, v_ref, gm_ref):
    tm = x_ref.shape[1]
    x = x_ref[0]
    mod = mod_ref[0]
    h = (_rms_rows(x, ng_ref[...]) * (1.0 + mod[1:2]) + mod[0:1]).astype(BF)

    def proj(col0, width=2 * LANES):
        return _dot(h, w_ref[:, col0:col0 + width])

    def gelu(t):
        return 0.5 * t * (1.0 + lax.erf(t * math.sqrt(0.5)))

    gmat = gmat_ref[...]
    cos = cos_ref[...]
    sin = sin_ref[...]
    for jb in range(DA_HEADS // 2):
        pq = proj(jb * 2 * LANES)
        pk = proj(DA_WIDTH + jb * 2 * LANES)
        for jj in range(2):
            sl = slice((2 * jb + jj) * LANES, (2 * jb + jj + 1) * LANES)
            qj = pq[:, jj * LANES:(jj + 1) * LANES]
            qn = qj * lax.rsqrt(_group_mean_sq(qj, gmat) + EPS) * qg_ref[...]
            qr = qn * cos + _swap_pairs(qn) * sin
            q_ref[0, :, sl] = (qr * Q_SCALE).astype(BF)
            kj = pk[:, jj * LANES:(jj + 1) * LANES]
            kn = kj * lax.rsqrt(_group_mean_sq(kj, gmat) + EPS) * kg_ref[...]
            kr = kn * cos + _swap_pairs(kn) * sin
            k_ref[0, :, sl] = kr.astype(BF)
    v_ref[0] = proj(2 * DA_WIDTH, DA_WIDTH).astype(BF)

    for g in range(GM_HEADS):
        sl = slice(g * LANES, (g + 1) * LANES)
        if g % 2 == 0:
            zu = gelu(proj(3 * DA_WIDTH + g * LANES))
            zv = gelu(proj(3 * DA_WIDTH + GM_WIDTH + g * LANES))
        u = zu[:, (g % 2) * LANES:(g % 2 + 1) * LANES]
        vg = zv[:, (g % 2) * LANES:(g % 2 + 1) * LANES]
        mu = jnp.mean(vg, axis=-1, keepdims=True)
        xc = vg - mu
        var = jnp.mean(xc * xc, axis=-1, keepdims=True)
        vn = (xc * lax.rsqrt(var + EPS) * lng_ref[:, sl] + lnb_ref[:, sl]).astype(BF)
        for cidx in range(tm // CHUNK):
            rows = slice(cidx * CHUNK, (cidx + 1) * CHUNK)
            mixed = _dot(ws_ref[g], vn[rows]) + bs_ref[g]
            y = u[rows] * mixed
            gm_ref[0, rows, sl] = _rms_rows(y, og_ref[:, sl]).astype(BF)


def _inproj_call(x, mod, ng, w_in, qg, kg, cos, sin, gmat, lng, lnb, ws, bs, og, tm):
    b, l, d = x.shape
    nw = w_in.shape[1]
    full = lambda shape: pl.BlockSpec(shape, lambda bi, i: (0,) * len(shape))
    tok = pl.BlockSpec((1, tm, DA_WIDTH), lambda bi, i: (bi, i, 0))
    out = jax.ShapeDtypeStruct((b, l, DA_WIDTH), BF)
    return pl.pallas_call(
        _inproj_kernel,
        out_shape=(out, out, out, out),
        grid=(b, l // tm),
        in_specs=[pl.BlockSpec((1, tm, d), lambda bi, i: (bi, i, 0)),
                  pl.BlockSpec((1, 6, d), lambda bi, i: (bi, 0, 0)),
                  full((1, d)), full((d, nw)), full((1, LANES)), full((1, LANES)),
                  pl.BlockSpec((tm, LANES), lambda bi, i: (i, 0)),
                  pl.BlockSpec((tm, LANES), lambda bi, i: (i, 0)),
                  full((LANES, LANES)), full((1, GM_WIDTH)), full((1, GM_WIDTH)),
                  full((GM_HEADS, CHUNK, CHUNK)), full((GM_HEADS, CHUNK, GM_HEAD_DIM)),
                  full((1, GM_WIDTH))],
        out_specs=(tok, tok, tok, tok),
        compiler_params=pltpu.CompilerParams(dimension_semantics=("arbitrary", "arbitrary"),
                                             vmem_limit_bytes=VMEM_LIMIT),
        name="inproj",
    )(x, mod, ng, w_in, qg, kg, cos, sin, gmat, lng, lnb, ws, bs, og)


def _ctxproj_kernel(x_ref, mod_ref, ng_ref, w_ref, kg_ref, gmat_ref, k_ref, v_ref):
    x = x_ref[0]
    mod = mod_ref[0]
    h = _rms_rows(x, ng_ref[...]) * (1.0 + mod[1:2]) + mod[0:1]
    p = _dot(h.astype(BF), w_ref[...])
    gmat = gmat_ref[...]
    for j in range(DA_HEADS):
        sl = slice(j * LANES, (j + 1) * LANES)
        kj = p[:, sl]
        kn = kj * lax.rsqrt(_group_mean_sq(kj, gmat) + EPS) * kg_ref[...]
        k_ref[0, :, sl] = kn.astype(BF)
    v_ref[0] = p[:, DA_WIDTH:].astype(BF)


def _ctxproj_call(ctx, mod_ctx, ng, w_kv, kg, gmat):
    b, lc, d = ctx.shape
    full = lambda shape: pl.BlockSpec(shape, lambda bi: (0,) * len(shape))
    tok = pl.BlockSpec((1, lc, DA_WIDTH), lambda bi: (bi, 0, 0))
    out = jax.ShapeDtypeStruct((b, lc, DA_WIDTH), BF)
    return pl.pallas_call(
        _ctxproj_kernel,
        out_shape=(out, out),
        grid=(b,),
        in_specs=[pl.BlockSpec((1, lc, d), lambda bi: (bi, 0, 0)),
                  full((1, 6, d)), full((1, d)), full((d, 2 * DA_WIDTH)),
                  full((1, LANES)), full((LANES, LANES))],
        out_specs=(tok, tok),
        compiler_params=pltpu.CompilerParams(dimension_semantics=("arbitrary",),
                                             vmem_limit_bytes=VMEM_LIMIT),
        name="ctxproj",
    )(ctx, mod_ctx, ng, w_kv, kg, gmat)


def _attn_kernel(lam_ref, q_ref, kx_ref, kc_ref, vx_ref, vc_ref, sg_ref, o_ref,
                 k_scr, vt_scr, s_scr, m_scr, *, ck, out_scale):
    tq = q_ref.shape[1]
    lc = kc_ref.shape[1]
    lx = kx_ref.shape[1]
    nck = (lc + lx) // ck

    step = pl.program_id(2)
    last = pl.num_programs(2) - 1

    def stacked_queries():
        q = q_ref[0]
        lane = lax.broadcasted_iota(jnp.int32, q.shape, 1)
        zero = jnp.zeros_like(q)
        return jnp.concatenate([jnp.where(lane < DA_HEAD_DIM, q, zero),
                                jnp.where(lane >= DA_HEAD_DIM, q, zero)], axis=0)

    def score_chunk(qq, cidx, mrun):
        rows = slice(cidx * ck, (cidx + 1) * ck)
        st = _dot_nt(k_scr[rows, :], qq)
        s_scr[rows, :] = st
        return jnp.maximum(mrun, jnp.max(st.reshape(ck // 8, 8, 2 * tq), axis=0))

    def weight_chunk(m, cidx, acc, lrun):
        rows = slice(cidx * ck, (cidx + 1) * ck)
        pt = jnp.exp2(s_scr[rows, :] - m)
        acc = acc + _dot(vt_scr[cidx], pt.astype(BF))
        return acc, lrun + jnp.sum(pt.reshape(ck // 8, 8, 2 * tq), axis=0)

    def finish(acc, lrun):
        r = 1.0 / jnp.sum(lrun, axis=0, keepdims=True)
        ot = acc[:, :tq] * r[:, :tq] - lam_ref[0] * (acc[:, tq:] * r[:, tq:])
        ms = jnp.mean(ot * ot, axis=0, keepdims=True)
        on = ot * lax.rsqrt(ms + EPS) * sg_ref[...] * out_scale
        o_ref[0] = on.T.astype(BF)

    mrun0 = jnp.full((8, 2 * tq), -jnp.inf, F32)
    lrun0 = jnp.zeros((8, 2 * tq), F32)
    acc0 = jnp.zeros((DA_V_DIM, 2 * tq), F32)

    @pl.when(step == 0)
    def _first():
        k_scr[0:lc, :] = kc_ref[0]
        k_scr[lc:lc + lx, :] = kx_ref[0]
        for cidx in range(nck):
            lo = cidx * ck
            if lo < lc:
                blk = vc_ref[0, lo:lo + ck, :]
            else:
                blk = vx_ref[0, lo - lc:lo - lc + ck, :]
            vt_scr[cidx] = blk.astype(F32).T.astype(BF)
        qq = stacked_queries()
        mrun = mrun0
        for cidx in range(nck):
            mrun = score_chunk(qq, cidx, mrun)
        m_scr[...] = mrun

    @pl.when((step > 0) & (step < last))
    def _steady():
        qq = stacked_queries()
        m = jnp.max(m_scr[...], axis=0, keepdims=True)
        mrun, lrun, acc = mrun0, lrun0, acc0
        for cidx in range(nck):
            acc, lrun = weight_chunk(m, cidx, acc, lrun)
            mrun = score_chunk(qq, cidx, mrun)
        finish(acc, lrun)
        m_scr[...] = mrun

    @pl.when(step == last)
    def _last():
        m = jnp.max(m_scr[...], axis=0, keepdims=True)
        lrun, acc = lrun0, acc0
        for cidx in range(nck):
            acc, lrun = weight_chunk(m, cidx, acc, lrun)
        finish(acc, lrun)


def _attn_call(lam, q, kx, kc, vx, vc, subln_col, tq, ck, out_scale):
    b, l, _ = q.shape
    lc = kc.shape[1]
    lk = lc + l
    assert lk % ck == 0 and lc % ck == 0 and l % tq == 0
    nq = l // tq
    kern = functools.partial(_attn_kernel, ck=ck, out_scale=out_scale)
    return pl.pallas_call(
        kern,
        out_shape=jax.ShapeDtypeStruct((b, l, DA_WIDTH), BF),
        grid=(b, DA_HEADS, nq + 1),
        in_specs=[pl.BlockSpec(memory_space=pltpu.SMEM),
                  pl.BlockSpec((1, tq, LANES), lambda bi, h, i: (bi, jnp.minimum(i, nq - 1), h)),
                  pl.BlockSpec((1, l, LANES), lambda bi, h, i: (bi, 0, h)),
                  pl.BlockSpec((1, lc, LANES), lambda bi, h, i: (bi, 0, h)),
                  pl.BlockSpec((1, l, LANES), lambda bi, h, i: (bi, 0, h)),
                  pl.BlockSpec((1, lc, LANES), lambda bi, h, i: (bi, 0, h)),
                  pl.BlockSpec((DA_V_DIM, 1), lambda bi, h, i: (0, 0))],
        out_specs=pl.BlockSpec((1, tq, LANES), lambda bi, h, i: (bi, jnp.maximum(i - 1, 0), h)),
        scratch_shapes=[pltpu.VMEM((lk, LANES), BF),
                        pltpu.VMEM((lk // ck, DA_V_DIM, ck), BF),
                        pltpu.VMEM((lk, 2 * tq), F32),
                        pltpu.VMEM((8, 2 * tq), F32)],
        compiler_params=pltpu.CompilerParams(
            dimension_semantics=("arbitrary", "arbitrary", "arbitrary"),
            vmem_limit_bytes=VMEM_LIMIT),
        name="attn",
    )(lam, q, kx, kc, vx, vc, subln_col)


def _route(scores, rbias):
    tm = scores.shape[1]
    assert GROUP_SIZE == 8 and scores.shape[0] == N_EXPERTS
    row = lax.broadcasted_iota(jnp.int32, (GROUP_SIZE, tm), 0).astype(F32)
    neg = jnp.full((GROUP_SIZE, tm), -jnp.inf, F32)
    biased = scores + rbias
    groups = [slice(g * GROUP_SIZE, (g + 1) * GROUP_SIZE) for g in range(N_GROUPS)]
    vals = [biased[sl] for sl in groups]

    gscore = []
    for v in vals:
        m1 = jnp.max(v, axis=0, keepdims=True)
        i1 = jnp.min(jnp.where(v == m1, row, 1e9), axis=0, keepdims=True)
        m2 = jnp.max(jnp.where(row == i1, neg, v), axis=0, keepdims=True)
        gscore.append(m1 + m2)

    cur = []
    for g in range(N_GROUPS):
        beaten = jnp.zeros_like(gscore[g])
        for g2 in range(N_GROUPS):
            if g2 == g:
                continue
            beat = (gscore[g2] >= gscore[g]) if g2 < g else (gscore[g2] > gscore[g])
            beaten = beaten + jnp.where(beat, 1.0, 0.0)
        cur.append(jnp.where(beaten < TOPK_GROUPS, vals[g], neg))

    ids = [row + float(g * GROUP_SIZE) for g in range(N_GROUPS)]
    sel = [jnp.zeros((GROUP_SIZE, tm), F32) for _ in range(N_GROUPS)]
    for _ in range(TOP_K):
        best = functools.reduce(jnp.maximum, cur)
        mx = jnp.max(best, axis=0, keepdims=True)
        cand = functools.reduce(jnp.minimum, [jnp.where(c == mx, i, 1e9) for c, i in zip(cur, ids)])
        idx = jnp.min(cand, axis=0, keepdims=True)
        hits = [i == idx for i in ids]
        sel = [jnp.where(h, 1.0, s) for h, s in zip(hits, sel)]
        cur = [jnp.where(h, neg, c) for h, c in zip(hits, cur)]
    w = [s * scores[sl] for s, sl in zip(sel, groups)]
    total = jnp.sum(functools.reduce(lambda a, b: a + b, w), axis=0, keepdims=True)
    gates = [wg / total * ROUTED_SCALE for wg in w]
    return jnp.concatenate(gates, axis=0), jnp.concatenate(sel, axis=0)


def _outproj_kernel(x_ref, at_ref, gm_ref, mod_ref, wo_ref, nfg_ref, wr_ref, rb_ref,
                    wsg_ref, wsu_ref, wsd_ref, tri_ref,
                    y0_ref, fx_ref, gt_ref, pt_ref, cnt_ref):
    mod = mod_ref[0]
    mix = _dot(at_ref[0], wo_ref[0:DA_WIDTH, :]) + _dot(gm_ref[0], wo_ref[DA_WIDTH:, :])
    x1 = x_ref[0] + mod[2:3] * mix
    fx = (_rms_rows(x1, nfg_ref[...]) * (1.0 + mod[4:5]) + mod[3:4]).astype(BF)
    fx_ref[0] = fx
    scores = _sigmoid(_dot_nt(wr_ref[...], fx))
    gates, sel = _route(scores, rb_ref[...])
    rank = _dot(sel.astype(BF), tri_ref[...])
    gt_ref[0] = gates
    pt_ref[0] = jnp.where(sel > 0.5, rank, -1.0)
    cnt_ref[0] = jnp.sum(sel, axis=1, keepdims=True).astype(jnp.int32)
    sg = _dot(fx, wsg_ref[...])
    su = _dot(fx, wsu_ref[...])
    hs = (sg * _sigmoid(sg)) * su
    y0_ref[0] = x1 + mod[5:6] * _dot(hs.astype(BF), wsd_ref[...])


def _outproj_call(x, attn, gm, mod, w_out, nfg, wr, rb, wsg, wsu, wsd, tri, tm):
    b, l, d = x.shape
    nt = l // tm
    full = lambda shape: pl.BlockSpec(shape, lambda bi, i: (0,) * len(shape))
    tokd = pl.BlockSpec((1, tm, d), lambda bi, i: (bi, i, 0))
    tokh = pl.BlockSpec((1, tm, DA_WIDTH), lambda bi, i: (bi, i, 0))
    expt = pl.BlockSpec((1, N_EXPERTS, tm), lambda bi, i: (bi, 0, i))
    ds = wsg.shape[1]
    return pl.pallas_call(
        _outproj_kernel,
        out_shape=(jax.ShapeDtypeStruct((b, l, d), F32),
                   jax.ShapeDtypeStruct((b, l, d), BF),
                   jax.ShapeDtypeStruct((b, N_EXPERTS, l), F32),
                   jax.ShapeDtypeStruct((b, N_EXPERTS, l), F32),
                   jax.ShapeDtypeStruct((b * nt, N_EXPERTS, 1), jnp.int32)),
        grid=(b, nt),
        in_specs=[tokd, tokh, tokh,
                  pl.BlockSpec((1, 6, d), lambda bi, i: (bi, 0, 0)),
                  full((d, d)), full((1, d)), full((N_EXPERTS, d)), full((N_EXPERTS, 1)),
                  full((d, ds)), full((d, ds)), full((ds, d)), full((tm, tm))],
        out_specs=(tokd, tokd, expt, expt,
                   pl.BlockSpec((1, N_EXPERTS, 1), lambda bi, i: (bi * nt + i, 0, 0))),
        compiler_params=pltpu.CompilerParams(dimension_semantics=("arbitrary", "arbitrary"),
                                             vmem_limit_bytes=VMEM_LIMIT),
        name="outproj",
    )(x, attn, gm, mod, w_out, nfg, wr, rb, wsg, wsu, wsd, tri)


def _moe_kernel(cnt_ref, order_ref, fx_ref, gt_ref, pt_ref, *refs, sub, caps, epg):
    wg_ref, wu_ref, wd_ref = refs[0:epg], refs[epg:2 * epg], refs[2 * epg:3 * epg]
    y0_ref, mod_ref, o_ref = refs[3 * epg:]
    tt = fx_ref.shape[1]
    nsub = tt // sub
    pair = pl.program_id(2)
    experts = [order_ref[pl.program_id(0) * N_EXPERTS + pair * epg + k] for k in range(epg)]
    sub0 = (pl.program_id(0) * pl.num_programs(1) + pl.program_id(1)) * nsub

    @pl.when(pair == 0)
    def _zero():
        o_ref[...] = jnp.zeros_like(o_ref)

    cmax = jnp.int32(0)
    for s in range(nsub):
        for k in range(epg):
            cmax = jnp.maximum(cmax, cnt_ref[(sub0 + s) * N_EXPERTS + experts[k]])

    def do_round(r, cap):
        slot = lax.broadcasted_iota(jnp.int32, (cap, sub), 0).astype(F32) + (r * cap).astype(F32)
        picks, xrows, grows = [], [], []
        for s in range(nsub):
            cols = slice(s * sub, (s + 1) * sub)
            hit = [pt_ref[0, pl.ds(experts[k], 1), cols] == slot for k in range(epg)]
            grows.append([jnp.sum(jnp.where(hit[k], gt_ref[0, pl.ds(experts[k], 1), cols], 0.0),
                                  axis=-1, keepdims=True) for k in range(epg)])
            pick = jnp.concatenate([jnp.where(h, 1.0, 0.0) for h in hit], axis=0).astype(BF)
            picks.append(pick)
            xrows.append(_dot(pick, fx_ref[0, cols, :]).astype(BF))
        outs = []
        for k in range(epg):
            xk = jnp.concatenate([xrows[s][k * cap:(k + 1) * cap] for s in range(nsub)], axis=0)
            gk = jnp.concatenate([grows[s][k] for s in range(nsub)], axis=0)
            a = _dot(xk, wg_ref[k][0])
            bb = _dot(xk, wu_ref[k][0])
            hm = (a * _sigmoid(a)) * bb * gk
            outs.append(_dot(hm.astype(BF), wd_ref[k][0]).astype(BF))
        for s in range(nsub):
            cols = slice(s * sub, (s + 1) * sub)
            stacked = jnp.concatenate([outs[k][s * cap:(s + 1) * cap] for k in range(epg)], axis=0)
            o_ref[0, cols, :] += lax.dot_general(picks[s], stacked, (((0,), (0,)), ((), ())),
                                                 preferred_element_type=F32)

    below = 0
    for cap in caps:
        fits = (cmax <= cap) if below == 0 else ((cmax > below) & (cmax <= cap))
        pl.when(fits)(functools.partial(do_round, jnp.int32(0), cap))
        below = cap

    @pl.when(cmax > caps[-1])
    def _many_rounds():
        def body(r, carry):
            do_round(r, caps[-1])
            return carry
        lax.fori_loop(0, (cmax + (caps[-1] - 1)) // caps[-1], body, 0)

    @pl.when(pair == pl.num_programs(2) - 1)
    def _finish():
        o_ref[0] = y0_ref[0] + mod_ref[0][5:6] * o_ref[0]


def _moe_call(counts, order, fx, gt, pt, wg, wu, wd, y0, mod, tt, sub, caps, epg):
    b, l, d = fx.shape
    ne, _, de = wg.shape
    assert l % tt == 0 and tt % sub == 0 and ne % epg == 0
    assert all(c % 16 == 0 for c in caps) and list(caps) == sorted(caps)
    tokd = pl.BlockSpec((1, tt, d), lambda bi, i, p, cnt, order: (bi, i, 0))
    expt = pl.BlockSpec((1, ne, tt), lambda bi, i, p, cnt, order: (bi, 0, i))

    def expert_block(shape, k):
        return pl.BlockSpec((1,) + shape,
                            lambda bi, i, p, cnt, order: (order[bi * ne + p * epg + k], 0, 0))

    kern = functools.partial(_moe_kernel, sub=sub, caps=tuple(caps), epg=epg)
    grid_spec = pltpu.PrefetchScalarGridSpec(
        num_scalar_prefetch=2,
        grid=(b, l // tt, ne // epg),
        in_specs=([tokd, expt, expt]
                  + [expert_block((d, de), k) for k in range(epg)]
                  + [expert_block((d, de), k) for k in range(epg)]
                  + [expert_block((de, d), k) for k in range(epg)]
                  + [tokd, pl.BlockSpec((1, 6, d), lambda bi, i, p, cnt, order: (bi, 0, 0))]),
        out_specs=tokd)
    return pl.pallas_call(
        kern,
        out_shape=jax.ShapeDtypeStruct((b, l, d), F32),
        grid_spec=grid_spec,
        compiler_params=pltpu.CompilerParams(
            dimension_semantics=("arbitrary", "arbitrary", "arbitrary"),
            vmem_limit_bytes=VMEM_LIMIT),
        name="moe",
    )(counts, order, fx, gt, pt, *([wg] * epg), *([wu] * epg), *([wd] * epg), y0, mod)


def _rope_tables(n_tokens):
    rows = n_tokens // GRID_W
    row = jnp.repeat(jnp.arange(rows, dtype=F32), GRID_W)
    col = jnp.tile(jnp.arange(GRID_W, dtype=F32), rows)
    half = DA_HEAD_DIM // 2
    inv_freq = ROPE_THETA ** (-jnp.arange(0, half, 2, dtype=F32) / half)
    ang = jnp.concatenate([row[:, None] * inv_freq, col[:, None] * inv_freq], axis=-1)
    cos, sin = jnp.cos(ang), jnp.sin(ang)
    cos64 = jnp.repeat(cos, 2, axis=-1)
    sin64 = jnp.stack([-sin, sin], axis=-1).reshape(n_tokens, DA_HEAD_DIM)
    return jnp.tile(cos64, (1, 2)), jnp.tile(sin64, (1, 2))


def kernel(x, c, ctx, c_ctx, w_ada, b_ada, norm_mix_g, w_in, q_norm_g, k_norm_g, da_lambda, subln_g, gm_ln_g, gm_ln_b, gm_ws, gm_bs, gm_out_g, w_out, norm_ffn_g, w_router, router_bias, we_gate, we_up, we_down, ws_gate, ws_up, ws_down):
    assert w_ada.shape[0] == 1, "single-layer kernel"
    b, l, d = x.shape
    lambda_init = 0.8 - 0.6 * math.exp(-0.3 * 0)
    lp = da_lambda[0].astype(F32)
    lam = (jnp.exp(jnp.sum(lp[0] * lp[1])) - jnp.exp(jnp.sum(lp[2] * lp[3])) + lambda_init).reshape(1)

    assert b + 1 <= COND_ROWS
    cond = jnp.zeros((COND_ROWS, d), F32).at[:b].set(c).at[b].set(c_ctx)
    ada = _ada_call(cond, w_ada[0], b_ada[0][None, :]).reshape(COND_ROWS, 6, d)
    mod = ada[:b]
    mod_ctx = ada[b:b + 1]

    cos, sin = _rope_tables(l)
    half = jnp.arange(LANES) // DA_HEAD_DIM
    gmat = (half[:, None] == half[None, :]).astype(BF)
    qg = jnp.tile(q_norm_g[0], 2)[None, :]
    kg = jnp.tile(k_norm_g[0], 2)[None, :]
    w_in_bf = w_in[0].astype(BF)
    bs_full = jnp.broadcast_to(gm_bs[0][:, :, None], (GM_HEADS, CHUNK, GM_HEAD_DIM))

    tm = min(ROUTE_TILE, l)
    assert l % GRID_W == 0 and l % tm == 0 and l % min(INPROJ_TILE, l) == 0 and tm % CHUNK == 0
    q, k, v, gm = _inproj_call(
        x, mod, norm_mix_g, w_in_bf, qg, kg, cos, sin, gmat,
        gm_ln_g[0].reshape(1, GM_WIDTH), gm_ln_b[0].reshape(1, GM_WIDTH),
        gm_ws[0].astype(BF), bs_full, gm_out_g[0].reshape(1, GM_WIDTH), min(INPROJ_TILE, l))
    kc, vc = _ctxproj_call(ctx, mod_ctx, norm_mix_g, w_in_bf[:, DA_WIDTH:3 * DA_WIDTH], kg, gmat)

    attn = _attn_call(lam, q, k, kc, v, vc, subln_g[0][:, None],
                      tq=min(ATTN_Q_TILE, l), ck=min(ATTN_KEY_CHUNK, ctx.shape[1]),
                      out_scale=1.0 - lambda_init)

    wr = w_router[0].T.astype(BF)
    rb = router_bias[0][:, None]
    tok = jnp.arange(tm)
    tri = (tok[:, None] < tok[None, :]).astype(BF)
    y0, fx, gt, pt, counts = _outproj_call(
        x, attn, gm, mod, w_out[0].astype(BF), norm_ffn_g, wr, rb,
        ws_gate[0].astype(BF), ws_up[0].astype(BF), ws_down[0].astype(BF), tri, tm)

    totals = jnp.sum(counts.reshape(b, -1, N_EXPERTS), axis=1)
    order = jnp.argsort(-totals, axis=1).astype(jnp.int32).reshape(-1)
    return _moe_call(counts.reshape(-1), order, fx, gt, pt,
                     we_gate[0].astype(BF), we_up[0].astype(BF), we_down[0].astype(BF),
                     y0, mod, tt=min(MOE_TILE, l), sub=tm, caps=MOE_CAPS, epg=MOE_EXPERTS_PER_STEP)
```

```python
import functools
import math

import jax
import jax.numpy as jnp
from jax import lax
from jax.experimental import pallas as pl
from jax.experimental.pallas import tpu as pltpu

BF = jnp.bfloat16
F32 = jnp.float32

EPS = 1e-6
GRID_W = 64
DA_HEADS = 4
DA_HEAD_DIM = 64
DA_V_DIM = 128
DA_WIDTH = 512
GM_HEADS = 4
GM_HEAD_DIM = 128
GM_WIDTH = 512
CHUNK = 128
ROPE_THETA = 10000.0
Q_SCALE = DA_HEAD_DIM ** -0.5 * math.log2(math.e)
N_EXPERTS = 32
TOP_K = 4
N_GROUPS = 4
TOPK_GROUPS = 2
GROUP_SIZE = N_EXPERTS // N_GROUPS
ROUTED_SCALE = 2.5
ATTN_Q_TILE = 256
ATTN_KEY_CHUNK = 256
INPROJ_TILE = 512
ROUTE_TILE = 512
MOE_TILE = 2048
MOE_CAPS = (48, 64, 80, 96, 128, 192, 256)
MOE_EXPERTS_PER_STEP = 2
ADA_BLOCK = 1024
COND_ROWS = 16
LANES = 128
VMEM_LIMIT = 56 * 1024 * 1024


def _sigmoid(x):
    return 1.0 / (1.0 + jnp.exp(-x))


def _dot(a, b):
    return jnp.dot(a, b, preferred_element_type=F32)


def _dot_nt(a, b):
    return lax.dot_general(a, b, (((1,), (1,)), ((), ())), preferred_element_type=F32)


def _rms_rows(x, g):
    ms = jnp.mean(x * x, axis=-1, keepdims=True)
    return x * lax.rsqrt(ms + EPS) * g


def _group_mean_sq(y, gmat):
    y2 = y * y
    return _dot(y2.astype(BF), gmat) * (1.0 / DA_HEAD_DIM)


def _swap_pairs(y):
    lane = lax.broadcasted_iota(jnp.int32, y.shape, 1)
    nxt = pltpu.roll(y, LANES - 1, 1)
    prv = pltpu.roll(y, 1, 1)
    return jnp.where((lane & 1) == 0, nxt, prv)


def _ada_kernel(cond_ref, w_ref, b_ref, o_ref):
    c = cond_ref[...]
    s = c * _sigmoid(c)
    o_ref[...] = _dot(s.astype(BF), w_ref[...].astype(BF)) + b_ref[...]


def _ada_call(cond, w_ada, b_ada):
    rows, d = cond.shape
    n = w_ada.shape[1]
    bn = ADA_BLOCK
    return pl.pallas_call(
        _ada_kernel,
        out_shape=jax.ShapeDtypeStruct((rows, n), F32),
        grid=(n // bn,),
        in_specs=[pl.BlockSpec((rows, d), lambda j: (0, 0)),
                  pl.BlockSpec((d, bn), lambda j: (0, j)),
                  pl.BlockSpec((1, bn), lambda j: (0, j))],
        out_specs=pl.BlockSpec((rows, bn), lambda j: (0, j)),
        compiler_params=pltpu.CompilerParams(dimension_semantics=("arbitrary",),
                                             vmem_limit_bytes=VMEM_LIMIT),
        name="ada",
    )(cond, w_ada, b_ada)


def _inproj_kernel(x_ref, mod_ref, ng_ref, w_ref, qg_ref, kg_ref, cos_ref, sin_ref, gmat_ref,
                   lng_ref, lnb_ref, ws_ref, bs_ref, og_ref,
                   q_ref, k_ref, v_ref, gm_ref):
    tm = x_ref.shape[1]
    x = x_ref[0]
    mod = mod_ref[0]
    h = (_rms_rows(x, ng_ref[...]) * (1.0 + mod[1:2]) + mod[0:1]).astype(BF)

    def proj(col0, width=2 * LANES):
        return _dot(h, w_ref[:, col0:col0 + width])

    def gelu(t):
        return 0.5 * t * (1.0 + lax.erf(t * math.sqrt(0.5)))

    gmat = gmat_ref[...]
    cos = cos_ref[...]
    sin = sin_ref[...]
    for jb in range(DA_HEADS // 2):
        pq = proj(jb * 2 * LANES)
        pk = proj(DA_WIDTH + jb * 2 * LANES)
        for jj in range(2):
            sl = slice((2 * jb + jj) * LANES, (2 * jb + jj + 1) * LANES)
            qj = pq[:, jj * LANES:(jj + 1) * LANES]
            qn = qj * lax.rsqrt(_group_mean_sq(qj, gmat) + EPS) * qg_ref[...]
            qr = qn * cos + _swap_pairs(qn) * sin
            q_ref[0, :, sl] = (qr * Q_SCALE).astype(BF)
            kj = pk[:, jj * LANES:(jj + 1) * LANES]
            kn = kj * lax.rsqrt(_group_mean_sq(kj, gmat) + EPS) * kg_ref[...]
            kr = kn * cos + _swap_pairs(kn) * sin
            k_ref[0, :, sl] = kr.astype(BF)
    v_ref[0] = proj(2 * DA_WIDTH, DA_WIDTH).astype(BF)

    for g in range(GM_HEADS):
        sl = slice(g * LANES, (g + 1) * LANES)
        if g % 2 == 0:
            zu = gelu(proj(3 * DA_WIDTH + g * LANES))
            zv = gelu(proj(3 * DA_WIDTH + GM_WIDTH + g * LANES))
        u = zu[:, (g % 2) * LANES:(g % 2 + 1) * LANES]
        vg = zv[:, (g % 2) * LANES:(g % 2 + 1) * LANES]
        mu = jnp.mean(vg, axis=-1, keepdims=True)
        xc = vg - mu
        var = jnp.mean(xc * xc, axis=-1, keepdims=True)
        vn = (xc * lax.rsqrt(var + EPS) * lng_ref[:, sl] + lnb_ref[:, sl]).astype(BF)
        for cidx in range(tm // CHUNK):
            rows = slice(cidx * CHUNK, (cidx + 1) * CHUNK)
            mixed = _dot(ws_ref[g], vn[rows]) + bs_ref[g]
            y = u[rows] * mixed
            gm_ref[0, rows, sl] = _rms_rows(y, og_ref[:, sl]).astype(BF)


def _inproj_call(x, mod, ng, w_in, qg, kg, cos, sin, gmat, lng, lnb, ws, bs, og, tm):
    b, l, d = x.shape
    nw = w_in.shape[1]
    full = lambda shape: pl.BlockSpec(shape, lambda bi, i: (0,) * len(shape))
    tok = pl.BlockSpec((1, tm, DA_WIDTH), lambda bi, i: (bi, i, 0))
    out = jax.ShapeDtypeStruct((b, l, DA_WIDTH), BF)
    return pl.pallas_call(
        _inproj_kernel,
        out_shape=(out, out, out, out),
        grid=(b, l // tm),
        in_specs=[pl.BlockSpec((1, tm, d), lambda bi, i: (bi, i, 0)),
                  pl.BlockSpec((1, 6, d), lambda bi, i: (bi, 0, 0)),
                  full((1, d)), full((d, nw)), full((1, LANES)), full((1, LANES)),
                  pl.BlockSpec((tm, LANES), lambda bi, i: (i, 0)),
                  pl.BlockSpec((tm, LANES), lambda bi, i: (i, 0)),
                  full((LANES, LANES)), full((1, GM_WIDTH)), full((1, GM_WIDTH)),
                  full((GM_HEADS, CHUNK, CHUNK)), full((GM_HEADS, CHUNK, GM_HEAD_DIM)),
                  full((1, GM_WIDTH))],
        out_specs=(tok, tok, tok, tok),
        compiler_params=pltpu.CompilerParams(dimension_semantics=("arbitrary", "arbitrary"),
                                             vmem_limit_bytes=VMEM_LIMIT),
        name="inproj",
    )(x, mod, ng, w_in, qg, kg, cos, sin, gmat, lng, lnb, ws, bs, og)


def _ctxproj_kernel(x_ref, mod_ref, ng_ref, w_ref, kg_ref, gmat_ref, k_ref, v_ref):
    x = x_ref[0]
    mod = mod_ref[0]
    h = _rms_rows(x, ng_ref[...]) * (1.0 + mod[1:2]) + mod[0:1]
    p = _dot(h.astype(BF), w_ref[...])
    gmat = gmat_ref[...]
    for j in range(DA_HEADS):
        sl = slice(j * LANES, (j + 1) * LANES)
        kj = p[:, sl]
        kn = kj * lax.rsqrt(_group_mean_sq(kj, gmat) + EPS) * kg_ref[...]
        k_ref[0, :, sl] = kn.astype(BF)
    v_ref[0] = p[:, DA_WIDTH:].astype(BF)


def _ctxproj_call(ctx, mod_ctx, ng, w_kv, kg, gmat):
    b, lc, d = ctx.shape
    full = lambda shape: pl.BlockSpec(shape, lambda bi: (0,) * len(shape))
    tok = pl.BlockSpec((1, lc, DA_WIDTH), lambda bi: (bi, 0, 0))
    out = jax.ShapeDtypeStruct((b, lc, DA_WIDTH), BF)
    return pl.pallas_call(
        _ctxproj_kernel,
        out_shape=(out, out),
        grid=(b,),
        in_specs=[pl.BlockSpec((1, lc, d), lambda bi: (bi, 0, 0)),
                  full((1, 6, d)), full((1, d)), full((d, 2 * DA_WIDTH)),
                  full((1, LANES)), full((LANES, LANES))],
        out_specs=(tok, tok),
        compiler_params=pltpu.CompilerParams(dimension_semantics=("arbitrary",),
                                             vmem_limit_bytes=VMEM_LIMIT),
        name="ctxproj",
    )(ctx, mod_ctx, ng, w_kv, kg, gmat)


def _attn_kernel(lam_ref, q_ref, kx_ref, kc_ref, vx_ref, vc_ref, sg_ref, o_ref,
                 k_scr, vt_scr, s_scr, m_scr, *, ck, out_scale):
    tq = q_ref.shape[1]
    lc = kc_ref.shape[1]
    lx = kx_ref.shape[1]
    nck = (lc + lx) // ck

    step = pl.program_id(2)
    last = pl.num_programs(2) - 1

    def stacked_queries():
        q = q_ref[0]
        lane = lax.broadcasted_iota(jnp.int32, q.shape, 1)
        zero = jnp.zeros_like(q)
        return jnp.concatenate([jnp.where(lane < DA_HEAD_DIM, q, zero),
                                jnp.where(lane >= DA_HEAD_DIM, q, zero)], axis=0)

    def score_chunk(qq, cidx, mrun):
        rows = slice(cidx * ck, (cidx + 1) * ck)
        st = _dot_nt(k_scr[rows, :], qq)
        s_scr[rows, :] = st
        return jnp.maximum(mrun, jnp.max(st.reshape(ck // 8, 8, 2 * tq), axis=0))

    def weight_chunk(m, cidx, acc, lrun):
        rows = slice(cidx * ck, (cidx + 1) * ck)
        pt = jnp.exp2(s_scr[rows, :] - m)
        acc = acc + _dot(vt_scr[cidx], pt.astype(BF))
        return acc, lrun + jnp.sum(pt.reshape(ck // 8, 8, 2 * tq), axis=0)

    def finish(acc, lrun):
        r = 1.0 / jnp.sum(lrun, axis=0, keepdims=True)
        ot = acc[:, :tq] * r[:, :tq] - lam_ref[0] * (acc[:, tq:] * r[:, tq:])
        ms = jnp.mean(ot * ot, axis=0, keepdims=True)
        on = ot * lax.rsqrt(ms + EPS) * sg_ref[...] * out_scale
        o_ref[0] = on.T.astype(BF)

    mrun0 = jnp.full((8, 2 * tq), -jnp.inf, F32)
    lrun0 = jnp.zeros((8, 2 * tq), F32)
    acc0 = jnp.zeros((DA_V_DIM, 2 * tq), F32)

    @pl.when(step == 0)
    def _first():
        k_scr[0:lc, :] = kc_ref[0]
        k_scr[lc:lc + lx, :] = kx_ref[0]
        for cidx in range(nck):
            lo = cidx * ck
            if lo < lc:
                blk = vc_ref[0, lo:lo + ck, :]
            else:
                blk = vx_ref[0, lo - lc:lo - lc + ck, :]
            vt_scr[cidx] = blk.astype(F32).T.astype(BF)
        qq = stacked_queries()
        mrun = mrun0
        for cidx in range(nck):
            mrun = score_chunk(qq, cidx, mrun)
        m_scr[...] = mrun

    @pl.when((step > 0) & (step < last))
    def _steady():
        qq = stacked_queries()
        m = jnp.max(m_scr[...], axis=0, keepdims=True)
        mrun, lrun, acc = mrun0, lrun0, acc0
        for cidx in range(nck):
            acc, lrun = weight_chunk(m, cidx, acc, lrun)
            mrun = score_chunk(qq, cidx, mrun)
        finish(acc, lrun)
        m_scr[...] = mrun

    @pl.when(step == last)
    def _last():
        m = jnp.max(m_scr[...], axis=0, keepdims=True)
        lrun, acc = lrun0, acc0
        for cidx in range(nck):
            acc, lrun = weight_chunk(m, cidx, acc, lrun)
        finish(acc, lrun)


def _attn_call(lam, q, kx, kc, vx, vc, subln_col, tq, ck, out_scale):
    b, l, _ = q.shape
    lc = kc.shape[1]
    lk = lc + l
    assert lk % ck == 0 and lc % ck == 0 and l % tq == 0
    nq = l // tq
    kern = functools.partial(_attn_kernel, ck=ck, out_scale=out_scale)
    return pl.pallas_call(
        kern,
        out_shape=jax.ShapeDtypeStruct((b, l, DA_WIDTH), BF),
        grid=(b, DA_HEADS, nq + 1),
        in_specs=[pl.BlockSpec(memory_space=pltpu.SMEM),
                  pl.BlockSpec((1, tq, LANES), lambda bi, h, i: (bi, jnp.minimum(i, nq - 1), h)),
                  pl.BlockSpec((1, l, LANES), lambda bi, h, i: (bi, 0, h)),
                  pl.BlockSpec((1, lc, LANES), lambda bi, h, i: (bi, 0, h)),
                  pl.BlockSpec((1, l, LANES), lambda bi, h, i: (bi, 0, h)),
                  pl.BlockSpec((1, lc, LANES), lambda bi, h, i: (bi, 0, h)),
                  pl.BlockSpec((DA_V_DIM, 1), lambda bi, h, i: (0, 0))],
        out_specs=pl.BlockSpec((1, tq, LANES), lambda bi, h, i: (bi, jnp.maximum(i - 1, 0), h)),
        scratch_shapes=[pltpu.VMEM((lk, LANES), BF),
                        pltpu.VMEM((lk // ck, DA_V_DIM, ck), BF),
                        pltpu.VMEM((lk, 2 * tq), F32),
                        pltpu.VMEM((8, 2 * tq), F32)],
        compiler_params=pltpu.CompilerParams(
            dimension_semantics=("arbitrary", "arbitrary", "arbitrary"),
            vmem_limit_bytes=VMEM_LIMIT),
        name="attn",
    )(lam, q, kx, kc, vx, vc, subln_col)


def _route(scores, rbias):
    tm = scores.shape[1]
    assert GROUP_SIZE == 8 and scores.shape[0] == N_EXPERTS
    row = lax.broadcasted_iota(jnp.int32, (GROUP_SIZE, tm), 0).astype(F32)
    neg = jnp.full((GROUP_SIZE, tm), -jnp.inf, F32)
    biased = scores + rbias
    groups = [slice(g * GROUP_SIZE, (g + 1) * GROUP_SIZE) for g in range(N_GROUPS)]
    vals = [biased[sl] for sl in groups]

    gscore = []
    for v in vals:
        m1 = jnp.max(v, axis=0, keepdims=True)
        i1 = jnp.min(jnp.where(v == m1, row, 1e9), axis=0, keepdims=True)
        m2 = jnp.max(jnp.where(row == i1, neg, v), axis=0, keepdims=True)
        gscore.append(m1 + m2)

    cur = []
    for g in range(N_GROUPS):
        beaten = jnp.zeros_like(gscore[g])
        for g2 in range(N_GROUPS):
            if g2 == g:
                continue
            beat = (gscore[g2] >= gscore[g]) if g2 < g else (gscore[g2] > gscore[g])
            beaten = beaten + jnp.where(beat, 1.0, 0.0)
        cur.append(jnp.where(beaten < TOPK_GROUPS, vals[g], neg))

    ids = [row + float(g * GROUP_SIZE) for g in range(N_GROUPS)]
    sel = [jnp.zeros((GROUP_SIZE, tm), F32) for _ in range(N_GROUPS)]
    for _ in range(TOP_K):
        best = functools.reduce(jnp.maximum, cur)
        mx = jnp.max(best, axis=0, keepdims=True)
        cand = functools.reduce(jnp.minimum, [jnp.where(c == mx, i, 1e9) for c, i in zip(cur, ids)])
        idx = jnp.min(cand, axis=0, keepdims=True)
        hits = [i == idx for i in ids]
        sel = [jnp.where(h, 1.0, s) for h, s in zip(hits, sel)]
        cur = [jnp.where(h, neg, c) for h, c in zip(hits, cur)]
    w = [s * scores[sl] for s, sl in zip(sel, groups)]
    total = jnp.sum(functools.reduce(lambda a, b: a + b, w), axis=0, keepdims=True)
    gates = [wg / total * ROUTED_SCALE for wg in w]
    return jnp.concatenate(gates, axis=0), jnp.concatenate(sel, axis=0)


def _outproj_kernel(x_ref, at_ref, gm_ref, mod_ref, wo_ref, nfg_ref, wr_ref, rb_ref,
                    wsg_ref, wsu_ref, wsd_ref, tri_ref,
                    y0_ref, fx_ref, gt_ref, pt_ref, cnt_ref):
    mod = mod_ref[0]
    mix = _dot(at_ref[0], wo_ref[0:DA_WIDTH, :]) + _dot(gm_ref[0], wo_ref[DA_WIDTH:, :])
    x1 = x_ref[0] + mod[2:3] * mix
    fx = (_rms_rows(x1, nfg_ref[...]) * (1.0 + mod[4:5]) + mod[3:4]).astype(BF)
    fx_ref[0] = fx
    scores = _sigmoid(_dot_nt(wr_ref[...], fx))
    gates, sel = _route(scores, rb_ref[...])
    rank = _dot(sel.astype(BF), tri_ref[...])
    gt_ref[0] = gates
    pt_ref[0] = jnp.where(sel > 0.5, rank, -1.0)
    cnt_ref[0] = jnp.sum(sel, axis=1, keepdims=True).astype(jnp.int32)
    sg = _dot(fx, wsg_ref[...])
    su = _dot(fx, wsu_ref[...])
    hs = (sg * _sigmoid(sg)) * su
    y0_ref[0] = x1 + mod[5:6] * _dot(hs.astype(BF), wsd_ref[...])


def _outproj_call(x, attn, gm, mod, w_out, nfg, wr, rb, wsg, wsu, wsd, tri, tm):
    b, l, d = x.shape
    nt = l // tm
    full = lambda shape: pl.BlockSpec(shape, lambda bi, i: (0,) * len(shape))
    tokd = pl.BlockSpec((1, tm, d), lambda bi, i: (bi, i, 0))
    tokh = pl.BlockSpec((1, tm, DA_WIDTH), lambda bi, i: (bi, i, 0))
    expt = pl.BlockSpec((1, N_EXPERTS, tm), lambda bi, i: (bi, 0, i))
    ds = wsg.shape[1]
    return pl.pallas_call(
        _outproj_kernel,
        out_shape=(jax.ShapeDtypeStruct((b, l, d), F32),
                   jax.ShapeDtypeStruct((b, l, d), BF),
                   jax.ShapeDtypeStruct((b, N_EXPERTS, l), F32),
                   jax.ShapeDtypeStruct((b, N_EXPERTS, l), F32),
                   jax.ShapeDtypeStruct((b * nt, N_EXPERTS, 1), jnp.int32)),
        grid=(b, nt),
        in_specs=[tokd, tokh, tokh,
                  pl.BlockSpec((1, 6, d), lambda bi, i: (bi, 0, 0)),
                  full((d, d)), full((1, d)), full((N_EXPERTS, d)), full((N_EXPERTS, 1)),
                  full((d, ds)), full((d, ds)), full((ds, d)), full((tm, tm))],
        out_specs=(tokd, tokd, expt, expt,
                   pl.BlockSpec((1, N_EXPERTS, 1), lambda bi, i: (bi * nt + i, 0, 0))),
        compiler_params=pltpu.CompilerParams(dimension_semantics=("arbitrary", "arbitrary"),
                                             vmem_limit_bytes=VMEM_LIMIT),
        name="outproj",
    )(x, attn, gm, mod, w_out, nfg, wr, rb, wsg, wsu, wsd, tri)


def _moe_kernel(cnt_ref, order_ref, fx_ref, gt_ref, pt_ref, *refs, sub, caps, epg):
    wg_ref, wu_ref, wd_ref = refs[0:epg], refs[epg:2 * epg], refs[2 * epg:3 * epg]
    y0_ref, mod_ref, o_ref = refs[3 * epg:]
    tt = fx_ref.shape[1]
    nsub = tt // sub
    pair = pl.program_id(2)
    experts = [order_ref[pl.program_id(0) * N_EXPERTS + pair * epg + k] for k in range(epg)]
    sub0 = (pl.program_id(0) * pl.num_programs(1) + pl.program_id(1)) * nsub

    @pl.when(pair == 0)
    def _zero():
        o_ref[...] = jnp.zeros_like(o_ref)

    cmax = jnp.int32(0)
    for s in range(nsub):
        for k in range(epg):
            cmax = jnp.maximum(cmax, cnt_ref[(sub0 + s) * N_EXPERTS + experts[k]])

    def do_round(r, cap):
        slot = lax.broadcasted_iota(jnp.int32, (cap, sub), 0).astype(F32) + (r * cap).astype(F32)
        picks, xrows, grows = [], [], []
        for s in range(nsub):
            cols = slice(s * sub, (s + 1) * sub)
            hit = [pt_ref[0, pl.ds(experts[k], 1), cols] == slot for k in range(epg)]
            grows.append([jnp.sum(jnp.where(hit[k], gt_ref[0, pl.ds(experts[k], 1), cols], 0.0),
                                  axis=-1, keepdims=True) for k in range(epg)])
            pick = jnp.concatenate([jnp.where(h, 1.0, 0.0) for h in hit], axis=0).astype(BF)
            picks.append(pick)
            xrows.append(_dot(pick, fx_ref[0, cols, :]).astype(BF))
        outs = []
        for k in range(epg):
            xk = jnp.concatenate([xrows[s][k * cap:(k + 1) * cap] for s in range(nsub)], axis=0)
            gk = jnp.concatenate([grows[s][k] for s in range(nsub)], axis=0)
            a = _dot(xk, wg_ref[k][0])
            bb = _dot(xk, wu_ref[k][0])
            hm = (a * _sigmoid(a)) * bb * gk
            outs.append(_dot(hm.astype(BF), wd_ref[k][0]).astype(BF))
        for s in range(nsub):
            cols = slice(s * sub, (s + 1) * sub)
            stacked = jnp.concatenate([outs[k][s * cap:(s + 1) * cap] for k in range(epg)], axis=0)
            o_ref[0, cols, :] += lax.dot_general(picks[s], stacked, (((0,), (0,)), ((), ())),
                                                 preferred_element_type=F32)

    below = 0
    for cap in caps:
        fits = (cmax <= cap) if below == 0 else ((cmax > below) & (cmax <= cap))
        pl.when(fits)(functools.partial(do_round, jnp.int32(0), cap))
        below = cap

    @pl.when(cmax > caps[-1])
    def _many_rounds():
        def body(r, carry):
            do_round(r, caps[-1])
            return carry
        lax.fori_loop(0, (cmax + (caps[-1] - 1)) // caps[-1], body, 0)

    @pl.when(pair == pl.num_programs(2) - 1)
    def _finish():
        o_ref[0] = y0_ref[0] + mod_ref[0][5:6] * o_ref[0]


def _moe_call(counts, order, fx, gt, pt, wg, wu, wd, y0, mod, tt, sub, caps, epg):
    b, l, d = fx.shape
    ne, _, de = wg.shape
    assert l % tt == 0 and tt % sub == 0 and ne % epg == 0
    assert all(c % 16 == 0 for c in caps) and list(caps) == sorted(caps)
    tokd = pl.BlockSpec((1, tt, d), lambda bi, i, p, cnt, order: (bi, i, 0))
    expt = pl.BlockSpec((1, ne, tt), lambda bi, i, p, cnt, order: (bi, 0, i))

    def expert_block(shape, k):
        return pl.BlockSpec((1,) + shape,
                            lambda bi, i, p, cnt, order: (order[bi * ne + p * epg + k], 0, 0))

    kern = functools.partial(_moe_kernel, sub=sub, caps=tuple(caps), epg=epg)
    grid_spec = pltpu.PrefetchScalarGridSpec(
        num_scalar_prefetch=2,
        grid=(b, l // tt, ne // epg),
        in_specs=([tokd, expt, expt]
                  + [expert_block((d, de), k) for k in range(epg)]
                  + [expert_block((d, de), k) for k in range(epg)]
                  + [expert_block((de, d), k) for k in range(epg)]
                  + [tokd, pl.BlockSpec((1, 6, d), lambda bi, i, p, cnt, order: (bi, 0, 0))]),
        out_specs=tokd)
    return pl.pallas_call(
        kern,
        out_shape=jax.ShapeDtypeStruct((b, l, d), F32),
        grid_spec=grid_spec,
        compiler_params=pltpu.CompilerParams(
            dimension_semantics=("arbitrary", "arbitrary", "arbitrary"),
            vmem_limit_bytes=VMEM_LIMIT),
        name="moe",
    )(counts, order, fx, gt, pt, *([wg] * epg), *([wu] * epg), *([wd] * epg), y0, mod)


def _rope_tables(n_tokens):
    rows = n_tokens // GRID_W
    row = jnp.repeat(jnp.arange(rows, dtype=F32), GRID_W)
    col = jnp.tile(jnp.arange(GRID_W, dtype=F32), rows)
    half = DA_HEAD_DIM // 2
    inv_freq = ROPE_THETA ** (-jnp.arange(0, half, 2, dtype=F32) / half)
    ang = jnp.concatenate([row[:, None] * inv_freq, col[:, None] * inv_freq], axis=-1)
    cos, sin = jnp.cos(ang), jnp.sin(ang)
    cos64 = jnp.repeat(cos, 2, axis=-1)
    sin64 = jnp.stack([-sin, sin], axis=-1).reshape(n_tokens, DA_HEAD_DIM)
    return jnp.tile(cos64, (1, 2)), jnp.tile(sin64, (1, 2))


def kernel(x, c, ctx, c_ctx, w_ada, b_ada, norm_mix_g, w_in, q_norm_g, k_norm_g, da_lambda, subln_g, gm_ln_g, gm_ln_b, gm_ws, gm_bs, gm_out_g, w_out, norm_ffn_g, w_router, router_bias, we_gate, we_up, we_down, ws_gate, ws_up, ws_down):
    assert w_ada.shape[0] == 1, "single-layer kernel"
    b, l, d = x.shape
    lambda_init = 0.8 - 0.6 * math.exp(-0.3 * 0)
    lp = da_lambda[0].astype(F32)
    lam = (jnp.exp(jnp.sum(lp[0] * lp[1])) - jnp.exp(jnp.sum(lp[2] * lp[3])) + lambda_init).reshape(1)

    assert b + 1 <= COND_ROWS
    cond = jnp.zeros((COND_ROWS, d), F32).at[:b].set(c).at[b].set(c_ctx)
    ada = _ada_call(cond, w_ada[0], b_ada[0][None, :]).reshape(COND_ROWS, 6, d)
    mod = ada[:b]
    mod_ctx = ada[b:b + 1]

    cos, sin = _rope_tables(l)
    half = jnp.arange(LANES) // DA_HEAD_DIM
    gmat = (half[:, None] == half[None, :]).astype(BF)
    qg = jnp.tile(q_norm_g[0], 2)[None, :]
    kg = jnp.tile(k_norm_g[0], 2)[None, :]
    w_in_bf = w_in[0].astype(BF)
    bs_full = jnp.broadcast_to(gm_bs[0][:, :, None], (GM_HEADS, CHUNK, GM_HEAD_DIM))

    tm = min(ROUTE_TILE, l)
    assert l % GRID_W == 0 and l % tm == 0 and l % min(INPROJ_TILE, l) == 0 and tm % CHUNK == 0
    q, k, v, gm = _inproj_call(
        x, mod, norm_mix_g, w_in_bf, qg, kg, cos, sin, gmat,
        gm_ln_g[0].reshape(1, GM_WIDTH), gm_ln_b[0].reshape(1, GM_WIDTH),
        gm_ws[0].astype(BF), bs_full, gm_out_g[0].reshape(1, GM_WIDTH), min(INPROJ_TILE, l))
    kc, vc = _ctxproj_call(ctx, mod_ctx, norm_mix_g, w_in_bf[:, DA_WIDTH:3 * DA_WIDTH], kg, gmat)

    attn = _attn_call(lam, q, k, kc, v, vc, subln_g[0][:, None],
                      tq=min(ATTN_Q_TILE, l), ck=min(ATTN_KEY_CHUNK, ctx.shape[1]),
                      out_scale=1.0 - lambda_init)

    wr = w_router[0].T.astype(BF)
    rb = router_bias[0][:, None]
    tok = jnp.arange(tm)
    tri = (tok[:, None] < tok[None, :]).astype(BF)
    y0, fx, gt, pt, counts = _outproj_call(
        x, attn, gm, mod, w_out[0].astype(BF), norm_ffn_g, wr, rb,
        ws_gate[0].astype(BF), ws_up[0].astype(BF), ws_down[0].astype(BF), tri, tm)

    totals = jnp.sum(counts.reshape(b, -1, N_EXPERTS), axis=1)
    order = jnp.argsort(-totals, axis=1).astype(jnp.int32).reshape(-1)
    return _moe_call(counts.reshape(-1), order, fx, gt, pt,
                     we_gate[0].astype(BF), we_up[0].astype(BF), we_down[0].astype(BF),
                     y0, mod, tt=min(MOE_TILE, l), sub=tm, caps=MOE_CAPS, epg=MOE_EXPERTS_PER_STEP)
```
